```python
import jax, jax.numpy as jnp
from jax import lax
import numpy as np

D_MODEL = 1024
BATCH = 4
SEQ = 4096
DEPTH = 4

ML_WIDTH = D_MODEL // 2
ML_HEADS = 4
ML_HEAD_DIM = ML_WIDTH // ML_HEADS
ML_CHUNK = 128
G_WIDTH = D_MODEL // 4
G_HEADS = 4
G_HEAD_DIM = G_WIDTH // G_HEADS
G_CHUNK = 128
C_WIDTH = D_MODEL // 4
C_GROUPS = 4
CONV_WIDTH = 31
MIX_WIDTH = ML_WIDTH + G_WIDTH + C_WIDTH
P_IN = 4 * ML_WIDTH + 2 * ML_HEADS + 2 * G_WIDTH + 2 * C_WIDTH
N_EXPERTS = 16
N_GROUPS = 4
EXPERTS_PER_GROUP = N_EXPERTS // N_GROUPS
TOP_K = 2
D_EXPERT = D_MODEL // 2
DEEPNORM_ALPHA = (2.0 * DEPTH) ** 0.25
DEEPNORM_BETA = (8.0 * DEPTH) ** -0.25
LN_EPS = 1e-5

kernel_name = "hybrid_mlstm_gmlp_conv_groupmoe_deepnorm"


def layer_norm(x, w, b, eps=LN_EPS):
    xf = x.astype(jnp.float32)
    mu = xf.mean(-1, keepdims=True)
    var = jnp.square(xf - mu).mean(-1, keepdims=True)
    return ((xf - mu) * lax.rsqrt(var + eps) * w.astype(jnp.float32) + b.astype(jnp.float32)).astype(x.dtype)


def mlstm_chunkwise(q, k, v, i_pre, f_pre):
    B_, S_, H_, Dh = q.shape
    L = ML_CHUNK
    NC = S_ // L
    f32 = jnp.float32

    def chunks(t):
        return t.astype(f32).reshape(B_, NC, L, H_, -1).transpose(0, 3, 1, 2, 4)

    q_c = chunks(q)
    k_c = chunks(k) * (Dh ** -0.5)
    v_c = chunks(v)
    logi = i_pre.astype(f32).reshape(B_, NC, L, H_).transpose(0, 3, 1, 2)
    logf = jax.nn.log_sigmoid(f_pre.astype(f32)).reshape(B_, NC, L, H_).transpose(0, 3, 1, 2)
    b = jnp.cumsum(logf, axis=-1)
    g = b[..., -1]
    a = g[..., None] - b + logi

    def step(carry, xs):
        C, n, m = carry
        k_j, v_j, a_j, g_j = xs
        m_new = jnp.maximum(g_j + m, a_j.max(-1))
        w = jnp.exp(a_j - m_new[..., None])
        decay = jnp.exp(g_j + m - m_new)
        C_new = decay[..., None, None] * C + jnp.einsum('bhl,bhld,bhle->bhde', w, k_j, v_j)
        n_new = decay[..., None] * n + jnp.einsum('bhl,bhld->bhd', w, k_j)
        return (C_new, n_new, m_new), (C, n, m)

    init = (jnp.zeros((B_, H_, Dh, Dh), f32), jnp.zeros((B_, H_, Dh), f32), jnp.zeros((B_, H_), f32))
    xs = (jnp.moveaxis(k_c, 2, 0), jnp.moveaxis(v_c, 2, 0), jnp.moveaxis(a, 2, 0), jnp.moveaxis(g, 2, 0))
    _, (C_prev, n_prev, m_prev) = lax.scan(step, init, xs)
    C_prev = jnp.moveaxis(C_prev, 0, 2)
    n_prev = jnp.moveaxis(n_prev, 0, 2)
    m_prev = jnp.moveaxis(m_prev, 0, 2)

    causal = jnp.tril(jnp.ones((L, L), bool))
    log_d = jnp.where(causal, b[..., :, None] - b[..., None, :] + logi[..., None, :], -jnp.inf)
    log_inter = b + m_prev[..., None]
    m_row = jnp.maximum(log_inter, log_d.max(-1))
    p = jnp.einsum('bhcld,bhcsd->bhcls', q_c, k_c) * jnp.exp(log_d - m_row[..., None])
    w_inter = jnp.exp(log_inter - m_row)
    num = jnp.einsum('bhcls,bhcsd->bhcld', p, v_c) + w_inter[..., None] * jnp.einsum('bhcld,bhcde->bhcle', q_c, C_prev)
    den = p.sum(-1) + w_inter * jnp.einsum('bhcld,bhcd->bhcl', q_c, n_prev)
    h = num / jnp.maximum(jnp.abs(den), jnp.exp(-m_row))[..., None]
    return h.transpose(0, 2, 3, 1, 4).reshape(B_, S_, H_ * Dh)


def mixer(x, w_in, b_in, mlstm_norm_w, gmlp_norm_w, gmlp_norm_b, gmlp_ws, gmlp_bs,
          conv_w, conv_b, conv_norm_w, conv_norm_b, w_out):
    B_, S_, _ = x.shape
    proj = x @ w_in + b_in
    sizes = (ML_WIDTH,) * 4 + (ML_HEADS,) * 2 + (G_WIDTH,) * 2 + (C_WIDTH,) * 2
    q, k, v, o, i_pre, f_pre, gu, gv, ca, cb = jnp.split(proj, np.cumsum(sizes)[:-1].tolist(), axis=-1)

    heads = lambda t: t.reshape(B_, S_, ML_HEADS, ML_HEAD_DIM)
    h = mlstm_chunkwise(heads(q), heads(k), heads(v), i_pre, f_pre)
    hh = h.reshape(B_, S_, ML_HEADS, ML_HEAD_DIM)
    mu = hh.mean(-1, keepdims=True)
    var = jnp.square(hh - mu).mean(-1, keepdims=True)
    hh = ((hh - mu) * lax.rsqrt(var + 1e-6)).reshape(B_, S_, ML_WIDTH) * mlstm_norm_w.astype(jnp.float32)
    h_a = (jax.nn.sigmoid(o.astype(jnp.float32)) * hh).astype(x.dtype)

    u = jax.nn.gelu(gu)
    z = layer_norm(jax.nn.gelu(gv), gmlp_norm_w, gmlp_norm_b)
    z = z.reshape(B_, S_ // G_CHUNK, G_CHUNK, G_HEADS, G_HEAD_DIM)
    ws = gmlp_ws * jnp.tril(jnp.ones((G_CHUNK, G_CHUNK), gmlp_ws.dtype))
    z = jnp.einsum('hts,bcshd->bcthd', ws, z) + gmlp_bs.T[:, :, None]
    h_b = u * z.reshape(B_, S_, G_WIDTH)

    c = ca * jax.nn.sigmoid(cb)
    c = lax.conv_general_dilated(c, conv_w[:, None, :], window_strides=(1,),
                                 padding=[(CONV_WIDTH - 1, 0)],
                                 dimension_numbers=('NWC', 'WIO', 'NWC'),
                                 feature_group_count=C_WIDTH) + conv_b
    h_c = jax.nn.silu(layer_norm(c, conv_norm_w, conv_norm_b))

    return jnp.concatenate([h_a, h_b, h_c], axis=-1) @ w_out


def moe(x, router_w, router_b, w_gate, w_up, w_down):
    B_, S_, D_ = x.shape
    xt = x.reshape(B_ * S_, D_)
    s = jax.nn.sigmoid(xt.astype(jnp.float32) @ router_w.astype(jnp.float32))
    sel = s + router_b.astype(jnp.float32)
    gscore = lax.top_k(sel.reshape(-1, N_GROUPS, EXPERTS_PER_GROUP), 2)[0].sum(-1)
    gidx = jnp.argmax(gscore, axis=-1)
    in_group = (jnp.arange(N_EXPERTS) // EXPERTS_PER_GROUP)[None, :] == gidx[:, None]
    _, eidx = lax.top_k(jnp.where(in_group, sel, -jnp.inf), TOP_K)
    gate = jnp.take_along_axis(s, eidx, axis=1)
    gate = gate / gate.sum(-1, keepdims=True)
    combine = (jax.nn.one_hot(eidx, N_EXPERTS, dtype=jnp.float32) * gate[..., None]).sum(1).astype(x.dtype)
    out = jnp.zeros_like(xt)
    for e in range(N_EXPERTS):
        hid = jax.nn.silu(xt @ w_gate[e]) * (xt @ w_up[e])
        out = out + combine[:, e:e + 1] * (hid @ w_down[e])
    return out.reshape(B_, S_, D_)


def setup_inputs(seed: int = 0) -> dict:
    key = jax.random.key(seed)
    ks = jax.random.split(key, 24)
    f32 = jnp.float32
    nrm = lambda k, shape, scale: jax.random.normal(k, shape, f32) * scale
    L = DEPTH
    x = nrm(ks[0], (BATCH, SEQ, D_MODEL), 1.0)
    v_lo = 2 * ML_WIDTH
    w_in = nrm(ks[1], (L, D_MODEL, P_IN), D_MODEL ** -0.5)
    w_in = w_in.at[:, :, v_lo:v_lo + ML_WIDTH].multiply(DEEPNORM_BETA)
    f_lo = 4 * ML_WIDTH + ML_HEADS
    b_in = nrm(ks[2], (L, P_IN), 0.02).at[:, f_lo:f_lo + ML_HEADS].add(jnp.linspace(3.0, 6.0, ML_HEADS))
    mlstm_norm_w = 1.0 + nrm(ks[3], (L, ML_WIDTH), 0.02)
    gmlp_norm_w = 1.0 + nrm(ks[4], (L, G_WIDTH), 0.02)
    gmlp_norm_b = nrm(ks[5], (L, G_WIDTH), 0.02)
    gmlp_ws = nrm(ks[6], (L, G_HEADS, G_CHUNK, G_CHUNK), G_CHUNK ** -0.5)
    gmlp_bs = 1.0 + nrm(ks[7], (L, G_HEADS, G_CHUNK), 0.02)
    conv_w = nrm(ks[8], (L, CONV_WIDTH, C_WIDTH), CONV_WIDTH ** -0.5)
    conv_b = nrm(ks[9], (L, C_WIDTH), 0.02)
    conv_norm_w = 1.0 + nrm(ks[10], (L, C_WIDTH), 0.02)
    conv_norm_b = nrm(ks[11], (L, C_WIDTH), 0.02)
    w_out = nrm(ks[12], (L, MIX_WIDTH, D_MODEL), MIX_WIDTH ** -0.5 * DEEPNORM_BETA)
    ln1_w = 1.0 + nrm(ks[13], (L, D_MODEL), 0.02)
    ln1_b = nrm(ks[14], (L, D_MODEL), 0.02)
    router_w = nrm(ks[15], (D_MODEL, N_EXPERTS), D_MODEL ** -0.5)
    router_b = nrm(ks[16], (N_EXPERTS,), 0.01)
    w_gate = nrm(ks[17], (L, N_EXPERTS, D_MODEL, D_EXPERT), D_MODEL ** -0.5)
    w_up = nrm(ks[18], (L, N_EXPERTS, D_MODEL, D_EXPERT), D_MODEL ** -0.5)
    w_down = nrm(ks[19], (L, N_EXPERTS, D_EXPERT, D_MODEL), D_EXPERT ** -0.5 * DEEPNORM_BETA)
    ln2_w = 1.0 + nrm(ks[20], (L, D_MODEL), 0.02)
    ln2_b = nrm(ks[21], (L, D_MODEL), 0.02)
    return {"x": x, "w_in": w_in, "b_in": b_in, "mlstm_norm_w": mlstm_norm_w,
            "gmlp_norm_w": gmlp_norm_w, "gmlp_norm_b": gmlp_norm_b, "gmlp_ws": gmlp_ws, "gmlp_bs": gmlp_bs,
            "conv_w": conv_w, "conv_b": conv_b, "conv_norm_w": conv_norm_w, "conv_norm_b": conv_norm_b,
            "w_out": w_out, "ln1_w": ln1_w, "ln1_b": ln1_b, "router_w": router_w, "router_b": router_b,
            "w_gate": w_gate, "w_up": w_up, "w_down": w_down, "ln2_w": ln2_w, "ln2_b": ln2_b}


def reference(x, w_in, b_in, mlstm_norm_w, gmlp_norm_w, gmlp_norm_b, gmlp_ws, gmlp_bs,
              conv_w, conv_b, conv_norm_w, conv_norm_b, w_out, ln1_w, ln1_b, router_w, router_b,
              w_gate, w_up, w_down, ln2_w, ln2_b):
    for l in range(DEPTH):
        y = mixer(x, w_in[l], b_in[l], mlstm_norm_w[l], gmlp_norm_w[l], gmlp_norm_b[l], gmlp_ws[l], gmlp_bs[l],
                  conv_w[l], conv_b[l], conv_norm_w[l], conv_norm_b[l], w_out[l])
        x = layer_norm(DEEPNORM_ALPHA * x + y, ln1_w[l], ln1_b[l])
        y = moe(x, router_w, router_b, w_gate[l], w_up[l], w_down[l])
        x = layer_norm(DEEPNORM_ALPHA * x + y, ln2_w[l], ln2_b[l])
    return x
```

```python
import functools

import jax
import jax.numpy as jnp
from jax import lax
from jax.experimental import pallas as pl
from jax.experimental.pallas import tpu as pltpu

D_MODEL = 1024
DEPTH = 4
ML_WIDTH = 512
ML_HEADS = 4
ML_HEAD_DIM = 128
CHUNK = 128
G_WIDTH = 256
G_HEADS = 4
G_HEAD_DIM = 64
C_WIDTH = 256
CONV_WIDTH = 31
N_EXPERTS = 16
N_GROUPS = 4
EXPERTS_PER_GROUP = 4
D_EXPERT = 512
DEEPNORM_ALPHA = (2.0 * DEPTH) ** 0.25
LN_EPS = 1e-5
K_SCALE = ML_HEAD_DIM ** -0.5

Q_LO, V_LO, O_LO, GU_LO, GV_LO, CA_LO, CB_LO = 0, 512, 1024, 1536, 1792, 2048, 2304
P_ROW = 2560
P_COL = ML_WIDTH + 2 * ML_HEADS

LANES = 128
CONV_HALO = 32
VMEM_LIMIT = 52 * 1024 * 1024

F32 = jnp.float32
BF16 = jnp.bfloat16
NEG_INF = float("-inf")


def _layer_norm(x, w, b, eps):
    mu = jnp.mean(x, axis=-1, keepdims=True)
    xc = x - mu
    var = jnp.mean(xc * xc, axis=-1, keepdims=True)
    return xc * lax.rsqrt(var + eps) * w + b


def _gelu_tanh(x):
    return 0.5 * x * (1.0 + jnp.tanh(0.7978845608028654 * (x + 0.044715 * (x * x * x))))


def _log_sigmoid(x):
    return jnp.minimum(x, 0.0) - jnp.log1p(jnp.exp(-jnp.abs(x)))


def _dot(a, b):
    return jnp.dot(a, b, preferred_element_type=F32)


def _dot_nt(a, b):
    return lax.dot_general(a, b, (((1,), (1,)), ((), ())), preferred_element_type=F32)


def _mixer_kernel(x_ref, w_row_ref, b_row_ref, w_col_ref, b_col_ref, mnw_ref, gnw_ref, gnb_ref,
                  ws_ref, bs_ref, cw_ref, cb_ref, cnw_ref, cnb_ref, w_out_ref, l1w_ref, l1b_ref,
                  o_ref, proj_ref, colp_ref, hcat_ref, cstate_ref, mstate_ref, cbuf_ref, *, tm):
    n_chunks = tm // CHUNK

    @pl.when(pl.program_id(1) == 0)
    def _():
        cstate_ref[...] = jnp.zeros_like(cstate_ref)
        mstate_ref[...] = jnp.zeros_like(mstate_ref)
        cbuf_ref[0:CONV_HALO, :] = jnp.zeros((CONV_HALO, C_WIDTH), F32)

    x = x_ref[...]
    xb = x.astype(BF16)
    proj_ref[...] = _dot(xb, w_row_ref[...]) + b_row_ref[...]
    colp_ref[...] = _dot_nt(w_col_ref[...], xb) + b_col_ref[...]

    gates = colp_ref[ML_WIDTH:P_COL, :]
    lane_in_chunk = lax.broadcasted_iota(jnp.int32, gates.shape, 1) % CHUNK
    logf_all = _log_sigmoid(gates)
    bcum = logf_all
    d = 1
    while d < CHUNK:
        bcum = bcum + jnp.where(lane_in_chunk >= d, pltpu.roll(bcum, d, 1), 0.0)
        d *= 2

    row_i = lax.broadcasted_iota(jnp.int32, (CHUNK, CHUNK), 0)
    col_i = lax.broadcasted_iota(jnp.int32, (CHUNK, CHUNK), 1)
    causal = col_i <= row_i
    diag = col_i == row_i
    ones_col = jnp.where(col_i == 0, 1.0, 0.0).astype(BF16)

    for c in range(n_chunks):
        r0 = c * CHUNK
        rows = slice(r0, r0 + CHUNK)
        logi = gates[0:ML_HEADS, r0:r0 + CHUNK]
        logf = logf_all[ML_HEADS:2 * ML_HEADS, r0:r0 + CHUNK]
        b_row = bcum[ML_HEADS:2 * ML_HEADS, r0:r0 + CHUNK]
        g = jnp.sum(logf, axis=-1, keepdims=True)
        m_prev = mstate_ref[0:ML_HEADS, 0:1]
        a_row = g - b_row + logi
        m_new = jnp.maximum(g + m_prev, jnp.max(a_row, axis=-1, keepdims=True))
        w_row = jnp.exp(a_row - m_new) * K_SCALE
        decay = jnp.exp(g + m_prev - m_new)
        mstate_ref[0:ML_HEADS, :] = jnp.broadcast_to(m_new, (ML_HEADS, LANES))

        for h in range(ML_HEADS):
            hs = slice(h * ML_HEAD_DIM, (h + 1) * ML_HEAD_DIM)
            q = proj_ref[rows, Q_LO + h * ML_HEAD_DIM:Q_LO + (h + 1) * ML_HEAD_DIM].astype(BF16)
            v = proj_ref[rows, V_LO + h * ML_HEAD_DIM:V_LO + (h + 1) * ML_HEAD_DIM].astype(BF16)
            o_gate = proj_ref[rows, O_LO + h * ML_HEAD_DIM:O_LO + (h + 1) * ML_HEAD_DIM]
            kt = colp_ref[hs, r0:r0 + CHUNK]
            v_ext = jnp.concatenate([v, ones_col], axis=1)

            b_r = b_row[h:h + 1, :]
            b_c = jnp.sum(jnp.where(diag, b_r, 0.0), axis=-1, keepdims=True)
            log_d = jnp.where(causal, b_c - b_r + logi[h:h + 1, :], NEG_INF)
            mp = m_prev[h:h + 1, :]
            log_inter = b_c + mp
            m_row = jnp.maximum(log_inter, jnp.max(log_d, axis=-1, keepdims=True))
            s = _dot(q, kt.astype(BF16))
            p = s * (jnp.exp(log_d - m_row) * K_SCALE)
            w_inter = jnp.exp(log_inter - m_row)
            c_ext = cstate_ref[h]
            intra = _dot(p.astype(BF16), v_ext)
            inter = _dot(q, c_ext.astype(BF16))
            tot = intra + w_inter * inter
            den = jnp.maximum(jnp.abs(tot[:, ML_HEAD_DIM:ML_HEAD_DIM + 1]), jnp.exp(-m_row))
            hh = tot[:, 0:ML_HEAD_DIM] / den

            ktw = (kt * w_row[h:h + 1, :]).astype(BF16)
            cstate_ref[h] = decay[h:h + 1, :] * c_ext + _dot(ktw, v_ext)

            mu = jnp.mean(hh, axis=-1, keepdims=True)
            hc = hh - mu
            var = jnp.mean(hc * hc, axis=-1, keepdims=True)
            hn = hc * lax.rsqrt(var + 1e-6) * mnw_ref[:, hs]
            hcat_ref[rows, hs] = (jax.nn.sigmoid(o_gate) * hn).astype(BF16)

        u = _gelu_tanh(proj_ref[rows, GU_LO:GU_LO + G_WIDTH])
        z = _layer_norm(_gelu_tanh(proj_ref[rows, GV_LO:GV_LO + G_WIDTH]), gnw_ref[...], gnb_ref[...], LN_EPS)
        lane_head = lax.broadcasted_iota(jnp.int32, (CHUNK, G_WIDTH), 1) // G_HEAD_DIM
        z_bd = jnp.concatenate(
            [jnp.where(lane_head == h, z, 0.0).astype(BF16) for h in range(G_HEADS)], axis=0)
        zs = _dot(ws_ref[...], z_bd) + bs_ref[...]
        hcat_ref[rows, ML_WIDTH:ML_WIDTH + G_WIDTH] = (u * zs).astype(BF16)

        ca = proj_ref[rows, CA_LO:CA_LO + C_WIDTH]
        cb = proj_ref[rows, CB_LO:CB_LO + C_WIDTH]
        cbuf_ref[CONV_HALO + r0:CONV_HALO + r0 + CHUNK, :] = ca * jax.nn.sigmoid(cb)

    first_tap = CONV_HALO - (CONV_WIDTH - 1)
    for c in range(n_chunks):
        r0 = c * CHUNK
        acc = jnp.zeros((CHUNK, C_WIDTH), F32) + cb_ref[...]
        for k in range(CONV_WIDTH):
            acc = acc + cw_ref[k:k + 1, :] * cbuf_ref[r0 + first_tap + k:r0 + first_tap + k + CHUNK, :]
        cn = _layer_norm(acc, cnw_ref[...], cnb_ref[...], LN_EPS)
        hcat_ref[r0:r0 + CHUNK, ML_WIDTH + G_WIDTH:D_MODEL] = (cn * jax.nn.sigmoid(cn)).astype(BF16)
    cbuf_ref[0:CONV_HALO, :] = cbuf_ref[tm:tm + CONV_HALO, :]

    y = _dot(hcat_ref[...], w_out_ref[...])
    o_ref[...] = _layer_norm(DEEPNORM_ALPHA * x + y, l1w_ref[...], l1b_ref[...], LN_EPS)


def _full(shape):
    nd = len(shape)
    return pl.BlockSpec(shape, lambda b, j, _nd=nd: (0,) * _nd)


def _mixer_layer(x, w_row, b_row, w_col, b_col, mnw, gnw, gnb, ws_cat, bs_full, cw, cb, cnw, cnb,
                 w_out, l1w, l1b, *, tm):
    batch, seq, _ = x.shape
    kernel = functools.partial(_mixer_kernel, tm=tm)
    weights = (w_row, b_row, w_col, b_col, mnw, gnw, gnb, ws_cat, bs_full, cw, cb, cnw, cnb, w_out, l1w, l1b)
    return pl.pallas_call(
        kernel,
        out_shape=jax.ShapeDtypeStruct(x.shape, F32),
        grid=(batch, seq // tm),
        in_specs=[pl.BlockSpec((None, tm, D_MODEL), lambda b, j: (b, j, 0))] + [_full(w.shape) for w in weights],
        out_specs=pl.BlockSpec((None, tm, D_MODEL), lambda b, j: (b, j, 0)),
        scratch_shapes=[
            pltpu.VMEM((tm, P_ROW), F32),
            pltpu.VMEM((P_COL, tm), F32),
            pltpu.VMEM((tm, D_MODEL), BF16),
            pltpu.VMEM((ML_HEADS, ML_HEAD_DIM, 2 * ML_HEAD_DIM), F32),
            pltpu.VMEM((8, LANES), F32),
            pltpu.VMEM((tm + CONV_HALO, C_WIDTH), F32),
        ],
        compiler_params=pltpu.CompilerParams(
            dimension_semantics=("arbitrary", "arbitrary"), vmem_limit_bytes=VMEM_LIMIT),
        name="mixer",
    )(x, *weights)


def _route(logits, router_b):
    lane = lax.broadcasted_iota(jnp.int32, logits.shape, 1)
    lane_f = lane.astype(F32)
    valid = lane < N_EXPERTS
    s = jax.nn.sigmoid(logits)
    sel = jnp.where(valid, s + router_b, NEG_INF)

    def top1(v):
        m = jnp.max(v, axis=-1, keepdims=True)
        idx = jnp.min(jnp.where(v == m, lane_f, float(LANES)), axis=-1, keepdims=True)
        return m, idx

    def top2(v):
        m1, i1 = top1(v)
        m2, i2 = top1(jnp.where(lane_f == i1, NEG_INF, v))
        return m1, i1, m2, i2

    group = lane // EXPERTS_PER_GROUP
    best = None
    for g in range(N_GROUPS):
        m1, _, m2, _ = top2(jnp.where(group == g, sel, NEG_INF))
        score = m1 + m2
        if best is None:
            best, gidx = score, jnp.zeros_like(score, dtype=jnp.int32)
        else:
            better = score > best
            gidx = jnp.where(better, g, gidx)
            best = jnp.where(better, score, best)
    _, e1, _, e2 = top2(jnp.where(group == gidx, sel, NEG_INF))
    g1 = jnp.sum(jnp.where(lane_f == e1, s, 0.0), axis=-1, keepdims=True)
    g2 = jnp.sum(jnp.where(lane_f == e2, s, 0.0), axis=-1, keepdims=True)
    tot = g1 + g2
    return jnp.where(lane_f == e1, g1 / tot, 0.0) + jnp.where(lane_f == e2, g2 / tot, 0.0)


def _moe_dense_kernel(x_ref, rw_ref, rb_ref, wg_ref, wu_ref, wd_ref, l2w_ref, l2b_ref, o_ref,
                      xb_ref, comb_ref, acc_ref):
    e = pl.program_id(1)

    @pl.when(e == 0)
    def _():
        x = x_ref[...]
        xb_ref[...] = x.astype(BF16)
        logits = jnp.dot(x, rw_ref[...], preferred_element_type=F32, precision=lax.Precision.HIGHEST)
        comb_ref[...] = _route(logits, rb_ref[...])
        acc_ref[...] = jnp.zeros_like(acc_ref)

    xb = xb_ref[...]
    gate = _dot(xb, wg_ref[...])
    up = _dot(xb, wu_ref[...])
    hid = (gate * jax.nn.sigmoid(gate) * up).astype(BF16)
    lane = lax.broadcasted_iota(jnp.int32, comb_ref.shape, 1)
    w_e = jnp.sum(jnp.where(lane == e, comb_ref[...], 0.0), axis=-1, keepdims=True)
    acc_ref[...] += w_e * _dot(hid, wd_ref[...])

    @pl.when(e == N_EXPERTS - 1)
    def _():
        o_ref[...] = _layer_norm(DEEPNORM_ALPHA * x_ref[...] + acc_ref[...], l2w_ref[...], l2b_ref[...], LN_EPS)


def _moe_layer(xt, rw_pad, rb_pad, wg, wu, wd, l2w, l2b, *, tm):
    n_tok = xt.shape[0]
    return pl.pallas_call(
        _moe_dense_kernel,
        out_shape=jax.ShapeDtypeStruct(xt.shape, F32),
        grid=(n_tok // tm, N_EXPERTS),
        in_specs=[
            pl.BlockSpec((tm, D_MODEL), lambda i, e: (i, 0)),
            pl.BlockSpec((D_MODEL, LANES), lambda i, e: (0, 0)),
            pl.BlockSpec((1, LANES), lambda i, e: (0, 0)),
            pl.BlockSpec((None, D_MODEL, D_EXPERT), lambda i, e: (e, 0, 0)),
            pl.BlockSpec((None, D_MODEL, D_EXPERT), lambda i, e: (e, 0, 0)),
            pl.BlockSpec((None, D_EXPERT, D_MODEL), lambda i, e: (e, 0, 0)),
            pl.BlockSpec((1, D_MODEL), lambda i, e: (0, 0)),
            pl.BlockSpec((1, D_MODEL), lambda i, e: (0, 0)),
        ],
        out_specs=pl.BlockSpec((tm, D_MODEL), lambda i, e: (i, 0)),
        scratch_shapes=[
            pltpu.VMEM((tm, D_MODEL), BF16),
            pltpu.VMEM((tm, LANES), F32),
            pltpu.VMEM((tm, D_MODEL), F32),
        ],
        compiler_params=pltpu.CompilerParams(
            dimension_semantics=("arbitrary", "arbitrary"), vmem_limit_bytes=VMEM_LIMIT),
        name="moe_dense",
    )(xt, rw_pad, rb_pad, wg, wu, wd, l2w, l2b)


def kernel(x, w_in, b_in, mlstm_norm_w, gmlp_norm_w, gmlp_norm_b, gmlp_ws, gmlp_bs, conv_w, conv_b,
           conv_norm_w, conv_norm_b, w_out, ln1_w, ln1_b, router_w, router_b, w_gate, w_up, w_down,
           ln2_w, ln2_b):
    batch, seq, d = x.shape
    mixer_tm = 256
    moe_tm = 1024

    q_lo, k_lo, v_lo, o_lo = 0, ML_WIDTH, 2 * ML_WIDTH, 3 * ML_WIDTH
    gate_lo = 4 * ML_WIDTH
    rest_lo = gate_lo + 2 * ML_HEADS

    def row_part(t):
        return jnp.concatenate([t[..., q_lo:k_lo], t[..., v_lo:gate_lo], t[..., rest_lo:]], axis=-1)

    def col_part(t):
        return jnp.concatenate([t[..., k_lo:v_lo], t[..., gate_lo:rest_lo]], axis=-1)

    tril = jnp.tril(jnp.ones((CHUNK, CHUNK), gmlp_ws.dtype))
    rw_pad = jnp.pad(router_w.astype(F32), ((0, 0), (0, LANES - N_EXPERTS)))
    rb_pad = jnp.pad(router_b.astype(F32), (0, LANES - N_EXPERTS)).reshape(1, LANES)

    for l in range(DEPTH):
        w_row = row_part(w_in[l]).astype(BF16)
        b_row = row_part(b_in[l]).reshape(1, P_ROW)
        w_col = col_part(w_in[l]).T.astype(BF16)
        b_col = col_part(b_in[l]).reshape(P_COL, 1)
        ws_cat = jnp.transpose(gmlp_ws[l] * tril, (1, 0, 2)).reshape(CHUNK, G_HEADS * CHUNK).astype(BF16)
        bs_full = jnp.repeat(gmlp_bs[l].T, G_HEAD_DIM, axis=1)
        cw = jnp.pad(conv_w[l], ((0, 1), (0, 0)))
        x = _mixer_layer(
            x, w_row, b_row, w_col, b_col, mlstm_norm_w[l].reshape(1, -1), gmlp_norm_w[l].reshape(1, -1),
            gmlp_norm_b[l].reshape(1, -1), ws_cat, bs_full, cw, conv_b[l].reshape(1, -1),
            conv_norm_w[l].reshape(1, -1), conv_norm_b[l].reshape(1, -1), w_out[l].astype(BF16),
            ln1_w[l].reshape(1, -1), ln1_b[l].reshape(1, -1), tm=mixer_tm)
        xt = _moe_layer(
            x.reshape(batch * seq, d), rw_pad, rb_pad, w_gate[l].astype(BF16), w_up[l].astype(BF16),
            w_down[l].astype(BF16), ln2_w[l].reshape(1, -1), ln2_b[l].reshape(1, -1), tm=moe_tm)
        x = xt.reshape(batch, seq, d)
    return x
```

```python
import functools

import jax
import jax.numpy as jnp
from jax import lax
from jax.experimental import pallas as pl
from jax.experimental.pallas import tpu as pltpu

D_MODEL = 1024
DEPTH = 4
ML_WIDTH = 512
ML_HEADS = 4
ML_HEAD_DIM = 128
CHUNK = 128
G_WIDTH = 256
G_HEADS = 4
G_HEAD_DIM = 64
C_WIDTH = 256
CONV_WIDTH = 31
N_EXPERTS = 16
N_GROUPS = 4
EXPERTS_PER_GROUP = 4
D_EXPERT = 512
DEEPNORM_ALPHA = (2.0 * DEPTH) ** 0.25
LN_EPS = 1e-5
K_SCALE = ML_HEAD_DIM ** -0.5

Q_LO, V_LO, O_LO, GU_LO, GV_LO, CA_LO, CB_LO = 0, 512, 1024, 1536, 1792, 2048, 2304
P_ROW = 2560
P_COL = ML_WIDTH + 2 * ML_HEADS

LANES = 128
SUBLANES = 8
MOE_TM = 256
ROW_TILE = 512
PAYLOAD_W = D_MODEL + LANES
COMPACT_ROWS = 2 * MOE_TM + LANES
CONV_HALO = 32
VMEM_LIMIT = 52 * 1024 * 1024

F32 = jnp.float32
BF16 = jnp.bfloat16
NEG_INF = float("-inf")


def _layer_norm(x, w, b, eps):
    mu = jnp.mean(x, axis=-1, keepdims=True)
    xc = x - mu
    var = jnp.mean(xc * xc, axis=-1, keepdims=True)
    return xc * lax.rsqrt(var + eps) * w + b


def _gelu_tanh(x):
    return 0.5 * x * (1.0 + jnp.tanh(0.7978845608028654 * (x + 0.044715 * (x * x * x))))


def _log_sigmoid(x):
    return jnp.minimum(x, 0.0) - jnp.log1p(jnp.exp(-jnp.abs(x)))


def _dot(a, b):
    return jnp.dot(a, b, preferred_element_type=F32)


def _dot_nt(a, b):
    return lax.dot_general(a, b, (((1,), (1,)), ((), ())), preferred_element_type=F32)


def _mixer_kernel(x_ref, w_row_ref, b_row_ref, w_col_ref, b_col_ref, mnw_ref, gnw_ref, gnb_ref,
                  ws_ref, bs_ref, cw_ref, cb_ref, cnw_ref, cnb_ref, w_out_ref, l1w_ref, l1b_ref,
                  o_ref, proj_ref, colp_ref, hcat_ref, cstate_ref, mstate_ref, cbuf_ref, *, tm):
    n_chunks = tm // CHUNK

    @pl.when(pl.program_id(1) == 0)
    def _():
        cstate_ref[...] = jnp.zeros_like(cstate_ref)
        mstate_ref[...] = jnp.zeros_like(mstate_ref)
        cbuf_ref[0:CONV_HALO, :] = jnp.zeros((CONV_HALO, C_WIDTH), F32)

    x = x_ref[...]
    xb = x.astype(BF16)
    proj_ref[...] = _dot(xb, w_row_ref[...]) + b_row_ref[...]
    colp_ref[...] = _dot_nt(w_col_ref[...], xb) + b_col_ref[...]

    gates = colp_ref[ML_WIDTH:P_COL, :]
    lane_in_chunk = lax.broadcasted_iota(jnp.int32, gates.shape, 1) % CHUNK
    logf_all = _log_sigmoid(gates)
    bcum = logf_all
    d = 1
    while d < CHUNK:
        bcum = bcum + jnp.where(lane_in_chunk >= d, pltpu.roll(bcum, d, 1), 0.0)
        d *= 2

    row_i = lax.broadcasted_iota(jnp.int32, (CHUNK, CHUNK), 0)
    col_i = lax.broadcasted_iota(jnp.int32, (CHUNK, CHUNK), 1)
    causal = col_i <= row_i
    diag = col_i == row_i
    ones_col = jnp.where(col_i == 0, 1.0, 0.0).astype(BF16)

    for c in range(n_chunks):
        r0 = c * CHUNK
        rows = slice(r0, r0 + CHUNK)
        logi = gates[0:ML_HEADS, r0:r0 + CHUNK]
        logf = logf_all[ML_HEADS:2 * ML_HEADS, r0:r0 + CHUNK]
        b_row = bcum[ML_HEADS:2 * ML_HEADS, r0:r0 + CHUNK]
        g = jnp.sum(logf, axis=-1, keepdims=True)
        m_prev = mstate_ref[0:ML_HEADS, 0:1]
        a_row = g - b_row + logi
        m_new = jnp.maximum(g + m_prev, jnp.max(a_row, axis=-1, keepdims=True))
        w_row = jnp.exp(a_row - m_new) * K_SCALE
        decay = jnp.exp(g + m_prev - m_new)
        mstate_ref[0:ML_HEADS, :] = jnp.broadcast_to(m_new, (ML_HEADS, LANES))

        for h in range(ML_HEADS):
            hs = slice(h * ML_HEAD_DIM, (h + 1) * ML_HEAD_DIM)
            q = proj_ref[rows, Q_LO + h * ML_HEAD_DIM:Q_LO + (h + 1) * ML_HEAD_DIM].astype(BF16)
            v = proj_ref[rows, V_LO + h * ML_HEAD_DIM:V_LO + (h + 1) * ML_HEAD_DIM].astype(BF16)
            o_gate = proj_ref[rows, O_LO + h * ML_HEAD_DIM:O_LO + (h + 1) * ML_HEAD_DIM]
            kt = colp_ref[hs, r0:r0 + CHUNK]
            v_ext = jnp.concatenate([v, ones_col], axis=1)

            b_r = b_row[h:h + 1, :]
            b_c = jnp.sum(jnp.where(diag, b_r, 0.0), axis=-1, keepdims=True)
            log_d = jnp.where(causal, b_c - b_r + logi[h:h + 1, :], NEG_INF)
            mp = m_prev[h:h + 1, :]
            log_inter = b_c + mp
            m_row = jnp.maximum(log_inter, jnp.max(log_d, axis=-1, keepdims=True))
            s = _dot(q, kt.astype(BF16))
            p = s * (jnp.exp(log_d - m_row) * K_SCALE)
            w_inter = jnp.exp(log_inter - m_row)
            c_ext = cstate_ref[h]
            intra = _dot(p.astype(BF16), v_ext)
            inter = _dot(q, c_ext.astype(BF16))
            tot = intra + w_inter * inter
            den = jnp.maximum(jnp.abs(tot[:, ML_HEAD_DIM:ML_HEAD_DIM + 1]), jnp.exp(-m_row))
            hh = tot[:, 0:ML_HEAD_DIM] / den

            ktw = (kt * w_row[h:h + 1, :]).astype(BF16)
            cstate_ref[h] = decay[h:h + 1, :] * c_ext + _dot(ktw, v_ext)

            mu = jnp.mean(hh, axis=-1, keepdims=True)
            hc = hh - mu
            var = jnp.mean(hc * hc, axis=-1, keepdims=True)
            hn = hc * lax.rsqrt(var + 1e-6) * mnw_ref[:, hs]
            hcat_ref[rows, hs] = (jax.nn.sigmoid(o_gate) * hn).astype(BF16)

        u = _gelu_tanh(proj_ref[rows, GU_LO:GU_LO + G_WIDTH])
        z = _layer_norm(_gelu_tanh(proj_ref[rows, GV_LO:GV_LO + G_WIDTH]), gnw_ref[...], gnb_ref[...], LN_EPS)
        lane_head = lax.broadcasted_iota(jnp.int32, (CHUNK, G_WIDTH), 1) // G_HEAD_DIM
        z_bd = jnp.concatenate(
            [jnp.where(lane_head == h, z, 0.0).astype(BF16) for h in range(G_HEADS)], axis=0)
        zs = _dot(ws_ref[...], z_bd) + bs_ref[...]
        hcat_ref[rows, ML_WIDTH:ML_WIDTH + G_WIDTH] = (u * zs).astype(BF16)

        ca = proj_ref[rows, CA_LO:CA_LO + C_WIDTH]
        cb = proj_ref[rows, CB_LO:CB_LO + C_WIDTH]
        cbuf_ref[CONV_HALO + r0:CONV_HALO + r0 + CHUNK, :] = ca * jax.nn.sigmoid(cb)

    first_tap = CONV_HALO - (CONV_WIDTH - 1)
    for c in range(n_chunks):
        r0 = c * CHUNK
        acc = jnp.zeros((CHUNK, C_WIDTH), F32) + cb_ref[...]
        for k in range(CONV_WIDTH):
            acc = acc + cw_ref[k:k + 1, :] * cbuf_ref[r0 + first_tap + k:r0 + first_tap + k + CHUNK, :]
        cn = _layer_norm(acc, cnw_ref[...], cnb_ref[...], LN_EPS)
        hcat_ref[r0:r0 + CHUNK, ML_WIDTH + G_WIDTH:D_MODEL] = (cn * jax.nn.sigmoid(cn)).astype(BF16)
    cbuf_ref[0:CONV_HALO, :] = cbuf_ref[tm:tm + CONV_HALO, :]

    y = _dot(hcat_ref[...], w_out_ref[...])
    o_ref[...] = _layer_norm(DEEPNORM_ALPHA * x + y, l1w_ref[...], l1b_ref[...], LN_EPS)


def _full(shape):
    nd = len(shape)
    return pl.BlockSpec(shape, lambda b, j, _nd=nd: (0,) * _nd)


def _mixer_layer(x, w_row, b_row, w_col, b_col, mnw, gnw, gnb, ws_cat, bs_full, cw, cb, cnw, cnb,
                 w_out, l1w, l1b, *, tm):
    batch, seq, _ = x.shape
    kernel = functools.partial(_mixer_kernel, tm=tm)
    weights = (w_row, b_row, w_col, b_col, mnw, gnw, gnb, ws_cat, bs_full, cw, cb, cnw, cnb, w_out, l1w, l1b)
    return pl.pallas_call(
        kernel,
        out_shape=jax.ShapeDtypeStruct(x.shape, F32),
        grid=(batch, seq // tm),
        in_specs=[pl.BlockSpec((None, tm, D_MODEL), lambda b, j: (b, j, 0))] + [_full(w.shape) for w in weights],
        out_specs=pl.BlockSpec((None, tm, D_MODEL), lambda b, j: (b, j, 0)),
        scratch_shapes=[
            pltpu.VMEM((tm, P_ROW), F32),
            pltpu.VMEM((P_COL, tm), F32),
            pltpu.VMEM((tm, D_MODEL), BF16),
            pltpu.VMEM((ML_HEADS, ML_HEAD_DIM, 2 * ML_HEAD_DIM), F32),
            pltpu.VMEM((8, LANES), F32),
            pltpu.VMEM((tm + CONV_HALO, C_WIDTH), F32),
        ],
        compiler_params=pltpu.CompilerParams(
            dimension_semantics=("arbitrary", "arbitrary"), vmem_limit_bytes=VMEM_LIMIT),
        name="mixer",
    )(x, *weights)


def _route(logits, router_b):
    lane = lax.broadcasted_iota(jnp.int32, logits.shape, 1)
    lane_f = lane.astype(F32)
    valid = lane < N_EXPERTS
    s = jax.nn.sigmoid(logits)
    sel = jnp.where(valid, s + router_b, NEG_INF)

    def top1(v):
        m = jnp.max(v, axis=-1, keepdims=True)
        idx = jnp.min(jnp.where(v == m, lane_f, float(LANES)), axis=-1, keepdims=True)
        return m, idx

    def top2(v):
        m1, i1 = top1(v)
        m2, i2 = top1(jnp.where(lane_f == i1, NEG_INF, v))
        return m1, i1, m2, i2

    group = lane // EXPERTS_PER_GROUP
    best = None
    for g in range(N_GROUPS):
        m1, _, m2, _ = top2(jnp.where(group == g, sel, NEG_INF))
        score = m1 + m2
        if best is None:
            best, gidx = score, jnp.zeros_like(score, dtype=jnp.int32)
        else:
            better = score > best
            gidx = jnp.where(better, g, gidx)
            best = jnp.where(better, score, best)
    _, e1, _, e2 = top2(jnp.where(group == gidx, sel, NEG_INF))
    g1 = jnp.sum(jnp.where(lane_f == e1, s, 0.0), axis=-1, keepdims=True)
    g2 = jnp.sum(jnp.where(lane_f == e2, s, 0.0), axis=-1, keepdims=True)
    tot = g1 + g2
    return e1, e2, g1 / tot, g2 / tot


def _route_kernel(x_ref, rw_ref, rb_ref, meta_ref, metat_ref, tile_n_ref, tile_cnt_ref, cnt_ref):
    @pl.when(pl.program_id(0) == 0)
    def _():
        cnt_ref[...] = jnp.zeros_like(cnt_ref)

    x = x_ref[...]
    tm = x.shape[0]
    logits = jnp.dot(x, rw_ref[...], preferred_element_type=F32, precision=lax.Precision.HIGHEST)
    e1, e2, g1, g2 = _route(logits, rb_ref[...])

    lane = lax.broadcasted_iota(jnp.int32, (tm, LANES), 1)
    lane_f = lane.astype(F32)
    sel1 = lane_f == e1
    sel2 = lane_f == e2
    assign = jnp.where(sel1, 1.0, jnp.where(sel2, 1.0, 0.0))
    r_i = lax.broadcasted_iota(jnp.int32, (tm, tm), 0)
    c_i = lax.broadcasted_iota(jnp.int32, (tm, tm), 1)
    earlier = jnp.where(c_i < r_i, 1.0, 0.0).astype(BF16)
    rank = _dot(earlier, assign.astype(BF16))
    n = jnp.sum(assign, axis=0, keepdims=True)
    n_up = jnp.floor((n + (SUBLANES - 1.0)) * (1.0 / SUBLANES)) * SUBLANES
    e_r = lax.broadcasted_iota(jnp.int32, (LANES, LANES), 0)
    e_c = lax.broadcasted_iota(jnp.int32, (LANES, LANES), 1)
    before = jnp.where(e_r < e_c, 1.0, 0.0).astype(BF16)
    off = _dot(jnp.broadcast_to(n_up, (SUBLANES, LANES)).astype(BF16), before)[0:1, :]
    pos = off + rank
    cpos1 = jnp.sum(jnp.where(sel1, pos, 0.0), axis=-1, keepdims=True)
    cpos2 = jnp.sum(jnp.where(sel2, pos, 0.0), axis=-1, keepdims=True)
    meta_ref[...] = jnp.where(lane == 0, cpos1, jnp.where(lane == 1, cpos2,
                              jnp.where(lane == 2, g1, jnp.where(lane == 3, g2, 0.0))))
    eye = r_i == c_i
    as_row = lambda v: jnp.sum(jnp.where(eye, v, 0.0), axis=0, keepdims=True)
    metat_ref[...] = jnp.concatenate(
        [as_row(cpos1), as_row(cpos2), as_row(g1), as_row(g2), jnp.zeros((SUBLANES - 4, tm), F32)], axis=0)

    cnt = cnt_ref[...]
    tile_n_ref[...] = jnp.broadcast_to(n_up, (SUBLANES, LANES)).astype(jnp.int32)
    tile_cnt_ref[...] = jnp.broadcast_to(cnt, (SUBLANES, LANES)).astype(jnp.int32)
    cnt_ref[...] = cnt + n_up


def _route_layer(xt, rw_pad, rb_pad):
    n_tok = xt.shape[0]
    n_tiles = n_tok // MOE_TM
    return pl.pallas_call(
        _route_kernel,
        out_shape=(
            jax.ShapeDtypeStruct((n_tok, LANES), F32),
            jax.ShapeDtypeStruct((SUBLANES, n_tok), F32),
            jax.ShapeDtypeStruct((n_tiles, SUBLANES, LANES), jnp.int32),
            jax.ShapeDtypeStruct((n_tiles, SUBLANES, LANES), jnp.int32),
        ),
        grid=(n_tiles,),
        in_specs=[
            pl.BlockSpec((MOE_TM, D_MODEL), lambda i: (i, 0)),
            pl.BlockSpec((D_MODEL, LANES), lambda i: (0, 0)),
            pl.BlockSpec((1, LANES), lambda i: (0, 0)),
        ],
        out_specs=(
            pl.BlockSpec((MOE_TM, LANES), lambda i: (i, 0)),
            pl.BlockSpec((SUBLANES, MOE_TM), lambda i: (0, i)),
            pl.BlockSpec((None, SUBLANES, LANES), lambda i: (i, 0, 0)),
            pl.BlockSpec((None, SUBLANES, LANES), lambda i: (i, 0, 0)),
        ),
        scratch_shapes=[pltpu.VMEM((1, LANES), F32)],
        compiler_params=pltpu.CompilerParams(dimension_semantics=("arbitrary",), vmem_limit_bytes=VMEM_LIMIT),
        name="route",
    )(xt, rw_pad, rb_pad)


def _for_each_run(n_tab, cnt_tab, tile, cape, make_copy, act):
    def body(e, off):
        n = n_tab[tile * N_EXPERTS + e]
        base = e * cape + cnt_tab[tile * N_EXPERTS + e]
        piece = MOE_TM
        while piece >= SUBLANES:
            start = jnp.bitwise_and(n, -2 * piece)

            @pl.when(jnp.bitwise_and(n, piece) != 0)
            def _(piece=piece, start=start):
                act(make_copy(pl.multiple_of(off + start, SUBLANES), pl.multiple_of(base + start, SUBLANES), piece))
            piece //= 2
        return off + n

    lax.fori_loop(0, N_EXPERTS, body, jnp.int32(0))


def _dispatch_kernel(n_tab, cnt_tab, x_ref, metat_ref, xs_hbm, comp_ref, sem, *, cape):
    i = pl.program_id(0)
    last = pl.num_programs(0) - 1
    slot = lax.rem(i, 2)

    def copies(tile, slot_, act):
        def make_copy(src_row, dst_row, rows):
            return pltpu.make_async_copy(comp_ref.at[slot_, pl.ds(src_row, rows)],
                                         xs_hbm.at[pl.ds(dst_row, rows)], sem.at[slot_])
        _for_each_run(n_tab, cnt_tab, tile, cape, make_copy, act)

    @pl.when(i >= 2)
    def _():
        copies(i - 2, slot, lambda cp: cp.wait())

    mt = metat_ref[...]
    row_f = lax.broadcasted_iota(jnp.int32, (COMPACT_ROWS, MOE_TM), 0).astype(F32)
    hit1 = row_f == mt[0:1, :]
    hit2 = row_f == mt[1:2, :]
    onehot = jnp.where(hit1, 1.0, jnp.where(hit2, 1.0, 0.0)).astype(BF16)
    comp_ref[slot, :, 0:D_MODEL] = _dot(onehot, x_ref[...].astype(BF16))
    gate = jnp.sum(jnp.where(hit1, mt[2:3, :], jnp.where(hit2, mt[3:4, :], 0.0)), axis=-1, keepdims=True)
    comp_ref[slot, :, D_MODEL:PAYLOAD_W] = jnp.broadcast_to(gate, (COMPACT_ROWS, LANES))
    copies(i, slot, lambda cp: cp.start())

    @pl.when(i == last)
    def _():
        @pl.when(i >= 1)
        def _():
            copies(i - 1, 1 - slot, lambda cp: cp.wait())
        copies(i, slot, lambda cp: cp.wait())


def _dispatch_layer(n_tab, cnt_tab, xt, metat, cape):
    n_tok = xt.shape[0]
    return pl.pallas_call(
        functools.partial(_dispatch_kernel, cape=cape),
        out_shape=jax.ShapeDtypeStruct((N_EXPERTS * cape, PAYLOAD_W), F32),
        grid_spec=pltpu.PrefetchScalarGridSpec(
            num_scalar_prefetch=2,
            grid=(n_tok // MOE_TM,),
            in_specs=[
                pl.BlockSpec((MOE_TM, D_MODEL), lambda i, n, c: (i, 0)),
                pl.BlockSpec((SUBLANES, MOE_TM), lambda i, n, c: (0, i)),
            ],
            out_specs=pl.BlockSpec(memory_space=pl.ANY),
            scratch_shapes=[pltpu.VMEM((2, COMPACT_ROWS, PAYLOAD_W), F32), pltpu.SemaphoreType.DMA((2,))],
        ),
        compiler_params=pltpu.CompilerParams(dimension_semantics=("arbitrary",), vmem_limit_bytes=VMEM_LIMIT),
        name="dispatch",
    )(n_tab, cnt_tab, xt, metat)


def _expert_kernel(tile_blk, tile_e, tile_valid, xs_ref, wg_ref, wu_ref, wd_ref, ys_ref, wgb_ref, wub_ref, wdb_ref):
    i = pl.program_id(0)
    valid = tile_valid[i]
    new_expert = jnp.logical_or(i == 0, tile_e[i] != tile_e[jnp.maximum(i - 1, 0)])

    @pl.when(jnp.logical_and(valid > 0, new_expert))
    def _():
        wgb_ref[...] = wg_ref[...].astype(BF16)
        wub_ref[...] = wu_ref[...].astype(BF16)
        wdb_ref[...] = wd_ref[...].astype(BF16)

    @pl.when(valid > 0)
    def _():
        keep = lax.broadcasted_iota(jnp.int32, (ROW_TILE, 1), 0) < valid
        xb = jnp.where(keep, xs_ref[:, 0:D_MODEL], 0.0).astype(BF16)
        gate = jnp.where(keep, xs_ref[:, D_MODEL:D_MODEL + 1], 0.0)
        g = _dot(xb, wgb_ref[...])
        u = _dot(xb, wub_ref[...])
        hid = (g * jax.nn.sigmoid(g) * u).astype(BF16)
        ys_ref[...] = gate * _dot(hid, wdb_ref[...])


def _expert_layer(tile_blk, tile_e, tile_valid, xs, wg, wu, wd):
    n_steps = tile_blk.shape[0]
    return pl.pallas_call(
        _expert_kernel,
        out_shape=jax.ShapeDtypeStruct((xs.shape[0], D_MODEL), F32),
        grid_spec=pltpu.PrefetchScalarGridSpec(
            num_scalar_prefetch=3,
            grid=(n_steps,),
            in_specs=[
                pl.BlockSpec((ROW_TILE, PAYLOAD_W), lambda i, blk, e, v: (blk[i], 0)),
                pl.BlockSpec((None, D_MODEL, D_EXPERT), lambda i, blk, e, v: (e[i], 0, 0)),
                pl.BlockSpec((None, D_MODEL, D_EXPERT), lambda i, blk, e, v: (e[i], 0, 0)),
                pl.BlockSpec((None, D_EXPERT, D_MODEL), lambda i, blk, e, v: (e[i], 0, 0)),
            ],
            out_specs=pl.BlockSpec((ROW_TILE, D_MODEL), lambda i, blk, e, v: (blk[i], 0)),
            scratch_shapes=[
                pltpu.VMEM((D_MODEL, D_EXPERT), BF16),
                pltpu.VMEM((D_MODEL, D_EXPERT), BF16),
                pltpu.VMEM((D_EXPERT, D_MODEL), BF16),
            ],
        ),
        compiler_params=pltpu.CompilerParams(dimension_semantics=("arbitrary",), vmem_limit_bytes=VMEM_LIMIT),
        name="experts",
    )(tile_blk, tile_e, tile_valid, xs, wg, wu, wd)


def _combine_kernel(n_tab, cnt_tab, x_ref, meta_ref, ys_hbm, l2w_ref, l2b_ref, o_ref, yc_ref, sem, *, cape):
    i = pl.program_id(0)
    n_steps = pl.num_programs(0)
    slot = lax.rem(i, 2)

    def copies(tile, slot_, act):
        def make_copy(buf_row, ys_row, rows):
            return pltpu.make_async_copy(ys_hbm.at[pl.ds(ys_row, rows)],
                                         yc_ref.at[slot_, pl.ds(buf_row, rows)], sem.at[slot_])
        _for_each_run(n_tab, cnt_tab, tile, cape, make_copy, act)

    @pl.when(i == 0)
    def _():
        yc_ref[...] = jnp.zeros_like(yc_ref)
        copies(0, 0, lambda cp: cp.start())

    @pl.when(i + 1 < n_steps)
    def _():
        copies(i + 1, 1 - slot, lambda cp: cp.start())

    copies(i, slot, lambda cp: cp.wait())

    meta = meta_ref[...]
    col_f = lax.broadcasted_iota(jnp.int32, (MOE_TM, COMPACT_ROWS), 1).astype(F32)
    onehot = jnp.where(col_f == meta[:, 0:1], 1.0, jnp.where(col_f == meta[:, 1:2], 1.0, 0.0)).astype(BF16)
    y = _dot(onehot, yc_ref[slot].astype(BF16))
    o_ref[...] = _layer_norm(DEEPNORM_ALPHA * x_ref[...] + y, l2w_ref[...], l2b_ref[...], LN_EPS)


def _combine_layer(n_tab, cnt_tab, xt, meta, ys, l2w, l2b, cape):
    n_tok = xt.shape[0]
    return pl.pallas_call(
        functools.partial(_combine_kernel, cape=cape),
        out_shape=jax.ShapeDtypeStruct(xt.shape, F32),
        grid_spec=pltpu.PrefetchScalarGridSpec(
            num_scalar_prefetch=2,
            grid=(n_tok // MOE_TM,),
            in_specs=[
                pl.BlockSpec((MOE_TM, D_MODEL), lambda i, n, c: (i, 0)),
                pl.BlockSpec((MOE_TM, LANES), lambda i, n, c: (i, 0)),
                pl.BlockSpec(memory_space=pl.ANY),
                pl.BlockSpec((1, D_MODEL), lambda i, n, c: (0, 0)),
                pl.BlockSpec((1, D_MODEL), lambda i, n, c: (0, 0)),
            ],
            out_specs=pl.BlockSpec((MOE_TM, D_MODEL), lambda i, n, c: (i, 0)),
            scratch_shapes=[pltpu.VMEM((2, COMPACT_ROWS, D_MODEL), F32), pltpu.SemaphoreType.DMA((2,))],
        ),
        compiler_params=pltpu.CompilerParams(dimension_semantics=("arbitrary",), vmem_limit_bytes=VMEM_LIMIT),
        name="combine",
    )(n_tab, cnt_tab, xt, meta, ys, l2w, l2b)


def _expert_tiles(counts, cape, n_steps):
    tiles_e = (counts + (ROW_TILE - 1)) // ROW_TILE
    cum = jnp.cumsum(tiles_e)
    total = cum[-1]
    step = jnp.arange(n_steps, dtype=jnp.int32)
    active = jnp.minimum(step, total - 1)
    e_of = jnp.searchsorted(cum, active, side="right").astype(jnp.int32)
    local = active - (cum - tiles_e)[e_of]
    blk = e_of * (cape // ROW_TILE) + local
    valid = jnp.where(step < total, jnp.clip(counts[e_of] - local * ROW_TILE, 0, ROW_TILE), 0)
    return blk.astype(jnp.int32), e_of, valid.astype(jnp.int32)


def _moe_layer(xt, rw_pad, rb_pad, wg, wu, wd, l2w, l2b):
    n_tok = xt.shape[0]
    n_tiles = n_tok // MOE_TM
    pad_rows = (SUBLANES - 1) * n_tiles
    cape = -(-(n_tok + pad_rows) // ROW_TILE) * ROW_TILE
    n_steps = (2 * n_tok + N_EXPERTS * pad_rows) // ROW_TILE + N_EXPERTS
    meta, metat, tile_n, tile_cnt = _route_layer(xt, rw_pad, rb_pad)
    n_tab = tile_n[:, 0, :N_EXPERTS].reshape(-1)
    cnt_tab = tile_cnt[:, 0, :N_EXPERTS].reshape(-1)
    counts = tile_cnt[-1, 0, :N_EXPERTS] + tile_n[-1, 0, :N_EXPERTS]
    xs = _dispatch_layer(n_tab, cnt_tab, xt, metat, cape)
    tile_blk, tile_e, tile_valid = _expert_tiles(counts, cape, n_steps)
    ys = _expert_layer(tile_blk, tile_e, tile_valid, xs, wg, wu, wd)
    return _combine_layer(n_tab, cnt_tab, xt, meta, ys, l2w, l2b, cape)


def kernel(x, w_in, b_in, mlstm_norm_w, gmlp_norm_w, gmlp_norm_b, gmlp_ws, gmlp_bs, conv_w, conv_b,
           conv_norm_w, conv_norm_b, w_out, ln1_w, ln1_b, router_w, router_b, w_gate, w_up, w_down,
           ln2_w, ln2_b):
    batch, seq, d = x.shape
    mixer_tm = 256

    q_lo, k_lo, v_lo = 0, ML_WIDTH, 2 * ML_WIDTH
    gate_lo = 4 * ML_WIDTH
    rest_lo = gate_lo + 2 * ML_HEADS

    def row_part(t):
        return jnp.concatenate([t[..., q_lo:k_lo], t[..., v_lo:gate_lo], t[..., rest_lo:]], axis=-1)

    def col_part(t):
        return jnp.concatenate([t[..., k_lo:v_lo], t[..., gate_lo:rest_lo]], axis=-1)

    tril = jnp.tril(jnp.ones((CHUNK, CHUNK), gmlp_ws.dtype))
    rw_pad = jnp.pad(router_w.astype(F32), ((0, 0), (0, LANES - N_EXPERTS)))
    rb_pad = jnp.pad(router_b.astype(F32), (0, LANES - N_EXPERTS)).reshape(1, LANES)

    for l in range(DEPTH):
        w_row = row_part(w_in[l]).astype(BF16)
        b_row = row_part(b_in[l]).reshape(1, P_ROW)
        w_col = col_part(w_in[l]).T.astype(BF16)
        b_col = col_part(b_in[l]).reshape(P_COL, 1)
        ws_cat = jnp.transpose(gmlp_ws[l] * tril, (1, 0, 2)).reshape(CHUNK, G_HEADS * CHUNK).astype(BF16)
        bs_full = jnp.repeat(gmlp_bs[l].T, G_HEAD_DIM, axis=1)
        cw = jnp.pad(conv_w[l], ((0, 1), (0, 0)))
        x = _mixer_layer(
            x, w_row, b_row, w_col, b_col, mlstm_norm_w[l].reshape(1, -1), gmlp_norm_w[l].reshape(1, -1),
            gmlp_norm_b[l].reshape(1, -1), ws_cat, bs_full, cw, conv_b[l].reshape(1, -1),
            conv_norm_w[l].reshape(1, -1), conv_norm_b[l].reshape(1, -1), w_out[l].astype(BF16),
            ln1_w[l].reshape(1, -1), ln1_b[l].reshape(1, -1), tm=mixer_tm)
        xt = _moe_layer(x.reshape(batch * seq, d), rw_pad, rb_pad, w_gate[l], w_up[l], w_down[l],
                        ln2_w[l].reshape(1, -1), ln2_b[l].reshape(1, -1))
        x = xt.reshape(batch, seq, d)
    return x
```

```python
import functools

import jax
import jax.numpy as jnp
from jax import lax
from jax.experimental import pallas as pl
from jax.experimental.pallas import tpu as pltpu

D_MODEL = 1024
DEPTH = 4
ML_WIDTH = 512
ML_HEADS = 4
ML_HEAD_DIM = 128
CHUNK = 128
G_WIDTH = 256
G_HEADS = 4
G_HEAD_DIM = 64
C_WIDTH = 256
CONV_WIDTH = 31
N_EXPERTS = 16
N_GROUPS = 4
EXPERTS_PER_GROUP = 4
D_EXPERT = 512
DEEPNORM_ALPHA = (2.0 * DEPTH) ** 0.25
LN_EPS = 1e-5
K_SCALE = ML_HEAD_DIM ** -0.5

Q_LO, V_LO, O_LO, GU_LO, GV_LO, CA_LO, CB_LO = 0, 512, 1024, 1536, 1792, 2048, 2304
P_ROW = 2560
P_COL = ML_WIDTH + 2 * ML_HEADS

LANES = 128
SUBLANES = 8
MOE_TM = 256
ROW_TILE = 512
PAYLOAD_W = D_MODEL + LANES
COMPACT_ROWS = 2 * MOE_TM + LANES
CONV_HALO = 32
VMEM_LIMIT = 52 * 1024 * 1024

F32 = jnp.float32
BF16 = jnp.bfloat16
NEG_INF = float("-inf")


def _layer_norm(x, w, b, eps):
    mu = jnp.mean(x, axis=-1, keepdims=True)
    xc = x - mu
    var = jnp.mean(xc * xc, axis=-1, keepdims=True)
    return xc * lax.rsqrt(var + eps) * w + b


def _gelu_tanh(x):
    return 0.5 * x * (1.0 + jnp.tanh(0.7978845608028654 * (x + 0.044715 * (x * x * x))))


def _log_sigmoid(x):
    return jnp.minimum(x, 0.0) - jnp.log1p(jnp.exp(-jnp.abs(x)))


def _dot(a, b):
    return jnp.dot(a, b, preferred_element_type=F32)


def _dot_nt(a, b):
    return lax.dot_general(a, b, (((1,), (1,)), ((), ())), preferred_element_type=F32)


def _route_tile(x_new, rwt_ref, rb_ref, metat_ref, tile_n_ref, tile_cnt_ref, cnt_ref):
    tm = x_new.shape[0]
    logits = lax.dot_general(rwt_ref[...], x_new, (((1,), (1,)), ((), ())),
                             preferred_element_type=F32, precision=lax.Precision.HIGHEST)
    s = jax.nn.sigmoid(logits)
    sel = s + rb_ref[...]
    e_i = lax.broadcasted_iota(jnp.int32, (N_EXPERTS, tm), 0)
    e_f = e_i.astype(F32)

    best = None
    for g in range(N_GROUPS):
        r = [sel[EXPERTS_PER_GROUP * g + k:EXPERTS_PER_GROUP * g + k + 1, :] for k in range(EXPERTS_PER_GROUP)]
        hi01, lo01 = jnp.maximum(r[0], r[1]), jnp.minimum(r[0], r[1])
        hi23, lo23 = jnp.maximum(r[2], r[3]), jnp.minimum(r[2], r[3])
        score = jnp.maximum(hi01, hi23) + jnp.maximum(jnp.minimum(hi01, hi23), jnp.maximum(lo01, lo23))
        if best is None:
            best, gidx = score, jnp.zeros(score.shape, jnp.int32)
        else:
            better = score > best
            gidx = jnp.where(better, g, gidx)
            best = jnp.where(better, score, best)

    def first_max(v):
        m = jnp.max(v, axis=0, keepdims=True)
        return jnp.min(jnp.where(v == m, e_f, float(N_EXPERTS)), axis=0, keepdims=True)

    in_group = jnp.where(e_i // EXPERTS_PER_GROUP == gidx, sel, NEG_INF)
    e1 = first_max(in_group)
    hit1 = e_f == e1
    e2 = first_max(jnp.where(hit1, NEG_INF, in_group))
    hit2 = e_f == e2
    g1 = jnp.sum(jnp.where(hit1, s, 0.0), axis=0, keepdims=True)
    g2 = jnp.sum(jnp.where(hit2, s, 0.0), axis=0, keepdims=True)
    tot = g1 + g2

    assign = jnp.where(hit1, 1.0, jnp.where(hit2, 1.0, 0.0))
    t_r = lax.broadcasted_iota(jnp.int32, (tm, tm), 0)
    t_c = lax.broadcasted_iota(jnp.int32, (tm, tm), 1)
    earlier = jnp.where(t_r < t_c, 1.0, 0.0).astype(BF16)
    rank = _dot(assign.astype(BF16), earlier)
    n = jnp.sum(assign, axis=1, keepdims=True)
    n_up = jnp.floor((n + (SUBLANES - 1.0)) * (1.0 / SUBLANES)) * SUBLANES
    x_r = lax.broadcasted_iota(jnp.int32, (N_EXPERTS, N_EXPERTS), 0)
    x_c = lax.broadcasted_iota(jnp.int32, (N_EXPERTS, N_EXPERTS), 1)
    lower = jnp.where(x_c < x_r, 1.0, 0.0).astype(BF16)
    n_up_b = jnp.broadcast_to(n_up, (N_EXPERTS, LANES))
    off = _dot(lower, n_up_b.astype(BF16))[:, 0:1]
    pos = off + rank
    cpos1 = jnp.sum(jnp.where(hit1, pos, 0.0), axis=0, keepdims=True)
    cpos2 = jnp.sum(jnp.where(hit2, pos, 0.0), axis=0, keepdims=True)
    metat_ref[...] = jnp.concatenate(
        [cpos1, cpos2, g1 / tot, g2 / tot, jnp.zeros((SUBLANES - 4, tm), F32)], axis=0)

    cnt = cnt_ref[...]
    tile_n_ref[...] = n_up_b.astype(jnp.int32)
    tile_cnt_ref[...] = cnt.astype(jnp.int32)
    cnt_ref[...] = cnt + n_up_b


def _mixer_kernel(x_ref, w_row_ref, b_row_ref, w_col_ref, b_col_ref, mnw_ref, gnw_ref, gnb_ref,
                  ws_ref, bs_ref, cw_ref, cb_ref, cnw_ref, cnb_ref, w_out_ref, l1w_ref, l1b_ref, rwt_ref, rb_ref,
                  o_ref, metat_ref, tile_n_ref, tile_cnt_ref,
                  proj_ref, colp_ref, hcat_ref, cstate_ref, mstate_ref, cbuf_ref, cnt_ref, *, tm):
    n_chunks = tm // CHUNK

    @pl.when(jnp.logical_and(pl.program_id(0) == 0, pl.program_id(1) == 0))
    def _():
        cnt_ref[...] = jnp.zeros_like(cnt_ref)

    @pl.when(pl.program_id(1) == 0)
    def _():
        cstate_ref[...] = jnp.zeros_like(cstate_ref)
        mstate_ref[...] = jnp.zeros_like(mstate_ref)
        cbuf_ref[0:CONV_HALO, :] = jnp.zeros((CONV_HALO, C_WIDTH), F32)

    x = x_ref[...]
    xb = x.astype(BF16)
    proj_ref[...] = _dot(xb, w_row_ref[...]) + b_row_ref[...]
    colp_ref[...] = _dot_nt(w_col_ref[...], xb) + b_col_ref[...]

    gates = colp_ref[ML_WIDTH:P_COL, :]
    lane_in_chunk = lax.broadcasted_iota(jnp.int32, gates.shape, 1) % CHUNK
    logf_all = _log_sigmoid(gates)
    bcum = logf_all
    d = 1
    while d < CHUNK:
        bcum = bcum + jnp.where(lane_in_chunk >= d, pltpu.roll(bcum, d, 1), 0.0)
        d *= 2

    row_i = lax.broadcasted_iota(jnp.int32, (CHUNK, CHUNK), 0)
    col_i = lax.broadcasted_iota(jnp.int32, (CHUNK, CHUNK), 1)
    causal = col_i <= row_i
    diag = col_i == row_i
    ones_col = jnp.where(col_i == 0, 1.0, 0.0).astype(BF16)

    for c in range(n_chunks):
        r0 = c * CHUNK
        rows = slice(r0, r0 + CHUNK)
        logi = gates[0:ML_HEADS, r0:r0 + CHUNK]
        logf = logf_all[ML_HEADS:2 * ML_HEADS, r0:r0 + CHUNK]
        b_row = bcum[ML_HEADS:2 * ML_HEADS, r0:r0 + CHUNK]
        g = jnp.sum(logf, axis=-1, keepdims=True)
        m_prev = mstate_ref[0:ML_HEADS, 0:1]
        a_row = g - b_row + logi
        m_new = jnp.maximum(g + m_prev, jnp.max(a_row, axis=-1, keepdims=True))
        w_row = jnp.exp(a_row - m_new) * K_SCALE
        decay = jnp.exp(g + m_prev - m_new)
        mstate_ref[0:ML_HEADS, :] = jnp.broadcast_to(m_new, (ML_HEADS, LANES))

        for h in range(ML_HEADS):
            hs = slice(h * ML_HEAD_DIM, (h + 1) * ML_HEAD_DIM)
            q = proj_ref[rows, Q_LO + h * ML_HEAD_DIM:Q_LO + (h + 1) * ML_HEAD_DIM].astype(BF16)
            v = proj_ref[rows, V_LO + h * ML_HEAD_DIM:V_LO + (h + 1) * ML_HEAD_DIM].astype(BF16)
            o_gate = proj_ref[rows, O_LO + h * ML_HEAD_DIM:O_LO + (h + 1) * ML_HEAD_DIM]
            kt = colp_ref[hs, r0:r0 + CHUNK]
            v_ext = jnp.concatenate([v, ones_col], axis=1)

            b_r = b_row[h:h + 1, :]
            b_c = jnp.sum(jnp.where(diag, b_r, 0.0), axis=-1, keepdims=True)
            log_d = jnp.where(causal, b_c - b_r + logi[h:h + 1, :], NEG_INF)
            mp = m_prev[h:h + 1, :]
            log_inter = b_c + mp
            m_row = jnp.maximum(log_inter, jnp.max(log_d, axis=-1, keepdims=True))
            s = _dot(q, kt.astype(BF16))
            p = s * (jnp.exp(log_d - m_row) * K_SCALE)
            w_inter = jnp.exp(log_inter - m_row)
            c_ext = cstate_ref[h]
            intra = _dot(p.astype(BF16), v_ext)
            inter = _dot(q, c_ext.astype(BF16))
            tot = intra + w_inter * inter
            den = jnp.maximum(jnp.abs(tot[:, ML_HEAD_DIM:ML_HEAD_DIM + 1]), jnp.exp(-m_row))
            hh = tot[:, 0:ML_HEAD_DIM] / den

            ktw = (kt * w_row[h:h + 1, :]).astype(BF16)
            cstate_ref[h] = decay[h:h + 1, :] * c_ext + _dot(ktw, v_ext)

            mu = jnp.mean(hh, axis=-1, keepdims=True)
            hc = hh - mu
            var = jnp.mean(hc * hc, axis=-1, keepdims=True)
            hn = hc * lax.rsqrt(var + 1e-6) * mnw_ref[:, hs]
            hcat_ref[rows, hs] = (jax.nn.sigmoid(o_gate) * hn).astype(BF16)

        u = _gelu_tanh(proj_ref[rows, GU_LO:GU_LO + G_WIDTH])
        z = _layer_norm(_gelu_tanh(proj_ref[rows, GV_LO:GV_LO + G_WIDTH]), gnw_ref[...], gnb_ref[...], LN_EPS)
        lane_head = lax.broadcasted_iota(jnp.int32, (CHUNK, G_WIDTH), 1) // G_HEAD_DIM
        z_bd = jnp.concatenate(
            [jnp.where(lane_head == h, z, 0.0).astype(BF16) for h in range(G_HEADS)], axis=0)
        zs = _dot(ws_ref[...], z_bd) + bs_ref[...]
        hcat_ref[rows, ML_WIDTH:ML_WIDTH + G_WIDTH] = (u * zs).astype(BF16)

        ca = proj_ref[rows, CA_LO:CA_LO + C_WIDTH]
        cb = proj_ref[rows, CB_LO:CB_LO + C_WIDTH]
        cbuf_ref[CONV_HALO + r0:CONV_HALO + r0 + CHUNK, :] = ca * jax.nn.sigmoid(cb)

    first_tap = CONV_HALO - (CONV_WIDTH - 1)
    for c in range(n_chunks):
        r0 = c * CHUNK
        acc = jnp.zeros((CHUNK, C_WIDTH), F32) + cb_ref[...]
        for k in range(CONV_WIDTH):
            acc = acc + cw_ref[k:k + 1, :] * cbuf_ref[r0 + first_tap + k:r0 + first_tap + k + CHUNK, :]
        cn = _layer_norm(acc, cnw_ref[...], cnb_ref[...], LN_EPS)
        hcat_ref[r0:r0 + CHUNK, ML_WIDTH + G_WIDTH:D_MODEL] = (cn * jax.nn.sigmoid(cn)).astype(BF16)
    cbuf_ref[0:CONV_HALO, :] = cbuf_ref[tm:tm + CONV_HALO, :]

    y = _dot(hcat_ref[...], w_out_ref[...])
    x_new = _layer_norm(DEEPNORM_ALPHA * x + y, l1w_ref[...], l1b_ref[...], LN_EPS)
    o_ref[...] = x_new
    _route_tile(x_new, rwt_ref, rb_ref, metat_ref, tile_n_ref, tile_cnt_ref, cnt_ref)


def _full(shape):
    nd = len(shape)
    return pl.BlockSpec(shape, lambda b, j, _nd=nd: (0,) * _nd)


def _mixer_layer(x, w_row, b_row, w_col, b_col, mnw, gnw, gnb, ws_cat, bs_full, cw, cb, cnw, cnb,
                 w_out, l1w, l1b, rwt, rb_col):
    batch, seq, _ = x.shape
    tm = MOE_TM
    tiles_per_seq = seq // tm
    n_tiles = batch * tiles_per_seq
    kernel = functools.partial(_mixer_kernel, tm=tm)
    weights = (w_row, b_row, w_col, b_col, mnw, gnw, gnb, ws_cat, bs_full, cw, cb, cnw, cnb, w_out, l1w, l1b,
               rwt, rb_col)
    tile_spec = pl.BlockSpec((None, N_EXPERTS, LANES), lambda b, j: (b * tiles_per_seq + j, 0, 0))
    return pl.pallas_call(
        kernel,
        out_shape=(
            jax.ShapeDtypeStruct(x.shape, F32),
            jax.ShapeDtypeStruct((SUBLANES, batch * seq), F32),
            jax.ShapeDtypeStruct((n_tiles, N_EXPERTS, LANES), jnp.int32),
            jax.ShapeDtypeStruct((n_tiles, N_EXPERTS, LANES), jnp.int32),
        ),
        grid=(batch, tiles_per_seq),
        in_specs=[pl.BlockSpec((None, tm, D_MODEL), lambda b, j: (b, j, 0))] + [_full(w.shape) for w in weights],
        out_specs=(
            pl.BlockSpec((None, tm, D_MODEL), lambda b, j: (b, j, 0)),
            pl.BlockSpec((SUBLANES, tm), lambda b, j: (0, b * tiles_per_seq + j)),
            tile_spec,
            tile_spec,
        ),
        scratch_shapes=[
            pltpu.VMEM((tm, P_ROW), F32),
            pltpu.VMEM((P_COL, tm), F32),
            pltpu.VMEM((tm, D_MODEL), BF16),
            pltpu.VMEM((ML_HEADS, ML_HEAD_DIM, 2 * ML_HEAD_DIM), F32),
            pltpu.VMEM((8, LANES), F32),
            pltpu.VMEM((tm + CONV_HALO, C_WIDTH), F32),
            pltpu.VMEM((N_EXPERTS, LANES), F32),
        ],
        compiler_params=pltpu.CompilerParams(
            dimension_semantics=("arbitrary", "arbitrary"), vmem_limit_bytes=VMEM_LIMIT),
        name="mixer",
    )(x, *weights)


def _pow2_pieces(n, largest, act):
    piece = largest
    while piece >= SUBLANES:
        start = jnp.bitwise_and(n, -2 * piece)

        @pl.when(jnp.bitwise_and(n, piece) != 0)
        def _(piece=piece, start=start):
            act(start, piece)
        piece //= 2


def _for_each_run(n_tab, cnt_tab, base_tab, tile, make_copy, act):
    def body(e, off):
        n = n_tab[tile * N_EXPERTS + e]
        base = base_tab[e] + cnt_tab[tile * N_EXPERTS + e]
        _pow2_pieces(n, MOE_TM, lambda start, size: act(make_copy(
            pl.multiple_of(off + start, SUBLANES), pl.multiple_of(base + start, SUBLANES), size)))
        return off + n

    lax.fori_loop(0, N_EXPERTS, body, jnp.int32(0))


def _dispatch_kernel(n_tab, cnt_tab, base_tab, fill_tab, x_ref, metat_ref, xs_hbm, comp_ref, zero_ref, sem,
                     *, n_rows):
    i = pl.program_id(0)
    last = pl.num_programs(0) - 1
    slot = lax.rem(i, 2)

    def copies(tile, slot_, act):
        def make_copy(src_row, dst_row, rows):
            return pltpu.make_async_copy(comp_ref.at[slot_, pl.ds(src_row, rows)],
                                         xs_hbm.at[pl.ds(dst_row, rows)], sem.at[slot_])
        _for_each_run(n_tab, cnt_tab, base_tab, tile, make_copy, act)

    def zero_fill(act):
        def zero_copy(dst_row, rows):
            return pltpu.make_async_copy(zero_ref.at[pl.ds(0, rows)],
                                         xs_hbm.at[pl.ds(pl.multiple_of(dst_row, SUBLANES), rows)], sem.at[2])

        def per_expert(e, carry):
            first = fill_tab[2 * e]
            _pow2_pieces(fill_tab[2 * e + 1], ROW_TILE // 2, lambda start, size: act(zero_copy(first + start, size)))
            return carry

        lax.fori_loop(0, N_EXPERTS, per_expert, jnp.int32(0))

        def per_half_tile(k, carry):
            act(zero_copy(fill_tab[2 * N_EXPERTS] + k * (ROW_TILE // 2), ROW_TILE // 2))
            return carry

        lax.fori_loop(0, (n_rows - fill_tab[2 * N_EXPERTS]) // (ROW_TILE // 2), per_half_tile, jnp.int32(0))

    @pl.when(i == 0)
    def _():
        zero_ref[...] = jnp.zeros_like(zero_ref)
        zero_fill(lambda cp: cp.start())

    @pl.when(i >= 2)
    def _():
        copies(i - 2, slot, lambda cp: cp.wait())

    mt = metat_ref[...]
    row_f = lax.broadcasted_iota(jnp.int32, (COMPACT_ROWS, MOE_TM), 0).astype(F32)
    hit1 = row_f == mt[0:1, :]
    hit2 = row_f == mt[1:2, :]
    onehot = jnp.where(hit1, 1.0, jnp.where(hit2, 1.0, 0.0)).astype(BF16)
    comp_ref[slot, :, 0:D_MODEL] = _dot(onehot, x_ref[...].astype(BF16))
    gate = jnp.sum(jnp.where(hit1, mt[2:3, :], jnp.where(hit2, mt[3:4, :], 0.0)), axis=-1, keepdims=True)
    comp_ref[slot, :, D_MODEL:PAYLOAD_W] = jnp.broadcast_to(gate, (COMPACT_ROWS, LANES))
    copies(i, slot, lambda cp: cp.start())

    @pl.when(i == last)
    def _():
        @pl.when(i >= 1)
        def _():
            copies(i - 1, 1 - slot, lambda cp: cp.wait())
        copies(i, slot, lambda cp: cp.wait())
        zero_fill(lambda cp: cp.wait())


def _dispatch_layer(n_tab, cnt_tab, base_tab, fill_tab, xt, metat, n_rows):
    n_tok = xt.shape[0]
    return pl.pallas_call(
        functools.partial(_dispatch_kernel, n_rows=n_rows),
        out_shape=jax.ShapeDtypeStruct((n_rows, PAYLOAD_W), F32),
        grid_spec=pltpu.PrefetchScalarGridSpec(
            num_scalar_prefetch=4,
            grid=(n_tok // MOE_TM,),
            in_specs=[
                pl.BlockSpec((MOE_TM, D_MODEL), lambda i, *_: (i, 0)),
                pl.BlockSpec((SUBLANES, MOE_TM), lambda i, *_: (0, i)),
            ],
            out_specs=pl.BlockSpec(memory_space=pl.ANY),
            scratch_shapes=[
                pltpu.VMEM((2, COMPACT_ROWS, PAYLOAD_W), F32),
                pltpu.VMEM((ROW_TILE // 2, PAYLOAD_W), F32),
                pltpu.SemaphoreType.DMA((3,)),
            ],
        ),
        compiler_params=pltpu.CompilerParams(dimension_semantics=("arbitrary",), vmem_limit_bytes=VMEM_LIMIT),
        name="dispatch",
    )(n_tab, cnt_tab, base_tab, fill_tab, xt, metat)


def _expert_kernel(tile_e, tile_valid, xs_ref, wg_ref, wu_ref, wd_ref, ys_ref, wgb_ref, wub_ref, wdb_ref):
    i = pl.program_id(0)
    valid = tile_valid[i]
    new_expert = jnp.logical_or(i == 0, tile_e[i] != tile_e[jnp.maximum(i - 1, 0)])

    @pl.when(jnp.logical_and(valid > 0, new_expert))
    def _():
        wgb_ref[...] = wg_ref[...].astype(BF16)
        wub_ref[...] = wu_ref[...].astype(BF16)
        wdb_ref[...] = wd_ref[...].astype(BF16)

    @pl.when(valid > 0)
    def _():
        xb = xs_ref[:, 0:D_MODEL].astype(BF16)
        g = _dot(xb, wgb_ref[...])
        u = _dot(xb, wub_ref[...])
        hid = (g * jax.nn.sigmoid(g) * u).astype(BF16)
        ys_ref[...] = xs_ref[:, D_MODEL:D_MODEL + 1] * _dot(hid, wdb_ref[...])

    @pl.when(valid == 0)
    def _():
        ys_ref[...] = jnp.zeros_like(ys_ref)


def _expert_layer(tile_e, tile_valid, xs, wg, wu, wd):
    n_steps = tile_e.shape[0]
    return pl.pallas_call(
        _expert_kernel,
        out_shape=jax.ShapeDtypeStruct((xs.shape[0], D_MODEL), F32),
        grid_spec=pltpu.PrefetchScalarGridSpec(
            num_scalar_prefetch=2,
            grid=(n_steps,),
            in_specs=[
                pl.BlockSpec((ROW_TILE, PAYLOAD_W), lambda i, e, v: (i, 0)),
                pl.BlockSpec((None, D_MODEL, D_EXPERT), lambda i, e, v: (e[i], 0, 0)),
                pl.BlockSpec((None, D_MODEL, D_EXPERT), lambda i, e, v: (e[i], 0, 0)),
                pl.BlockSpec((None, D_EXPERT, D_MODEL), lambda i, e, v: (e[i], 0, 0)),
            ],
            out_specs=pl.BlockSpec((ROW_TILE, D_MODEL), lambda i, e, v: (i, 0)),
            scratch_shapes=[
                pltpu.VMEM((D_MODEL, D_EXPERT), BF16),
                pltpu.VMEM((D_MODEL, D_EXPERT), BF16),
                pltpu.VMEM((D_EXPERT, D_MODEL), BF16),
            ],
        ),
        compiler_params=pltpu.CompilerParams(dimension_semantics=("arbitrary",), vmem_limit_bytes=VMEM_LIMIT),
        name="experts",
    )(tile_e, tile_valid, xs, wg, wu, wd)


def _combine_kernel(n_tab, cnt_tab, base_tab, x_ref, metat_ref, ys_hbm, l2w_ref, l2b_ref, o_ref, yc_ref, sem):
    i = pl.program_id(0)
    n_steps = pl.num_programs(0)
    slot = lax.rem(i, 2)

    def copies(tile, slot_, act):
        def make_copy(buf_row, ys_row, rows):
            return pltpu.make_async_copy(ys_hbm.at[pl.ds(ys_row, rows)],
                                         yc_ref.at[slot_, pl.ds(buf_row, rows)], sem.at[slot_])
        _for_each_run(n_tab, cnt_tab, base_tab, tile, make_copy, act)

    @pl.when(i == 0)
    def _():
        yc_ref[...] = jnp.zeros_like(yc_ref)
        copies(0, 0, lambda cp: cp.start())

    @pl.when(i + 1 < n_steps)
    def _():
        copies(i + 1, 1 - slot, lambda cp: cp.start())

    copies(i, slot, lambda cp: cp.wait())

    mt = metat_ref[...]
    t_r = lax.broadcasted_iota(jnp.int32, (MOE_TM, MOE_TM), 0)
    t_c = lax.broadcasted_iota(jnp.int32, (MOE_TM, MOE_TM), 1)
    as_col = lambda row: jnp.sum(jnp.where(t_r == t_c, row, 0.0), axis=-1, keepdims=True)
    col_f = lax.broadcasted_iota(jnp.int32, (MOE_TM, COMPACT_ROWS), 1).astype(F32)
    onehot = jnp.where(col_f == as_col(mt[0:1, :]), 1.0,
                       jnp.where(col_f == as_col(mt[1:2, :]), 1.0, 0.0)).astype(BF16)
    y = _dot(onehot, yc_ref[slot].astype(BF16))
    o_ref[...] = _layer_norm(DEEPNORM_ALPHA * x_ref[...] + y, l2w_ref[...], l2b_ref[...], LN_EPS)


def _combine_layer(n_tab, cnt_tab, base_tab, xt, metat, ys, l2w, l2b):
    n_tok = xt.shape[0]
    return pl.pallas_call(
        _combine_kernel,
        out_shape=jax.ShapeDtypeStruct(xt.shape, F32),
        grid_spec=pltpu.PrefetchScalarGridSpec(
            num_scalar_prefetch=3,
            grid=(n_tok // MOE_TM,),
            in_specs=[
                pl.BlockSpec((MOE_TM, D_MODEL), lambda i, *_: (i, 0)),
                pl.BlockSpec((SUBLANES, MOE_TM), lambda i, *_: (0, i)),
                pl.BlockSpec(memory_space=pl.ANY),
                pl.BlockSpec((1, D_MODEL), lambda i, *_: (0, 0)),
                pl.BlockSpec((1, D_MODEL), lambda i, *_: (0, 0)),
            ],
            out_specs=pl.BlockSpec((MOE_TM, D_MODEL), lambda i, *_: (i, 0)),
            scratch_shapes=[pltpu.VMEM((2, COMPACT_ROWS, D_MODEL), F32), pltpu.SemaphoreType.DMA((2,))],
        ),
        compiler_params=pltpu.CompilerParams(dimension_semantics=("arbitrary",), vmem_limit_bytes=VMEM_LIMIT),
        name="combine",
    )(n_tab, cnt_tab, base_tab, xt, metat, ys, l2w, l2b)


def _expert_plan(counts, n_steps):
    tiles_e = (counts + (ROW_TILE - 1)) // ROW_TILE
    cum = jnp.cumsum(tiles_e)
    first_tile = cum - tiles_e
    total = cum[-1]
    step = jnp.arange(n_steps, dtype=jnp.int32)
    owner = (step[:, None] >= cum[None, :]).sum(axis=1).astype(jnp.int32)
    is_owner = owner[:, None] == jnp.arange(N_EXPERTS, dtype=jnp.int32)[None, :]
    local = step - jnp.where(is_owner, first_tile[None, :], 0).sum(axis=1)
    rows_left = jnp.where(is_owner, counts[None, :], 0).sum(axis=1) - local * ROW_TILE
    valid = jnp.where(step < total, jnp.clip(rows_left, 0, ROW_TILE), 0).astype(jnp.int32)
    tile_e = jnp.minimum(owner, N_EXPERTS - 1)
    base = (first_tile * ROW_TILE).astype(jnp.int32)
    fill = jnp.stack([base + counts, tiles_e * ROW_TILE - counts], axis=1).reshape(-1)
    fill_tab = jnp.concatenate([fill, (total * ROW_TILE)[None]]).astype(jnp.int32)
    return base, tile_e, valid, fill_tab


def _moe_layer(xt, metat, tile_n, tile_cnt, wg, wu, wd, l2w, l2b):
    n_tok = xt.shape[0]
    n_tiles = n_tok // MOE_TM
    n_steps = (2 * n_tok + N_EXPERTS * (SUBLANES - 1) * n_tiles) // ROW_TILE + N_EXPERTS
    n_tab = tile_n[:, :, 0].reshape(-1)
    cnt_tab = tile_cnt[:, :, 0].reshape(-1)
    counts = tile_cnt[-1, :, 0] + tile_n[-1, :, 0]
    base_tab, tile_e, tile_valid, fill_tab = _expert_plan(counts, n_steps)
    xs = _dispatch_layer(n_tab, cnt_tab, base_tab, fill_tab, xt, metat, n_steps * ROW_TILE)
    ys = _expert_layer(tile_e, tile_valid, xs, wg, wu, wd)
    return _combine_layer(n_tab, cnt_tab, base_tab, xt, metat, ys, l2w, l2b)


def kernel(x, w_in, b_in, mlstm_norm_w, gmlp_norm_w, gmlp_norm_b, gmlp_ws, gmlp_bs, conv_w, conv_b,
           conv_norm_w, conv_norm_b, w_out, ln1_w, ln1_b, router_w, router_b, w_gate, w_up, w_down,
           ln2_w, ln2_b):
    batch, seq, d = x.shape

    q_lo, k_lo, v_lo = 0, ML_WIDTH, 2 * ML_WIDTH
    gate_lo = 4 * ML_WIDTH
    rest_lo = gate_lo + 2 * ML_HEADS

    def row_part(t):
        return jnp.concatenate([t[..., q_lo:k_lo], t[..., v_lo:gate_lo], t[..., rest_lo:]], axis=-1)

    def col_part(t):
        return jnp.concatenate([t[..., k_lo:v_lo], t[..., gate_lo:rest_lo]], axis=-1)

    tril = jnp.tril(jnp.ones((CHUNK, CHUNK), gmlp_ws.dtype))
    rwt = router_w.astype(F32).T
    rb_col = router_b.astype(F32).reshape(N_EXPERTS, 1)

    for l in range(DEPTH):
        w_row = row_part(w_in[l]).astype(BF16)
        b_row = row_part(b_in[l]).reshape(1, P_ROW)
        w_col = col_part(w_in[l]).T.astype(BF16)
        b_col = col_part(b_in[l]).reshape(P_COL, 1)
        ws_cat = jnp.transpose(gmlp_ws[l] * tril, (1, 0, 2)).reshape(CHUNK, G_HEADS * CHUNK).astype(BF16)
        bs_full = jnp.repeat(gmlp_bs[l].T, G_HEAD_DIM, axis=1)
        cw = jnp.pad(conv_w[l], ((0, 1), (0, 0)))
        x, metat, tile_n, tile_cnt = _mixer_layer(
            x, w_row, b_row, w_col, b_col, mlstm_norm_w[l].reshape(1, -1), gmlp_norm_w[l].reshape(1, -1),
            gmlp_norm_b[l].reshape(1, -1), ws_cat, bs_full, cw, conv_b[l].reshape(1, -1),
            conv_norm_w[l].reshape(1, -1), conv_norm_b[l].reshape(1, -1), w_out[l].astype(BF16),
            ln1_w[l].reshape(1, -1), ln1_b[l].reshape(1, -1), rwt, rb_col)
        xt = _moe_layer(x.reshape(batch * seq, d), metat, tile_n, tile_cnt, w_gate[l], w_up[l], w_down[l],
                        ln2_w[l].reshape(1, -1), ln2_b[l].reshape(1, -1))
        x = xt.reshape(batch, seq, d)
    return x
```

```python
import functools

import jax
import jax.numpy as jnp
from jax import lax
from jax.experimental import pallas as pl
from jax.experimental.pallas import tpu as pltpu

D_MODEL = 1024
DEPTH = 4
ML_WIDTH = 512
ML_HEADS = 4
ML_HEAD_DIM = 128
CHUNK = 128
G_WIDTH = 256
G_HEADS = 4
G_HEAD_DIM = 64
C_WIDTH = 256
CONV_WIDTH = 31
N_EXPERTS = 16
N_GROUPS = 4
EXPERTS_PER_GROUP = 4
D_EXPERT = 512
DEEPNORM_ALPHA = (2.0 * DEPTH) ** 0.25
LN_EPS = 1e-5
K_SCALE = ML_HEAD_DIM ** -0.5

Q_LO, V_LO, O_LO, GU_LO, GV_LO, CA_LO, CB_LO = 0, 512, 1024, 1536, 1792, 2048, 2304
P_ROW = 2560
P_COL = ML_WIDTH + 2 * ML_HEADS

LANES = 128
SUBLANES = 8
MOE_TM = 256
ROW_TILE = 512
PAYLOAD_W = D_MODEL + LANES
COMPACT_ROWS = 2 * MOE_TM + LANES
CONV_HALO = 32
CONV_SPAN = CONV_HALO - SUBLANES
STREAMS = 2
VMEM_LIMIT = 58 * 1024 * 1024

F32 = jnp.float32
BF16 = jnp.bfloat16
NEG_INF = float("-inf")


def _layer_norm(x, w, b, eps):
    mu = jnp.mean(x, axis=-1, keepdims=True)
    xc = x - mu
    var = jnp.mean(xc * xc, axis=-1, keepdims=True)
    return xc * lax.rsqrt(var + eps) * w + b


def _gelu_tanh(x):
    return 0.5 * x * (1.0 + jnp.tanh(0.7978845608028654 * (x + 0.044715 * (x * x * x))))


def _log_sigmoid(x):
    return jnp.minimum(x, 0.0) - jnp.log1p(jnp.exp(-jnp.abs(x)))


def _dot(a, b):
    return jnp.dot(a, b, preferred_element_type=F32)


def _dot_nt(a, b):
    return lax.dot_general(a, b, (((1,), (1,)), ((), ())), preferred_element_type=F32)


def _router_logits(x_new, rw_ref):
    xh = x_new.astype(BF16)
    xl = (x_new - xh.astype(F32)).astype(BF16)
    head = _dot(xh, rw_ref[...])
    return head[:, 0:LANES] + head[:, LANES:2 * LANES] + _dot(xl, rw_ref[:, 0:LANES])


def _route_tile(logits, rb_ref, metat_ref, tile_n_ref, tile_cnt_ref, cnt_ref):
    tm = logits.shape[1]
    s_all = jax.nn.sigmoid(logits)
    sel_all = s_all + rb_ref[...]
    rows_of = lambda v: [v[k:k + 1, :] for k in range(N_EXPERTS)]
    s, sel = rows_of(s_all), rows_of(sel_all)

    best = None
    for g in range(N_GROUPS):
        r = sel[EXPERTS_PER_GROUP * g:EXPERTS_PER_GROUP * (g + 1)]
        hi01, lo01 = jnp.maximum(r[0], r[1]), jnp.minimum(r[0], r[1])
        hi23, lo23 = jnp.maximum(r[2], r[3]), jnp.minimum(r[2], r[3])
        score = jnp.maximum(hi01, hi23) + jnp.maximum(jnp.minimum(hi01, hi23), jnp.maximum(lo01, lo23))
        if best is None:
            best, gidx = score, jnp.zeros(score.shape, jnp.int32)
        else:
            better = score > best
            gidx = jnp.where(better, g, gidx)
            best = jnp.where(better, score, best)

    def of_group(rows):
        out = []
        for j in range(EXPERTS_PER_GROUP):
            v = rows[j]
            for g in range(1, N_GROUPS):
                v = jnp.where(gidx == g, rows[EXPERTS_PER_GROUP * g + j], v)
            out.append(v)
        return out

    def first_max(vals):
        best_v, best_j = vals[0], jnp.zeros(vals[0].shape, F32)
        for j in range(1, len(vals)):
            better = vals[j] > best_v
            best_j = jnp.where(better, float(j), best_j)
            best_v = jnp.where(better, vals[j], best_v)
        return best_j

    def take(vals, idx):
        v = vals[0]
        for j in range(1, len(vals)):
            v = jnp.where(idx == float(j), vals[j], v)
        return v

    cand, cand_s = of_group(sel), of_group(s)
    j1 = first_max(cand)
    j2 = first_max([jnp.where(j1 == float(j), NEG_INF, cand[j]) for j in range(EXPERTS_PER_GROUP)])
    g1, g2 = take(cand_s, j1), take(cand_s, j2)
    tot = g1 + g2
    first_of_group = gidx.astype(F32) * float(EXPERTS_PER_GROUP)
    e1, e2 = first_of_group + j1, first_of_group + j2

    e_f = lax.broadcasted_iota(jnp.int32, (N_EXPERTS, tm), 0).astype(F32)
    assign = jnp.where(e_f == e1, 1.0, jnp.where(e_f == e2, 1.0, 0.0)).astype(BF16)
    t_r = lax.broadcasted_iota(jnp.int32, (tm, tm), 0)
    t_c = lax.broadcasted_iota(jnp.int32, (tm, tm), 1)
    earlier = jnp.where(t_r < t_c, 1.0, 0.0).astype(BF16)
    rank = _dot(assign, earlier)
    n_b = _dot(assign, jnp.ones((tm, LANES), BF16))
    n_up_b = jnp.floor((n_b + (SUBLANES - 1.0)) * (1.0 / SUBLANES)) * SUBLANES
    x_r = lax.broadcasted_iota(jnp.int32, (N_EXPERTS, N_EXPERTS), 0)
    x_c = lax.broadcasted_iota(jnp.int32, (N_EXPERTS, N_EXPERTS), 1)
    lower = jnp.where(x_c < x_r, 1.0, 0.0).astype(BF16)
    off_b = _dot(lower, n_up_b.astype(BF16))
    pos = rows_of(jnp.concatenate([off_b] * (tm // LANES), axis=1) + rank)
    metat_ref[...] = jnp.concatenate(
        [take(pos, e1), take(pos, e2), g1 / tot, g2 / tot, jnp.zeros((SUBLANES - 4, tm), F32)], axis=0)

    cnt = cnt_ref[...]
    tile_n_ref[...] = n_up_b.astype(jnp.int32)
    tile_cnt_ref[...] = cnt.astype(jnp.int32)
    cnt_ref[...] = cnt + n_up_b


def _project(x_tile, w_row_ref, b_row_ref, w_col_ref, b_col_ref, proj_ref, colp_ref):
    xb = x_tile.astype(BF16)
    proj_ref[...] = _dot(xb, w_row_ref[...]) + b_row_ref[...]
    colp_ref[...] = _dot_nt(w_col_ref[...], xb) + b_col_ref[...]


def _mix_tile(x, proj_ref, colp_ref, mnw_ref, gnw_ref, gnb_ref, ws_ref, bs_ref, cw_ref, cb_ref, cnw_ref,
              cnb_ref, w_out_ref, l1w_ref, l1b_ref, hcat_ref, cstate_ref, mstate_ref, cbuf_ref, shift_ref, tm):
    n_chunks = tm // CHUNK
    gates = colp_ref[ML_WIDTH:P_COL, :]
    lane_in_chunk = lax.broadcasted_iota(jnp.int32, gates.shape, 1) % CHUNK
    logf_all = _log_sigmoid(gates)
    bcum = logf_all
    d = 1
    while d < CHUNK:
        bcum = bcum + jnp.where(lane_in_chunk >= d, pltpu.roll(bcum, d, 1), 0.0)
        d *= 2

    row_i = lax.broadcasted_iota(jnp.int32, (CHUNK, CHUNK), 0)
    col_i = lax.broadcasted_iota(jnp.int32, (CHUNK, CHUNK), 1)
    causal = col_i <= row_i
    diag = col_i == row_i
    ones_col = jnp.where(col_i == 0, 1.0, 0.0).astype(BF16)

    for sc in range(STREAMS * n_chunks):
        st, c = divmod(sc, n_chunks)
        r0 = st * tm + c * CHUNK
        rows = slice(r0, r0 + CHUNK)
        m_rows = slice(st * SUBLANES, st * SUBLANES + ML_HEADS)
        logi = gates[0:ML_HEADS, r0:r0 + CHUNK]
        logf = logf_all[ML_HEADS:2 * ML_HEADS, r0:r0 + CHUNK]
        b_row = bcum[ML_HEADS:2 * ML_HEADS, r0:r0 + CHUNK]
        g = jnp.sum(logf, axis=-1, keepdims=True)
        m_prev = mstate_ref[m_rows, 0:1]
        a_row = g - b_row + logi
        m_new = jnp.maximum(g + m_prev, jnp.max(a_row, axis=-1, keepdims=True))
        w_row = jnp.exp(a_row - m_new) * K_SCALE
        decay = jnp.exp(g + m_prev - m_new)
        mstate_ref[m_rows, :] = jnp.broadcast_to(m_new, (ML_HEADS, LANES))

        for h in range(ML_HEADS):
            hs = slice(h * ML_HEAD_DIM, (h + 1) * ML_HEAD_DIM)
            q = proj_ref[rows, Q_LO + h * ML_HEAD_DIM:Q_LO + (h + 1) * ML_HEAD_DIM].astype(BF16)
            v = proj_ref[rows, V_LO + h * ML_HEAD_DIM:V_LO + (h + 1) * ML_HEAD_DIM].astype(BF16)
            o_gate = proj_ref[rows, O_LO + h * ML_HEAD_DIM:O_LO + (h + 1) * ML_HEAD_DIM]
            kt = colp_ref[hs, r0:r0 + CHUNK]
            v_ext = jnp.concatenate([v, ones_col], axis=1)

            b_r = b_row[h:h + 1, :]
            b_c = jnp.sum(jnp.where(diag, b_r, 0.0), axis=-1, keepdims=True)
            log_d = jnp.where(causal, b_c - b_r + logi[h:h + 1, :], NEG_INF)
            mp = m_prev[h:h + 1, :]
            log_inter = b_c + mp
            m_row = jnp.maximum(log_inter, jnp.max(log_d, axis=-1, keepdims=True))
            s = _dot(q, kt.astype(BF16))
            p = s * (jnp.exp(log_d - m_row) * K_SCALE)
            w_inter = jnp.exp(log_inter - m_row)
            c_ext = cstate_ref[st * ML_HEADS + h]
            intra = _dot(p.astype(BF16), v_ext)
            inter = _dot(q, c_ext.astype(BF16))
            tot = intra + w_inter * inter
            den = jnp.maximum(jnp.abs(tot[:, ML_HEAD_DIM:ML_HEAD_DIM + 1]), jnp.exp(-m_row))
            hh = tot[:, 0:ML_HEAD_DIM] / den

            ktw = (kt * w_row[h:h + 1, :]).astype(BF16)
            cstate_ref[st * ML_HEADS + h] = decay[h:h + 1, :] * c_ext + _dot(ktw, v_ext)

            mu = jnp.mean(hh, axis=-1, keepdims=True)
            hc = hh - mu
            var = jnp.mean(hc * hc, axis=-1, keepdims=True)
            hn = hc * lax.rsqrt(var + 1e-6) * mnw_ref[:, hs]
            hcat_ref[rows, hs] = (jax.nn.sigmoid(o_gate) * hn).astype(BF16)

        u = _gelu_tanh(proj_ref[rows, GU_LO:GU_LO + G_WIDTH])
        z = _layer_norm(_gelu_tanh(proj_ref[rows, GV_LO:GV_LO + G_WIDTH]), gnw_ref[...], gnb_ref[...], LN_EPS)
        lane_head = lax.broadcasted_iota(jnp.int32, (CHUNK, G_WIDTH), 1) // G_HEAD_DIM
        z_bd = jnp.concatenate(
            [jnp.where(lane_head == h, z, 0.0).astype(BF16) for h in range(G_HEADS)], axis=0)
        zs = _dot(ws_ref[...], z_bd) + bs_ref[...]
        hcat_ref[rows, ML_WIDTH:ML_WIDTH + G_WIDTH] = (u * zs).astype(BF16)

        ca = proj_ref[rows, CA_LO:CA_LO + C_WIDTH]
        cb = proj_ref[rows, CB_LO:CB_LO + C_WIDTH]
        cbuf_ref[st, CONV_HALO + c * CHUNK:CONV_HALO + (c + 1) * CHUNK, :] = ca * jax.nn.sigmoid(cb)

    first_tap = CONV_HALO - (CONV_WIDTH - 1)
    for st in range(STREAMS):
        for sh in range(1, SUBLANES):
            shift_ref[st, sh - 1] = cbuf_ref[st, sh:sh + tm + CONV_SPAN, :]
        for c in range(n_chunks):
            acc = jnp.zeros((CHUNK, C_WIDTH), F32) + cb_ref[...]
            for k in range(CONV_WIDTH):
                whole, sh = divmod(first_tap + k, SUBLANES)
                lo = c * CHUNK + whole * SUBLANES
                tap = cbuf_ref[st, lo:lo + CHUNK, :] if sh == 0 else shift_ref[st, sh - 1, lo:lo + CHUNK, :]
                acc = acc + cw_ref[k:k + 1, :] * tap
            cn = _layer_norm(acc, cnw_ref[...], cnb_ref[...], LN_EPS)
            r0 = st * tm + c * CHUNK
            hcat_ref[r0:r0 + CHUNK, ML_WIDTH + G_WIDTH:D_MODEL] = (cn * jax.nn.sigmoid(cn)).astype(BF16)
        cbuf_ref[st, 0:CONV_HALO, :] = cbuf_ref[st, tm:tm + CONV_HALO, :]

    y = _dot(hcat_ref[...], w_out_ref[...])
    x_new = _layer_norm(DEEPNORM_ALPHA * x + y, l1w_ref[...], l1b_ref[...], LN_EPS)
    return x_new


def _mixer_kernel(x_ref, w_row_ref, b_row_ref, w_col_ref, b_col_ref, mnw_ref, gnw_ref, gnb_ref,
                  ws_ref, bs_ref, cw_ref, cb_ref, cnw_ref, cnb_ref, w_out_ref, l1w_ref, l1b_ref, rw_ref, rb_ref,
                  o_ref, metat_ref, tile_n_ref, tile_cnt_ref,
                  proj_ref, colp_ref, hcat_ref, cstate_ref, mstate_ref, cbuf_ref, shift_ref, cnt_ref,
                  *, tm, steps_per_seq):
    i = pl.program_id(0)

    @pl.when(i == 0)
    def _():
        cnt_ref[...] = jnp.zeros_like(cnt_ref)

    @pl.when(i % steps_per_seq == 0)
    def _():
        cstate_ref[...] = jnp.zeros_like(cstate_ref)
        mstate_ref[...] = jnp.zeros_like(mstate_ref)
        cbuf_ref[:, 0:CONV_HALO, :] = jnp.zeros((STREAMS, CONV_HALO, C_WIDTH), F32)

    x = x_ref[...].reshape(STREAMS * tm, D_MODEL)
    _project(x, w_row_ref, b_row_ref, w_col_ref, b_col_ref, proj_ref, colp_ref)
    x_new = _mix_tile(x, proj_ref, colp_ref, mnw_ref, gnw_ref, gnb_ref, ws_ref, bs_ref, cw_ref, cb_ref, cnw_ref,
                      cnb_ref, w_out_ref, l1w_ref, l1b_ref, hcat_ref, cstate_ref, mstate_ref, cbuf_ref, shift_ref,
                      tm)
    o_ref[...] = x_new.reshape(STREAMS, tm, D_MODEL)
    logits = _router_logits(x_new, rw_ref)
    for st in range(STREAMS):
        _route_tile(logits[st * tm:(st + 1) * tm, :].T[0:N_EXPERTS, :], rb_ref, metat_ref.at[st], tile_n_ref.at[st],
                    tile_cnt_ref.at[st], cnt_ref)


def _full(shape):
    nd = len(shape)
    return pl.BlockSpec(shape, lambda i, _nd=nd: (0,) * _nd, pipeline_mode=pl.Buffered(1))


def _mixer_layer(x, w_row, b_row, w_col, b_col, mnw, gnw, gnb, ws_cat, bs_full, cw, cb, cnw, cnb,
                 w_out, l1w, l1b, rw_split, rb_col):
    batch, seq, _ = x.shape
    tm = MOE_TM
    steps_per_seq = seq // tm
    n_steps = batch // STREAMS * steps_per_seq
    n_tiles = batch * seq // tm
    rows = STREAMS * tm
    kernel = functools.partial(_mixer_kernel, tm=tm, steps_per_seq=steps_per_seq)
    weights = (w_row, b_row, w_col, b_col, mnw, gnw, gnb, ws_cat, bs_full, cw, cb, cnw, cnb, w_out, l1w, l1b,
               rw_split, rb_col)
    tile_spec = pl.BlockSpec((STREAMS, N_EXPERTS, LANES), lambda i: (i, 0, 0))
    step_spec = pl.BlockSpec((STREAMS, tm, D_MODEL), lambda i: (i // steps_per_seq, i % steps_per_seq, 0))
    return pl.pallas_call(
        kernel,
        out_shape=(
            jax.ShapeDtypeStruct(x.shape, F32),
            jax.ShapeDtypeStruct((n_tiles, SUBLANES, tm), F32),
            jax.ShapeDtypeStruct((n_tiles, N_EXPERTS, LANES), jnp.int32),
            jax.ShapeDtypeStruct((n_tiles, N_EXPERTS, LANES), jnp.int32),
        ),
        grid=(n_steps,),
        in_specs=[step_spec] + [_full(w.shape) for w in weights],
        out_specs=(step_spec, pl.BlockSpec((STREAMS, SUBLANES, tm), lambda i: (i, 0, 0)), tile_spec, tile_spec),
        scratch_shapes=[
            pltpu.VMEM((rows, P_ROW), F32),
            pltpu.VMEM((P_COL, rows), F32),
            pltpu.VMEM((rows, D_MODEL), BF16),
            pltpu.VMEM((STREAMS * ML_HEADS, ML_HEAD_DIM, 2 * ML_HEAD_DIM), F32),
            pltpu.VMEM((STREAMS * SUBLANES, LANES), F32),
            pltpu.VMEM((STREAMS, tm + CONV_HALO, C_WIDTH), F32),
            pltpu.VMEM((STREAMS, SUBLANES - 1, tm + CONV_SPAN, C_WIDTH), F32),
            pltpu.VMEM((N_EXPERTS, LANES), F32),
        ],
        compiler_params=pltpu.CompilerParams(
            dimension_semantics=("arbitrary",), vmem_limit_bytes=VMEM_LIMIT),
        name="mixer",
    )(x, *weights)


def _pow2_pieces(n, largest, act):
    piece = largest
    while piece >= SUBLANES:
        start = jnp.bitwise_and(n, -2 * piece)

        @pl.when(jnp.bitwise_and(n, piece) != 0)
        def _(piece=piece, start=start):
            act(start, piece)
        piece //= 2


def _for_each_run(n_tab, cnt_tab, base_tab, tile, make_copy, act):
    def body(e, off):
        n = n_tab[tile * N_EXPERTS + e]
        base = base_tab[e] + cnt_tab[tile * N_EXPERTS + e]
        _pow2_pieces(n, MOE_TM, lambda start, size: act(make_copy(
            pl.multiple_of(off + start, SUBLANES), pl.multiple_of(base + start, SUBLANES), size)))
        return off + n

    lax.fori_loop(0, N_EXPERTS, body, jnp.int32(0))


def _dispatch_kernel(n_tab, cnt_tab, base_tab, fill_tab, x_ref, metat_ref, xs_hbm, comp_ref, zero_ref, sem,
                     *, n_rows):
    i = pl.program_id(0)
    last = pl.num_programs(0) - 1
    slot = lax.rem(i, 2)

    def copies(tile, slot_, act):
        def make_copy(src_row, dst_row, rows):
            return pltpu.make_async_copy(comp_ref.at[slot_, pl.ds(src_row, rows)],
                                         xs_hbm.at[pl.ds(dst_row, rows)], sem.at[slot_])
        _for_each_run(n_tab, cnt_tab, base_tab, tile, make_copy, act)

    def zero_fill(act):
        def zero_copy(dst_row, rows):
            return pltpu.make_async_copy(zero_ref.at[pl.ds(0, rows)],
                                         xs_hbm.at[pl.ds(pl.multiple_of(dst_row, SUBLANES), rows)], sem.at[2])

        def per_expert(e, carry):
            first = fill_tab[2 * e]
            _pow2_pieces(fill_tab[2 * e + 1], ROW_TILE // 2, lambda start, size: act(zero_copy(first + start, size)))
            return carry

        lax.fori_loop(0, N_EXPERTS, per_expert, jnp.int32(0))

        def per_half_tile(k, carry):
            act(zero_copy(fill_tab[2 * N_EXPERTS] + k * (ROW_TILE // 2), ROW_TILE // 2))
            return carry

        lax.fori_loop(0, (n_rows - fill_tab[2 * N_EXPERTS]) // (ROW_TILE // 2), per_half_tile, jnp.int32(0))

    @pl.when(i == 0)
    def _():
        zero_ref[...] = jnp.zeros_like(zero_ref)
        zero_fill(lambda cp: cp.start())

    @pl.when(i >= 2)
    def _():
        copies(i - 2, slot, lambda cp: cp.wait())

    mt = metat_ref[...]
    row_f = lax.broadcasted_iota(jnp.int32, (COMPACT_ROWS, MOE_TM), 0).astype(F32)
    hit1 = row_f == mt[0:1, :]
    hit2 = row_f == mt[1:2, :]
    onehot = jnp.where(hit1, 1.0, jnp.where(hit2, 1.0, 0.0)).astype(BF16)
    comp_ref[slot, :, 0:D_MODEL] = _dot(onehot, x_ref[...].astype(BF16))
    gate = jnp.sum(jnp.where(hit1, mt[2:3, :], jnp.where(hit2, mt[3:4, :], 0.0)), axis=-1, keepdims=True)
    comp_ref[slot, :, D_MODEL:PAYLOAD_W] = jnp.broadcast_to(gate, (COMPACT_ROWS, LANES))
    copies(i, slot, lambda cp: cp.start())

    @pl.when(i == last)
    def _():
        @pl.when(i >= 1)
        def _():
            copies(i - 1, 1 - slot, lambda cp: cp.wait())
        copies(i, slot, lambda cp: cp.wait())
        zero_fill(lambda cp: cp.wait())


def _token_tile_spec(seq):
    tiles_per_seq = seq // MOE_TM

    def index_map(i, *_):
        group, stream = i // STREAMS, i % STREAMS
        return (group // tiles_per_seq * STREAMS + stream, group % tiles_per_seq, 0)

    return pl.BlockSpec((None, MOE_TM, D_MODEL), index_map)


def _dispatch_layer(n_tab, cnt_tab, base_tab, fill_tab, x, metat, n_rows):
    batch, seq, _ = x.shape
    return pl.pallas_call(
        functools.partial(_dispatch_kernel, n_rows=n_rows),
        out_shape=jax.ShapeDtypeStruct((n_rows, PAYLOAD_W), F32),
        grid_spec=pltpu.PrefetchScalarGridSpec(
            num_scalar_prefetch=4,
            grid=(batch * seq // MOE_TM,),
            in_specs=[
                _token_tile_spec(seq),
                pl.BlockSpec((None, SUBLANES, MOE_TM), lambda i, *_: (i, 0, 0)),
            ],
            out_specs=pl.BlockSpec(memory_space=pl.ANY),
            scratch_shapes=[
                pltpu.VMEM((2, COMPACT_ROWS, PAYLOAD_W), F32),
                pltpu.VMEM((ROW_TILE // 2, PAYLOAD_W), F32),
                pltpu.SemaphoreType.DMA((3,)),
            ],
        ),
        compiler_params=pltpu.CompilerParams(dimension_semantics=("arbitrary",), vmem_limit_bytes=VMEM_LIMIT),
        name="dispatch",
    )(n_tab, cnt_tab, base_tab, fill_tab, x, metat)


def _expert_kernel(tile_e, tile_valid, xs_ref, wg_ref, wu_ref, wd_ref, ys_ref, wgb_ref, wub_ref, wdb_ref):
    i = pl.program_id(0)
    valid = tile_valid[i]
    new_expert = jnp.logical_or(i == 0, tile_e[i] != tile_e[jnp.maximum(i - 1, 0)])

    @pl.when(jnp.logical_and(valid > 0, new_expert))
    def _():
        wgb_ref[...] = wg_ref[...].astype(BF16)
        wub_ref[...] = wu_ref[...].astype(BF16)
        wdb_ref[...] = wd_ref[...].astype(BF16)

    @pl.when(valid > 0)
    def _():
        xb = xs_ref[:, 0:D_MODEL].astype(BF16)
        g = _dot(xb, wgb_ref[...])
        u = _dot(xb, wub_ref[...])
        hid = (g * jax.nn.sigmoid(g) * u).astype(BF16)
        ys_ref[...] = xs_ref[:, D_MODEL:D_MODEL + 1] * _dot(hid, wdb_ref[...])

    @pl.when(valid == 0)
    def _():
        ys_ref[...] = jnp.zeros_like(ys_ref)


def _expert_layer(tile_e, tile_valid, xs, wg, wu, wd):
    n_steps = tile_e.shape[0]
    return pl.pallas_call(
        _expert_kernel,
        out_shape=jax.ShapeDtypeStruct((xs.shape[0], D_MODEL), F32),
        grid_spec=pltpu.PrefetchScalarGridSpec(
            num_scalar_prefetch=2,
            grid=(n_steps,),
            in_specs=[
                pl.BlockSpec((ROW_TILE, PAYLOAD_W), lambda i, e, v: (i, 0)),
                pl.BlockSpec((None, D_MODEL, D_EXPERT), lambda i, e, v: (e[i], 0, 0)),
                pl.BlockSpec((None, D_MODEL, D_EXPERT), lambda i, e, v: (e[i], 0, 0)),
                pl.BlockSpec((None, D_EXPERT, D_MODEL), lambda i, e, v: (e[i], 0, 0)),
            ],
            out_specs=pl.BlockSpec((ROW_TILE, D_MODEL), lambda i, e, v: (i, 0)),
            scratch_shapes=[
                pltpu.VMEM((D_MODEL, D_EXPERT), BF16),
                pltpu.VMEM((D_MODEL, D_EXPERT), BF16),
                pltpu.VMEM((D_EXPERT, D_MODEL), BF16),
            ],
        ),
        compiler_params=pltpu.CompilerParams(dimension_semantics=("arbitrary",), vmem_limit_bytes=VMEM_LIMIT),
        name="experts",
    )(tile_e, tile_valid, xs, wg, wu, wd)


def _combine_kernel(n_tab, cnt_tab, base_tab, x_ref, metat_ref, ys_hbm, l2w_ref, l2b_ref, o_ref, yc_ref, sem):
    i = pl.program_id(0)
    n_steps = pl.num_programs(0)
    slot = lax.rem(i, 2)

    def copies(tile, slot_, act):
        def make_copy(buf_row, ys_row, rows):
            return pltpu.make_async_copy(ys_hbm.at[pl.ds(ys_row, rows)],
                                         yc_ref.at[slot_, pl.ds(buf_row, rows)], sem.at[slot_])
        _for_each_run(n_tab, cnt_tab, base_tab, tile, make_copy, act)

    @pl.when(i == 0)
    def _():
        yc_ref[...] = jnp.zeros_like(yc_ref)
        copies(0, 0, lambda cp: cp.start())

    @pl.when(i + 1 < n_steps)
    def _():
        copies(i + 1, 1 - slot, lambda cp: cp.start())

    copies(i, slot, lambda cp: cp.wait())

    mt = metat_ref[...]
    t_r = lax.broadcasted_iota(jnp.int32, (MOE_TM, MOE_TM), 0)
    t_c = lax.broadcasted_iota(jnp.int32, (MOE_TM, MOE_TM), 1)
    as_col = lambda row: jnp.sum(jnp.where(t_r == t_c, row, 0.0), axis=-1, keepdims=True)
    col_f = lax.broadcasted_iota(jnp.int32, (MOE_TM, COMPACT_ROWS), 1).astype(F32)
    onehot = jnp.where(col_f == as_col(mt[0:1, :]), 1.0,
                       jnp.where(col_f == as_col(mt[1:2, :]), 1.0, 0.0)).astype(BF16)
    y = _dot(onehot, yc_ref[slot].astype(BF16))
    o_ref[...] = _layer_norm(DEEPNORM_ALPHA * x_ref[...] + y, l2w_ref[...], l2b_ref[...], LN_EPS)


def _combine_layer(n_tab, cnt_tab, base_tab, x, metat, ys, l2w, l2b):
    batch, seq, _ = x.shape
    return pl.pallas_call(
        _combine_kernel,
        out_shape=jax.ShapeDtypeStruct(x.shape, F32),
        grid_spec=pltpu.PrefetchScalarGridSpec(
            num_scalar_prefetch=3,
            grid=(batch * seq // MOE_TM,),
            in_specs=[
                _token_tile_spec(seq),
                pl.BlockSpec((None, SUBLANES, MOE_TM), lambda i, *_: (i, 0, 0)),
                pl.BlockSpec(memory_space=pl.ANY),
                pl.BlockSpec((1, D_MODEL), lambda i, *_: (0, 0)),
                pl.BlockSpec((1, D_MODEL), lambda i, *_: (0, 0)),
            ],
            out_specs=_token_tile_spec(seq),
            scratch_shapes=[pltpu.VMEM((2, COMPACT_ROWS, D_MODEL), F32), pltpu.SemaphoreType.DMA((2,))],
        ),
        compiler_params=pltpu.CompilerParams(dimension_semantics=("arbitrary",), vmem_limit_bytes=VMEM_LIMIT),
        name="combine",
    )(n_tab, cnt_tab, base_tab, x, metat, ys, l2w, l2b)


def _expert_plan(counts, n_steps):
    tiles_e = (counts + (ROW_TILE - 1)) // ROW_TILE
    cum = jnp.cumsum(tiles_e)
    first_tile = cum - tiles_e
    total = cum[-1]
    step = jnp.arange(n_steps, dtype=jnp.int32)
    owner = (step[:, None] >= cum[None, :]).sum(axis=1).astype(jnp.int32)
    is_owner = owner[:, None] == jnp.arange(N_EXPERTS, dtype=jnp.int32)[None, :]
    local = step - jnp.where(is_owner, first_tile[None, :], 0).sum(axis=1)
    rows_left = jnp.where(is_owner, counts[None, :], 0).sum(axis=1) - local * ROW_TILE
    valid = jnp.where(step < total, jnp.clip(rows_left, 0, ROW_TILE), 0).astype(jnp.int32)
    tile_e = jnp.minimum(owner, N_EXPERTS - 1)
    base = (first_tile * ROW_TILE).astype(jnp.int32)
    fill = jnp.stack([base + counts, tiles_e * ROW_TILE - counts], axis=1).reshape(-1)
    fill_tab = jnp.concatenate([fill, (total * ROW_TILE)[None]]).astype(jnp.int32)
    return base, tile_e, valid, fill_tab


def _moe_layer(x, metat, tile_n, tile_cnt, wg, wu, wd, l2w, l2b):
    n_tok = x.shape[0] * x.shape[1]
    n_tiles = n_tok // MOE_TM
    n_steps = (2 * n_tok + N_EXPERTS * (SUBLANES - 1) * n_tiles) // ROW_TILE + N_EXPERTS
    n_tab = tile_n[:, :, 0].reshape(-1)
    cnt_tab = tile_cnt[:, :, 0].reshape(-1)
    counts = tile_cnt[-1, :, 0] + tile_n[-1, :, 0]
    base_tab, tile_e, tile_valid, fill_tab = _expert_plan(counts, n_steps)
    xs = _dispatch_layer(n_tab, cnt_tab, base_tab, fill_tab, x, metat, n_steps * ROW_TILE)
    ys = _expert_layer(tile_e, tile_valid, xs, wg, wu, wd)
    return _combine_layer(n_tab, cnt_tab, base_tab, x, metat, ys, l2w, l2b)


def kernel(x, w_in, b_in, mlstm_norm_w, gmlp_norm_w, gmlp_norm_b, gmlp_ws, gmlp_bs, conv_w, conv_b,
           conv_norm_w, conv_norm_b, w_out, ln1_w, ln1_b, router_w, router_b, w_gate, w_up, w_down,
           ln2_w, ln2_b):
    batch, seq, d = x.shape

    q_lo, k_lo, v_lo = 0, ML_WIDTH, 2 * ML_WIDTH
    gate_lo = 4 * ML_WIDTH
    rest_lo = gate_lo + 2 * ML_HEADS

    def row_part(t):
        return jnp.concatenate([t[..., q_lo:k_lo], t[..., v_lo:gate_lo], t[..., rest_lo:]], axis=-1)

    def col_part(t):
        return jnp.concatenate([t[..., k_lo:v_lo], t[..., gate_lo:rest_lo]], axis=-1)

    tril = jnp.tril(jnp.ones((CHUNK, CHUNK), gmlp_ws.dtype))
    rw_pad = jnp.pad(router_w.astype(F32), ((0, 0), (0, LANES - N_EXPERTS)))
    rw_head = rw_pad.astype(BF16)
    rw_split = jnp.concatenate([rw_head, (rw_pad - rw_head.astype(F32)).astype(BF16)], axis=1)
    rb_col = router_b.astype(F32).reshape(N_EXPERTS, 1)

    for l in range(DEPTH):
        w_row = row_part(w_in[l]).astype(BF16)
        b_row = row_part(b_in[l]).reshape(1, P_ROW)
        w_col = col_part(w_in[l]).T.astype(BF16)
        b_col = col_part(b_in[l]).reshape(P_COL, 1)
        ws_cat = jnp.transpose(gmlp_ws[l] * tril, (1, 0, 2)).reshape(CHUNK, G_HEADS * CHUNK).astype(BF16)
        bs_full = jnp.repeat(gmlp_bs[l].T, G_HEAD_DIM, axis=1)
        cw = jnp.pad(conv_w[l], ((0, 1), (0, 0)))
        x, metat, tile_n, tile_cnt = _mixer_layer(
            x, w_row, b_row, w_col, b_col, mlstm_norm_w[l].reshape(1, -1), gmlp_norm_w[l].reshape(1, -1),
            gmlp_norm_b[l].reshape(1, -1), ws_cat, bs_full, cw, conv_b[l].reshape(1, -1),
            conv_norm_w[l].reshape(1, -1), conv_norm_b[l].reshape(1, -1), w_out[l].astype(BF16),
            ln1_w[l].reshape(1, -1), ln1_b[l].reshape(1, -1), rw_split, rb_col)
        x = _moe_layer(x, metat, tile_n, tile_cnt, w_gate[l], w_up[l], w_down[l],
                       ln2_w[l].reshape(1, -1), ln2_b[l].reshape(1, -1))
    return x
```

```python
import functools

import jax
import jax.numpy as jnp
from jax import lax
from jax.experimental import pallas as pl
from jax.experimental.pallas import tpu as pltpu

D_MODEL = 1024
DEPTH = 4
ML_WIDTH = 512
ML_HEADS = 4
ML_HEAD_DIM = 128
CHUNK = 128
G_WIDTH = 256
G_HEADS = 4
G_HEAD_DIM = 64
C_WIDTH = 256
CONV_WIDTH = 31
N_EXPERTS = 16
N_GROUPS = 4
EXPERTS_PER_GROUP = 4
D_EXPERT = 512
DEEPNORM_ALPHA = (2.0 * DEPTH) ** 0.25
LN_EPS = 1e-5
K_SCALE = ML_HEAD_DIM ** -0.5

Q_LO, V_LO, O_LO, GU_LO, GV_LO, CA_LO, CB_LO = 0, 512, 1024, 1536, 1792, 2048, 2304
P_ROW = 2560
P_COL = ML_WIDTH + 2 * ML_HEADS
P_COL_PAD = ML_WIDTH + 16
IN_Q, IN_K, IN_VO, IN_GATES, IN_REST = (0, 512), (512, 1024), (1024, 2048), (2048, 2056), (2056, 3080)
P_IN = 3080
W_PREP_ROWS = 256

LANES = 128
SUBLANES = 8
MOE_TM = 256
ROW_TILE = 512
PAYLOAD_W = D_MODEL + LANES
COMPACT_ROWS = 2 * MOE_TM + LANES
CONV_HALO = 32
CONV_SPAN = CONV_HALO - SUBLANES
STREAMS = 2
VMEM_LIMIT = 58 * 1024 * 1024

F32 = jnp.float32
BF16 = jnp.bfloat16
NEG_INF = float("-inf")


def _layer_norm(x, w, b, eps):
    mu = jnp.mean(x, axis=-1, keepdims=True)
    xc = x - mu
    var = jnp.mean(xc * xc, axis=-1, keepdims=True)
    return xc * lax.rsqrt(var + eps) * w + b


def _gelu_tanh(x):
    return 0.5 * x * (1.0 + jnp.tanh(0.7978845608028654 * (x + 0.044715 * (x * x * x))))


def _log_sigmoid(x):
    return jnp.minimum(x, 0.0) - jnp.log1p(jnp.exp(-jnp.abs(x)))


def _dot(a, b):
    return jnp.dot(a, b, preferred_element_type=F32)


def _dot_nt(a, b):
    return lax.dot_general(a, b, (((1,), (1,)), ((), ())), preferred_element_type=F32)


def _router_logits(x_new, rw_ref):
    xh = x_new.astype(BF16)
    xl = (x_new - xh.astype(F32)).astype(BF16)
    head = _dot(xh, rw_ref[...])
    return head[:, 0:LANES] + head[:, LANES:2 * LANES] + _dot(xl, rw_ref[:, 0:LANES])


def _route_tile(logits, rb_ref, metat_ref, tile_n_ref, tile_cnt_ref, cnt_ref):
    tm = logits.shape[1]
    s_all = jax.nn.sigmoid(logits)
    sel_all = s_all + rb_ref[...]
    rows_of = lambda v: [v[k:k + 1, :] for k in range(N_EXPERTS)]
    s, sel = rows_of(s_all), rows_of(sel_all)

    best = None
    for g in range(N_GROUPS):
        r = sel[EXPERTS_PER_GROUP * g:EXPERTS_PER_GROUP * (g + 1)]
        hi01, lo01 = jnp.maximum(r[0], r[1]), jnp.minimum(r[0], r[1])
        hi23, lo23 = jnp.maximum(r[2], r[3]), jnp.minimum(r[2], r[3])
        score = jnp.maximum(hi01, hi23) + jnp.maximum(jnp.minimum(hi01, hi23), jnp.maximum(lo01, lo23))
        if best is None:
            best, gidx = score, jnp.zeros(score.shape, jnp.int32)
        else:
            better = score > best
            gidx = jnp.where(better, g, gidx)
            best = jnp.where(better, score, best)

    def of_group(rows):
        out = []
        for j in range(EXPERTS_PER_GROUP):
            v = rows[j]
            for g in range(1, N_GROUPS):
                v = jnp.where(gidx == g, rows[EXPERTS_PER_GROUP * g + j], v)
            out.append(v)
        return out

    def first_max(vals):
        best_v, best_j = vals[0], jnp.zeros(vals[0].shape, F32)
        for j in range(1, len(vals)):
            better = vals[j] > best_v
            best_j = jnp.where(better, float(j), best_j)
            best_v = jnp.where(better, vals[j], best_v)
        return best_j

    def take(vals, idx):
        v = vals[0]
        for j in range(1, len(vals)):
            v = jnp.where(idx == float(j), vals[j], v)
        return v

    cand, cand_s = of_group(sel), of_group(s)
    j1 = first_max(cand)
    j2 = first_max([jnp.where(j1 == float(j), NEG_INF, cand[j]) for j in range(EXPERTS_PER_GROUP)])
    g1, g2 = take(cand_s, j1), take(cand_s, j2)
    tot = g1 + g2
    first_of_group = gidx.astype(F32) * float(EXPERTS_PER_GROUP)
    e1, e2 = first_of_group + j1, first_of_group + j2

    e_f = lax.broadcasted_iota(jnp.int32, (N_EXPERTS, tm), 0).astype(F32)
    assign = jnp.where(e_f == e1, 1.0, jnp.where(e_f == e2, 1.0, 0.0)).astype(BF16)
    t_r = lax.broadcasted_iota(jnp.int32, (tm, tm), 0)
    t_c = lax.broadcasted_iota(jnp.int32, (tm, tm), 1)
    earlier = jnp.where(t_r < t_c, 1.0, 0.0).astype(BF16)
    rank = _dot(assign, earlier)
    n_b = _dot(assign, jnp.ones((tm, LANES), BF16))
    n_up_b = jnp.floor((n_b + (SUBLANES - 1.0)) * (1.0 / SUBLANES)) * SUBLANES
    x_r = lax.broadcasted_iota(jnp.int32, (N_EXPERTS, N_EXPERTS), 0)
    x_c = lax.broadcasted_iota(jnp.int32, (N_EXPERTS, N_EXPERTS), 1)
    lower = jnp.where(x_c < x_r, 1.0, 0.0).astype(BF16)
    off_b = _dot(lower, n_up_b.astype(BF16))
    pos = rows_of(jnp.concatenate([off_b] * (tm // LANES), axis=1) + rank)
    metat_ref[...] = jnp.concatenate(
        [take(pos, e1), take(pos, e2), g1 / tot, g2 / tot, jnp.zeros((SUBLANES - 4, tm), F32)], axis=0)

    cnt = cnt_ref[...]
    tile_n_ref[...] = n_up_b.astype(jnp.int32)
    tile_cnt_ref[...] = cnt.astype(jnp.int32)
    cnt_ref[...] = cnt + n_up_b


def _project(x_tile, w_row_ref, b_row_ref, w_col_ref, b_col_ref, proj_ref, colp_ref):
    xb = x_tile.astype(BF16)
    proj_ref[...] = _dot(xb, w_row_ref[...]) + b_row_ref[...]
    colp_ref[...] = _dot_nt(w_col_ref[...], xb) + b_col_ref[...]


def _mix_tile(x, proj_ref, colp_ref, mnw_ref, gnw_ref, gnb_ref, ws_ref, bs_ref, cw_ref, cb_ref, cnw_ref,
              cnb_ref, w_out_ref, l1w_ref, l1b_ref, hcat_ref, cstate_ref, mstate_ref, cbuf_ref, shift_ref, tm):
    n_chunks = tm // CHUNK
    gates = colp_ref[ML_WIDTH:P_COL, :]
    lane_in_chunk = lax.broadcasted_iota(jnp.int32, gates.shape, 1) % CHUNK
    logf_all = _log_sigmoid(gates)
    bcum = logf_all
    d = 1
    while d < CHUNK:
        bcum = bcum + jnp.where(lane_in_chunk >= d, pltpu.roll(bcum, d, 1), 0.0)
        d *= 2

    row_i = lax.broadcasted_iota(jnp.int32, (CHUNK, CHUNK), 0)
    col_i = lax.broadcasted_iota(jnp.int32, (CHUNK, CHUNK), 1)
    causal = col_i <= row_i
    diag = col_i == row_i
    ones_col = jnp.where(col_i == 0, 1.0, 0.0).astype(BF16)

    for sc in range(STREAMS * n_chunks):
        st, c = divmod(sc, n_chunks)
        r0 = st * tm + c * CHUNK
        rows = slice(r0, r0 + CHUNK)
        m_rows = slice(st * SUBLANES, st * SUBLANES + ML_HEADS)
        logi = gates[0:ML_HEADS, r0:r0 + CHUNK]
        logf = logf_all[ML_HEADS:2 * ML_HEADS, r0:r0 + CHUNK]
        b_row = bcum[ML_HEADS:2 * ML_HEADS, r0:r0 + CHUNK]
        g = jnp.sum(logf, axis=-1, keepdims=True)
        m_prev = mstate_ref[m_rows, 0:1]
        a_row = g - b_row + logi
        m_new = jnp.maximum(g + m_prev, jnp.max(a_row, axis=-1, keepdims=True))
        w_row = jnp.exp(a_row - m_new) * K_SCALE
        decay = jnp.exp(g + m_prev - m_new)
        mstate_ref[m_rows, :] = jnp.broadcast_to(m_new, (ML_HEADS, LANES))

        for h in range(ML_HEADS):
            hs = slice(h * ML_HEAD_DIM, (h + 1) * ML_HEAD_DIM)
            q = proj_ref[rows, Q_LO + h * ML_HEAD_DIM:Q_LO + (h + 1) * ML_HEAD_DIM].astype(BF16)
            v = proj_ref[rows, V_LO + h * ML_HEAD_DIM:V_LO + (h + 1) * ML_HEAD_DIM].astype(BF16)
            o_gate = proj_ref[rows, O_LO + h * ML_HEAD_DIM:O_LO + (h + 1) * ML_HEAD_DIM]
            kt = colp_ref[hs, r0:r0 + CHUNK]
            v_ext = jnp.concatenate([v, ones_col], axis=1)

            b_r = b_row[h:h + 1, :]
            b_c = jnp.sum(jnp.where(diag, b_r, 0.0), axis=-1, keepdims=True)
            log_d = jnp.where(causal, b_c - b_r + logi[h:h + 1, :], NEG_INF)
            mp = m_prev[h:h + 1, :]
            log_inter = b_c + mp
            m_row = jnp.maximum(log_inter, jnp.max(log_d, axis=-1, keepdims=True))
            s = _dot(q, kt.astype(BF16))
            p = s * (jnp.exp(log_d - m_row) * K_SCALE)
            w_inter = jnp.exp(log_inter - m_row)
            c_ext = cstate_ref[st * ML_HEADS + h]
            intra = _dot(p.astype(BF16), v_ext)
            inter = _dot(q, c_ext.astype(BF16))
            tot = intra + w_inter * inter
            den = jnp.maximum(jnp.abs(tot[:, ML_HEAD_DIM:ML_HEAD_DIM + 1]), jnp.exp(-m_row))
            hh = tot[:, 0:ML_HEAD_DIM] / den

            ktw = (kt * w_row[h:h + 1, :]).astype(BF16)
            cstate_ref[st * ML_HEADS + h] = decay[h:h + 1, :] * c_ext + _dot(ktw, v_ext)

            mu = jnp.mean(hh, axis=-1, keepdims=True)
            hc = hh - mu
            var = jnp.mean(hc * hc, axis=-1, keepdims=True)
            hn = hc * lax.rsqrt(var + 1e-6) * mnw_ref[:, hs]
            hcat_ref[rows, hs] = (jax.nn.sigmoid(o_gate) * hn).astype(BF16)

        u = _gelu_tanh(proj_ref[rows, GU_LO:GU_LO + G_WIDTH])
        z = _layer_norm(_gelu_tanh(proj_ref[rows, GV_LO:GV_LO + G_WIDTH]), gnw_ref[...], gnb_ref[...], LN_EPS)
        lane_head = lax.broadcasted_iota(jnp.int32, (CHUNK, G_WIDTH), 1) // G_HEAD_DIM
        z_bd = jnp.concatenate(
            [jnp.where(lane_head == h, z, 0.0).astype(BF16) for h in range(G_HEADS)], axis=0)
        zs = _dot(ws_ref[...], z_bd) + bs_ref[...]
        hcat_ref[rows, ML_WIDTH:ML_WIDTH + G_WIDTH] = (u * zs).astype(BF16)

        ca = proj_ref[rows, CA_LO:CA_LO + C_WIDTH]
        cb = proj_ref[rows, CB_LO:CB_LO + C_WIDTH]
        cbuf_ref[st, CONV_HALO + c * CHUNK:CONV_HALO + (c + 1) * CHUNK, :] = ca * jax.nn.sigmoid(cb)

    first_tap = CONV_HALO - (CONV_WIDTH - 1)
    for st in range(STREAMS):
        for sh in range(1, SUBLANES):
            shift_ref[st, sh - 1] = cbuf_ref[st, sh:sh + tm + CONV_SPAN, :]
        for c in range(n_chunks):
            acc = jnp.zeros((CHUNK, C_WIDTH), F32) + cb_ref[...]
            for k in range(CONV_WIDTH):
                whole, sh = divmod(first_tap + k, SUBLANES)
                lo = c * CHUNK + whole * SUBLANES
                tap = cbuf_ref[st, lo:lo + CHUNK, :] if sh == 0 else shift_ref[st, sh - 1, lo:lo + CHUNK, :]
                acc = acc + cw_ref[k:k + 1, :] * tap
            cn = _layer_norm(acc, cnw_ref[...], cnb_ref[...], LN_EPS)
            r0 = st * tm + c * CHUNK
            hcat_ref[r0:r0 + CHUNK, ML_WIDTH + G_WIDTH:D_MODEL] = (cn * jax.nn.sigmoid(cn)).astype(BF16)
        cbuf_ref[st, 0:CONV_HALO, :] = cbuf_ref[st, tm:tm + CONV_HALO, :]

    y = _dot(hcat_ref[...], w_out_ref[...])
    x_new = _layer_norm(DEEPNORM_ALPHA * x + y, l1w_ref[...], l1b_ref[...], LN_EPS)
    return x_new


def _prepare_in_proj(w_in_ref, w_row_ref, w_col_ref):
    for r0 in range(0, D_MODEL, W_PREP_ROWS):
        rs = slice(r0, r0 + W_PREP_ROWS)
        w_row_ref[rs, Q_LO:V_LO] = w_in_ref[rs, IN_Q[0]:IN_Q[1]].astype(BF16)
        w_row_ref[rs, V_LO:GU_LO] = w_in_ref[rs, IN_VO[0]:IN_VO[1]].astype(BF16)
        w_row_ref[rs, GU_LO:P_ROW] = w_in_ref[rs, IN_GATES[0]:P_IN][:, IN_REST[0] - IN_GATES[0]:].astype(BF16)
        w_col_ref[0:ML_WIDTH, rs] = w_in_ref[rs, IN_K[0]:IN_K[1]].T.astype(BF16)
        gates_t = w_in_ref[rs, IN_GATES[0]:IN_GATES[0] + LANES].T
        keep = lax.broadcasted_iota(jnp.int32, (P_COL_PAD - ML_WIDTH, W_PREP_ROWS), 0) < 2 * ML_HEADS
        w_col_ref[ML_WIDTH:P_COL_PAD, rs] = jnp.where(keep, gates_t[0:P_COL_PAD - ML_WIDTH, :], 0.0).astype(BF16)


def _mixer_kernel(x_ref, w_in_ref, b_row_ref, b_col_ref, mnw_ref, gnw_ref, gnb_ref,
                  ws_ref, bs_ref, cw_ref, cb_ref, cnw_ref, cnb_ref, w_out_ref, l1w_ref, l1b_ref, rw_ref, rb_ref,
                  o_ref, metat_ref, tile_n_ref, tile_cnt_ref,
                  w_row_ref, w_col_ref, proj_ref, colp_ref, hcat_ref, cstate_ref, mstate_ref, cbuf_ref, shift_ref,
                  cnt_ref, *, tm, steps_per_seq):
    i = pl.program_id(0)

    @pl.when(i == 0)
    def _():
        cnt_ref[...] = jnp.zeros_like(cnt_ref)
        _prepare_in_proj(w_in_ref, w_row_ref, w_col_ref)

    @pl.when(i % steps_per_seq == 0)
    def _():
        cstate_ref[...] = jnp.zeros_like(cstate_ref)
        mstate_ref[...] = jnp.zeros_like(mstate_ref)
        cbuf_ref[:, 0:CONV_HALO, :] = jnp.zeros((STREAMS, CONV_HALO, C_WIDTH), F32)

    x = x_ref[...].reshape(STREAMS * tm, D_MODEL)
    _project(x, w_row_ref, b_row_ref, w_col_ref, b_col_ref, proj_ref, colp_ref)
    x_new = _mix_tile(x, proj_ref, colp_ref, mnw_ref, gnw_ref, gnb_ref, ws_ref, bs_ref, cw_ref, cb_ref, cnw_ref,
                      cnb_ref, w_out_ref, l1w_ref, l1b_ref, hcat_ref, cstate_ref, mstate_ref, cbuf_ref, shift_ref,
                      tm)
    o_ref[...] = x_new.reshape(STREAMS, tm, D_MODEL)
    logits = _router_logits(x_new, rw_ref)
    for st in range(STREAMS):
        _route_tile(logits[st * tm:(st + 1) * tm, :].T[0:N_EXPERTS, :], rb_ref, metat_ref.at[st], tile_n_ref.at[st],
                    tile_cnt_ref.at[st], cnt_ref)


def _full(shape):
    nd = len(shape)
    return pl.BlockSpec(shape, lambda i, _nd=nd: (0,) * _nd, pipeline_mode=pl.Buffered(1))


def _mixer_layer(x, w_in, b_row, b_col, mnw, gnw, gnb, ws_cat, bs_full, cw, cb, cnw, cnb,
                 w_out, l1w, l1b, rw_split, rb_col):
    batch, seq, _ = x.shape
    tm = MOE_TM
    steps_per_seq = seq // tm
    n_steps = batch // STREAMS * steps_per_seq
    n_tiles = batch * seq // tm
    rows = STREAMS * tm
    kernel = functools.partial(_mixer_kernel, tm=tm, steps_per_seq=steps_per_seq)
    weights = (w_in, b_row, b_col, mnw, gnw, gnb, ws_cat, bs_full, cw, cb, cnw, cnb, w_out, l1w, l1b,
               rw_split, rb_col)
    tile_spec = pl.BlockSpec((STREAMS, N_EXPERTS, LANES), lambda i: (i, 0, 0))
    step_spec = pl.BlockSpec((STREAMS, tm, D_MODEL), lambda i: (i // steps_per_seq, i % steps_per_seq, 0))
    return pl.pallas_call(
        kernel,
        out_shape=(
            jax.ShapeDtypeStruct(x.shape, F32),
            jax.ShapeDtypeStruct((n_tiles, SUBLANES, tm), F32),
            jax.ShapeDtypeStruct((n_tiles, N_EXPERTS, LANES), jnp.int32),
            jax.ShapeDtypeStruct((n_tiles, N_EXPERTS, LANES), jnp.int32),
        ),
        grid=(n_steps,),
        in_specs=[step_spec] + [_full(w.shape) for w in weights],
        out_specs=(step_spec, pl.BlockSpec((STREAMS, SUBLANES, tm), lambda i: (i, 0, 0)), tile_spec, tile_spec),
        scratch_shapes=[
            pltpu.VMEM((D_MODEL, P_ROW), BF16),
            pltpu.VMEM((P_COL_PAD, D_MODEL), BF16),
            pltpu.VMEM((rows, P_ROW), F32),
            pltpu.VMEM((P_COL_PAD, rows), F32),
            pltpu.VMEM((rows, D_MODEL), BF16),
            pltpu.VMEM((STREAMS * ML_HEADS, ML_HEAD_DIM, 2 * ML_HEAD_DIM), F32),
            pltpu.VMEM((STREAMS * SUBLANES, LANES), F32),
            pltpu.VMEM((STREAMS, tm + CONV_HALO, C_WIDTH), F32),
            pltpu.VMEM((STREAMS, SUBLANES - 1, tm + CONV_SPAN, C_WIDTH), F32),
            pltpu.VMEM((N_EXPERTS, LANES), F32),
        ],
        compiler_params=pltpu.CompilerParams(
            dimension_semantics=("arbitrary",), vmem_limit_bytes=VMEM_LIMIT),
        name="mixer",
    )(x, *weights)


def _pow2_pieces(n, largest, act):
    piece = largest
    while piece >= SUBLANES:
        start = jnp.bitwise_and(n, -2 * piece)

        @pl.when(jnp.bitwise_and(n, piece) != 0)
        def _(piece=piece, start=start):
            act(start, piece)
        piece //= 2


def _for_each_run(n_tab, cnt_tab, base_tab, tile, make_copy, act):
    def body(e, off):
        n = n_tab[tile * N_EXPERTS + e]
        base = base_tab[e] + cnt_tab[tile * N_EXPERTS + e]
        _pow2_pieces(n, MOE_TM, lambda start, size: act(make_copy(
            pl.multiple_of(off + start, SUBLANES), pl.multiple_of(base + start, SUBLANES), size)))
        return off + n

    lax.fori_loop(0, N_EXPERTS, body, jnp.int32(0))


def _dispatch_kernel(n_tab, cnt_tab, base_tab, fill_tab, x_ref, metat_ref, xs_hbm, comp_ref, zero_ref, sem,
                     *, n_rows):
    i = pl.program_id(0)
    last = pl.num_programs(0) - 1
    slot = lax.rem(i, 2)

    def copies(step, slot_, act):
        for k in range(STREAMS):
            def make_copy(src_row, dst_row, rows, k=k):
                return pltpu.make_async_copy(comp_ref.at[slot_, k, pl.ds(src_row, rows)],
                                             xs_hbm.at[pl.ds(dst_row, rows)], sem.at[slot_])
            _for_each_run(n_tab, cnt_tab, base_tab, step * STREAMS + k, make_copy, act)

    def zero_fill(act):
        def zero_copy(dst_row, rows):
            return pltpu.make_async_copy(zero_ref.at[pl.ds(0, rows)],
                                         xs_hbm.at[pl.ds(pl.multiple_of(dst_row, SUBLANES), rows)], sem.at[2])

        def per_expert(e, carry):
            first = fill_tab[2 * e]
            _pow2_pieces(fill_tab[2 * e + 1], ROW_TILE // 2, lambda start, size: act(zero_copy(first + start, size)))
            return carry

        lax.fori_loop(0, N_EXPERTS, per_expert, jnp.int32(0))

        def per_half_tile(k, carry):
            act(zero_copy(fill_tab[2 * N_EXPERTS] + k * (ROW_TILE // 2), ROW_TILE // 2))
            return carry

        lax.fori_loop(0, (n_rows - fill_tab[2 * N_EXPERTS]) // (ROW_TILE // 2), per_half_tile, jnp.int32(0))

    @pl.when(i == 0)
    def _():
        zero_ref[...] = jnp.zeros_like(zero_ref)
        zero_fill(lambda cp: cp.start())

    @pl.when(i >= 2)
    def _():
        copies(i - 2, slot, lambda cp: cp.wait())

    row_f = lax.broadcasted_iota(jnp.int32, (COMPACT_ROWS, MOE_TM), 0).astype(F32)
    for k in range(STREAMS):
        mt = metat_ref[k]
        hit1 = row_f == mt[0:1, :]
        hit2 = row_f == mt[1:2, :]
        onehot = jnp.where(hit1, 1.0, jnp.where(hit2, 1.0, 0.0)).astype(BF16)
        comp_ref[slot, k, :, 0:D_MODEL] = _dot(onehot, x_ref[k].astype(BF16))
        gate = jnp.sum(jnp.where(hit1, mt[2:3, :], jnp.where(hit2, mt[3:4, :], 0.0)), axis=-1, keepdims=True)
        comp_ref[slot, k, :, D_MODEL:PAYLOAD_W] = jnp.broadcast_to(gate, (COMPACT_ROWS, LANES))
    copies(i, slot, lambda cp: cp.start())

    @pl.when(i == last)
    def _():
        @pl.when(i >= 1)
        def _():
            copies(i - 1, 1 - slot, lambda cp: cp.wait())
        copies(i, slot, lambda cp: cp.wait())
        zero_fill(lambda cp: cp.wait())


def _token_step_spec(seq):
    tiles_per_seq = seq // MOE_TM
    return pl.BlockSpec((STREAMS, MOE_TM, D_MODEL), lambda i, *_: (i // tiles_per_seq, i % tiles_per_seq, 0))


_ROUTE_STEP_SPEC = pl.BlockSpec((STREAMS, SUBLANES, MOE_TM), lambda i, *_: (i, 0, 0))


def _dispatch_layer(n_tab, cnt_tab, base_tab, fill_tab, x, metat, n_rows):
    batch, seq, _ = x.shape
    return pl.pallas_call(
        functools.partial(_dispatch_kernel, n_rows=n_rows),
        out_shape=jax.ShapeDtypeStruct((n_rows, PAYLOAD_W), F32),
        grid_spec=pltpu.PrefetchScalarGridSpec(
            num_scalar_prefetch=4,
            grid=(batch * seq // (STREAMS * MOE_TM),),
            in_specs=[_token_step_spec(seq), _ROUTE_STEP_SPEC],
            out_specs=pl.BlockSpec(memory_space=pl.ANY),
            scratch_shapes=[
                pltpu.VMEM((2, STREAMS, COMPACT_ROWS, PAYLOAD_W), F32),
                pltpu.VMEM((ROW_TILE // 2, PAYLOAD_W), F32),
                pltpu.SemaphoreType.DMA((3,)),
            ],
        ),
        compiler_params=pltpu.CompilerParams(dimension_semantics=("arbitrary",), vmem_limit_bytes=VMEM_LIMIT),
        name="dispatch",
    )(n_tab, cnt_tab, base_tab, fill_tab, x, metat)


def _expert_kernel(tile_e, tile_valid, xs_ref, wg_ref, wu_ref, wd_ref, ys_ref, wgb_ref, wub_ref, wdb_ref):
    i = pl.program_id(0)
    valid = tile_valid[i]
    new_expert = jnp.logical_or(i == 0, tile_e[i] != tile_e[jnp.maximum(i - 1, 0)])

    @pl.when(jnp.logical_and(valid > 0, new_expert))
    def _():
        wgb_ref[...] = wg_ref[...].astype(BF16)
        wub_ref[...] = wu_ref[...].astype(BF16)
        wdb_ref[...] = wd_ref[...].astype(BF16)

    @pl.when(valid > 0)
    def _():
        half = ROW_TILE // 2
        rows = [slice(0, half), slice(half, ROW_TILE)]
        xb = [xs_ref[r, 0:D_MODEL].astype(BF16) for r in rows]
        g = [_dot(xb[h], wgb_ref[...]) for h in range(2)]
        u = [_dot(xb[h], wub_ref[...]) for h in range(2)]
        for h in range(2):
            hid = (g[h] * jax.nn.sigmoid(g[h]) * u[h]).astype(BF16)
            ys_ref[rows[h], :] = xs_ref[rows[h], D_MODEL:D_MODEL + 1] * _dot(hid, wdb_ref[...])

    @pl.when(valid == 0)
    def _():
        ys_ref[...] = jnp.zeros_like(ys_ref)


def _expert_layer(tile_e, tile_valid, xs, wg, wu, wd):
    n_steps = tile_e.shape[0]
    return pl.pallas_call(
        _expert_kernel,
        out_shape=jax.ShapeDtypeStruct((xs.shape[0], D_MODEL), F32),
        grid_spec=pltpu.PrefetchScalarGridSpec(
            num_scalar_prefetch=2,
            grid=(n_steps,),
            in_specs=[
                pl.BlockSpec((ROW_TILE, PAYLOAD_W), lambda i, e, v: (i, 0)),
                pl.BlockSpec((None, D_MODEL, D_EXPERT), lambda i, e, v: (e[i], 0, 0)),
                pl.BlockSpec((None, D_MODEL, D_EXPERT), lambda i, e, v: (e[i], 0, 0)),
                pl.BlockSpec((None, D_EXPERT, D_MODEL), lambda i, e, v: (e[i], 0, 0)),
            ],
            out_specs=pl.BlockSpec((ROW_TILE, D_MODEL), lambda i, e, v: (i, 0)),
            scratch_shapes=[
                pltpu.VMEM((D_MODEL, D_EXPERT), BF16),
                pltpu.VMEM((D_MODEL, D_EXPERT), BF16),
                pltpu.VMEM((D_EXPERT, D_MODEL), BF16),
            ],
        ),
        compiler_params=pltpu.CompilerParams(dimension_semantics=("arbitrary",), vmem_limit_bytes=VMEM_LIMIT),
        name="experts",
    )(tile_e, tile_valid, xs, wg, wu, wd)


def _combine_kernel(n_tab, cnt_tab, base_tab, x_ref, metat_ref, ys_hbm, l2w_ref, l2b_ref, o_ref, yc_ref, sem):
    i = pl.program_id(0)
    n_steps = pl.num_programs(0)
    slot = lax.rem(i, 2)

    def copies(step, slot_, act):
        for k in range(STREAMS):
            def make_copy(buf_row, ys_row, rows, k=k):
                return pltpu.make_async_copy(ys_hbm.at[pl.ds(ys_row, rows)],
                                             yc_ref.at[slot_, k, pl.ds(buf_row, rows)], sem.at[slot_])
            _for_each_run(n_tab, cnt_tab, base_tab, step * STREAMS + k, make_copy, act)

    @pl.when(i == 0)
    def _():
        yc_ref[...] = jnp.zeros_like(yc_ref)
        copies(0, 0, lambda cp: cp.start())

    @pl.when(i + 1 < n_steps)
    def _():
        copies(i + 1, 1 - slot, lambda cp: cp.start())

    copies(i, slot, lambda cp: cp.wait())

    t_r = lax.broadcasted_iota(jnp.int32, (MOE_TM, MOE_TM), 0)
    t_c = lax.broadcasted_iota(jnp.int32, (MOE_TM, MOE_TM), 1)
    as_col = lambda row: jnp.sum(jnp.where(t_r == t_c, row, 0.0), axis=-1, keepdims=True)
    col_f = lax.broadcasted_iota(jnp.int32, (MOE_TM, COMPACT_ROWS), 1).astype(F32)
    for k in range(STREAMS):
        mt = metat_ref[k]
        onehot = jnp.where(col_f == as_col(mt[0:1, :]), 1.0,
                           jnp.where(col_f == as_col(mt[1:2, :]), 1.0, 0.0)).astype(BF16)
        y = _dot(onehot, yc_ref[slot, k].astype(BF16))
        o_ref[k] = _layer_norm(DEEPNORM_ALPHA * x_ref[k] + y, l2w_ref[...], l2b_ref[...], LN_EPS)


def _combine_layer(n_tab, cnt_tab, base_tab, x, metat, ys, l2w, l2b):
    batch, seq, _ = x.shape
    return pl.pallas_call(
        _combine_kernel,
        out_shape=jax.ShapeDtypeStruct(x.shape, F32),
        grid_spec=pltpu.PrefetchScalarGridSpec(
            num_scalar_prefetch=3,
            grid=(batch * seq // (STREAMS * MOE_TM),),
            in_specs=[
                _token_step_spec(seq),
                _ROUTE_STEP_SPEC,
                pl.BlockSpec(memory_space=pl.ANY),
                pl.BlockSpec((1, D_MODEL), lambda i, *_: (0, 0)),
                pl.BlockSpec((1, D_MODEL), lambda i, *_: (0, 0)),
            ],
            out_specs=_token_step_spec(seq),
            scratch_shapes=[pltpu.VMEM((2, STREAMS, COMPACT_ROWS, D_MODEL), F32), pltpu.SemaphoreType.DMA((2,))],
        ),
        compiler_params=pltpu.CompilerParams(dimension_semantics=("arbitrary",), vmem_limit_bytes=VMEM_LIMIT),
        name="combine",
    )(n_tab, cnt_tab, base_tab, x, metat, ys, l2w, l2b)


def _expert_plan(counts, n_steps):
    tiles_e = (counts + (ROW_TILE - 1)) // ROW_TILE
    cum = jnp.cumsum(tiles_e)
    first_tile = cum - tiles_e
    total = cum[-1]
    step = jnp.arange(n_steps, dtype=jnp.int32)
    owner = (step[:, None] >= cum[None, :]).sum(axis=1).astype(jnp.int32)
    is_owner = owner[:, None] == jnp.arange(N_EXPERTS, dtype=jnp.int32)[None, :]
    local = step - jnp.where(is_owner, first_tile[None, :], 0).sum(axis=1)
    rows_left = jnp.where(is_owner, counts[None, :], 0).sum(axis=1) - local * ROW_TILE
    valid = jnp.where(step < total, jnp.clip(rows_left, 0, ROW_TILE), 0).astype(jnp.int32)
    tile_e = jnp.minimum(owner, N_EXPERTS - 1)
    base = (first_tile * ROW_TILE).astype(jnp.int32)
    fill = jnp.stack([base + counts, tiles_e * ROW_TILE - counts], axis=1).reshape(-1)
    fill_tab = jnp.concatenate([fill, (total * ROW_TILE)[None]]).astype(jnp.int32)
    return base, tile_e, valid, fill_tab


def _moe_layer(x, metat, tile_n, tile_cnt, wg, wu, wd, l2w, l2b):
    n_tok = x.shape[0] * x.shape[1]
    n_tiles = n_tok // MOE_TM
    n_steps = (2 * n_tok + N_EXPERTS * (SUBLANES - 1) * n_tiles) // ROW_TILE + N_EXPERTS
    n_tab = tile_n[:, :, 0].reshape(-1)
    cnt_tab = tile_cnt[:, :, 0].reshape(-1)
    counts = tile_cnt[-1, :, 0] + tile_n[-1, :, 0]
    base_tab, tile_e, tile_valid, fill_tab = _expert_plan(counts, n_steps)
    xs = _dispatch_layer(n_tab, cnt_tab, base_tab, fill_tab, x, metat, n_steps * ROW_TILE)
    ys = _expert_layer(tile_e, tile_valid, xs, wg, wu, wd)
    return _combine_layer(n_tab, cnt_tab, base_tab, x, metat, ys, l2w, l2b)


def kernel(x, w_in, b_in, mlstm_norm_w, gmlp_norm_w, gmlp_norm_b, gmlp_ws, gmlp_bs, conv_w, conv_b,
           conv_norm_w, conv_norm_b, w_out, ln1_w, ln1_b, router_w, router_b, w_gate, w_up, w_down,
           ln2_w, ln2_b):
    batch, seq, d = x.shape

    q_lo, k_lo, v_lo = 0, ML_WIDTH, 2 * ML_WIDTH
    gate_lo = 4 * ML_WIDTH
    rest_lo = gate_lo + 2 * ML_HEADS

    def row_part(t):
        return jnp.concatenate([t[..., q_lo:k_lo], t[..., v_lo:gate_lo], t[..., rest_lo:]], axis=-1)

    def col_part(t):
        return jnp.concatenate([t[..., k_lo:v_lo], t[..., gate_lo:rest_lo]], axis=-1)

    tril = jnp.tril(jnp.ones((CHUNK, CHUNK), gmlp_ws.dtype))
    rw_pad = jnp.pad(router_w.astype(F32), ((0, 0), (0, LANES - N_EXPERTS)))
    rw_head = rw_pad.astype(BF16)
    rw_split = jnp.concatenate([rw_head, (rw_pad - rw_head.astype(F32)).astype(BF16)], axis=1)
    rb_col = router_b.astype(F32).reshape(N_EXPERTS, 1)

    for l in range(DEPTH):
        b_row = row_part(b_in[l]).reshape(1, P_ROW)
        b_col = jnp.pad(col_part(b_in[l]), (0, P_COL_PAD - P_COL)).reshape(P_COL_PAD, 1)
        ws_cat = jnp.transpose(gmlp_ws[l] * tril, (1, 0, 2)).reshape(CHUNK, G_HEADS * CHUNK).astype(BF16)
        bs_full = jnp.repeat(gmlp_bs[l].T, G_HEAD_DIM, axis=1)
        cw = jnp.pad(conv_w[l], ((0, 1), (0, 0)))
        x, metat, tile_n, tile_cnt = _mixer_layer(
            x, w_in[l], b_row, b_col, mlstm_norm_w[l].reshape(1, -1), gmlp_norm_w[l].reshape(1, -1),
            gmlp_norm_b[l].reshape(1, -1), ws_cat, bs_full, cw, conv_b[l].reshape(1, -1),
            conv_norm_w[l].reshape(1, -1), conv_norm_b[l].reshape(1, -1), w_out[l].astype(BF16),
            ln1_w[l].reshape(1, -1), ln1_b[l].reshape(1, -1), rw_split, rb_col)
        x = _moe_layer(x, metat, tile_n, tile_cnt, w_gate[l], w_up[l], w_down[l],
                       ln2_w[l].reshape(1, -1), ln2_b[l].reshape(1, -1))
    return x
```

```python
import functools

import jax
import jax.numpy as jnp
from jax import lax
from jax.experimental import pallas as pl
from jax.experimental.pallas import tpu as pltpu

D_MODEL = 1024
DEPTH = 4
ML_WIDTH = 512
ML_HEADS = 4
ML_HEAD_DIM = 128
CHUNK = 128
G_WIDTH = 256
G_HEADS = 4
G_HEAD_DIM = 64
C_WIDTH = 256
CONV_WIDTH = 31
N_EXPERTS = 16
N_GROUPS = 4
EXPERTS_PER_GROUP = 4
D_EXPERT = 512
DEEPNORM_ALPHA = (2.0 * DEPTH) ** 0.25
LN_EPS = 1e-5
K_SCALE = ML_HEAD_DIM ** -0.5

Q_LO, V_LO, O_LO, GU_LO, GV_LO, CA_LO, CB_LO = 0, 512, 1024, 1536, 1792, 2048, 2304
P_ROW = 2560
P_COL = ML_WIDTH + 2 * ML_HEADS
P_COL_PAD = ML_WIDTH + 16
IN_Q, IN_K, IN_VO, IN_GATES, IN_REST = (0, 512), (512, 1024), (1024, 2048), (2048, 2056), (2056, 3080)
P_IN = 3080
W_PREP_ROWS = 256

LANES = 128
SUBLANES = 8
MOE_TM = 256
ROW_TILE = 512
PAYLOAD_W = D_MODEL + LANES
COMPACT_ROWS = 2 * MOE_TM + LANES
CONV_HALO = 32
CONV_SPAN = CONV_HALO - SUBLANES
STREAMS = 2
VMEM_LIMIT = 58 * 1024 * 1024

F32 = jnp.float32
BF16 = jnp.bfloat16
NEG_INF = float("-inf")


def _layer_norm(x, w, b, eps):
    mu = jnp.mean(x, axis=-1, keepdims=True)
    xc = x - mu
    var = jnp.mean(xc * xc, axis=-1, keepdims=True)
    return xc * lax.rsqrt(var + eps) * w + b


def _gelu_tanh(x):
    return 0.5 * x * (1.0 + jnp.tanh(0.7978845608028654 * (x + 0.044715 * (x * x * x))))


def _log_sigmoid(x):
    return jnp.minimum(x, 0.0) - jnp.log1p(jnp.exp(-jnp.abs(x)))


def _dot(a, b):
    return jnp.dot(a, b, preferred_element_type=F32)


def _dot_nt(a, b):
    return lax.dot_general(a, b, (((1,), (1,)), ((), ())), preferred_element_type=F32)


def _router_logits(x_new, rw_ref):
    xh = x_new.astype(BF16)
    xl = (x_new - xh.astype(F32)).astype(BF16)
    head = _dot(xh, rw_ref[...])
    return head[:, 0:LANES] + head[:, LANES:2 * LANES] + _dot(xl, rw_ref[:, 0:LANES])


def _route_tile(logits, rb_ref, metat_ref, tile_n_ref, tile_cnt_ref, cnt_ref):
    tm = logits.shape[1]
    s_all = jax.nn.sigmoid(logits)
    sel_all = s_all + rb_ref[...]
    rows_of = lambda v: [v[k:k + 1, :] for k in range(N_EXPERTS)]
    s, sel = rows_of(s_all), rows_of(sel_all)

    best = None
    for g in range(N_GROUPS):
        r = sel[EXPERTS_PER_GROUP * g:EXPERTS_PER_GROUP * (g + 1)]
        hi01, lo01 = jnp.maximum(r[0], r[1]), jnp.minimum(r[0], r[1])
        hi23, lo23 = jnp.maximum(r[2], r[3]), jnp.minimum(r[2], r[3])
        score = jnp.maximum(hi01, hi23) + jnp.maximum(jnp.minimum(hi01, hi23), jnp.maximum(lo01, lo23))
        if best is None:
            best, gidx = score, jnp.zeros(score.shape, jnp.int32)
        else:
            better = score > best
            gidx = jnp.where(better, g, gidx)
            best = jnp.where(better, score, best)

    def of_group(rows):
        out = []
        for j in range(EXPERTS_PER_GROUP):
            v = rows[j]
            for g in range(1, N_GROUPS):
                v = jnp.where(gidx == g, rows[EXPERTS_PER_GROUP * g + j], v)
            out.append(v)
        return out

    def first_max(vals):
        best_v, best_j = vals[0], jnp.zeros(vals[0].shape, F32)
        for j in range(1, len(vals)):
            better = vals[j] > best_v
            best_j = jnp.where(better, float(j), best_j)
            best_v = jnp.where(better, vals[j], best_v)
        return best_j

    def take(vals, idx):
        v = vals[0]
        for j in range(1, len(vals)):
            v = jnp.where(idx == float(j), vals[j], v)
        return v

    cand, cand_s = of_group(sel), of_group(s)
    j1 = first_max(cand)
    j2 = first_max([jnp.where(j1 == float(j), NEG_INF, cand[j]) for j in range(EXPERTS_PER_GROUP)])
    g1, g2 = take(cand_s, j1), take(cand_s, j2)
    tot = g1 + g2
    first_of_group = gidx.astype(F32) * float(EXPERTS_PER_GROUP)
    e1, e2 = first_of_group + j1, first_of_group + j2

    e_f = lax.broadcasted_iota(jnp.int32, (N_EXPERTS, tm), 0).astype(F32)
    assign = jnp.where(e_f == e1, 1.0, jnp.where(e_f == e2, 1.0, 0.0)).astype(BF16)
    t_r = lax.broadcasted_iota(jnp.int32, (tm, tm), 0)
    t_c = lax.broadcasted_iota(jnp.int32, (tm, tm), 1)
    earlier = jnp.where(t_r < t_c, 1.0, 0.0).astype(BF16)
    rank = _dot(assign, earlier)
    n_b = _dot(assign, jnp.ones((tm, LANES), BF16))
    n_up_b = jnp.floor((n_b + (SUBLANES - 1.0)) * (1.0 / SUBLANES)) * SUBLANES
    x_r = lax.broadcasted_iota(jnp.int32, (N_EXPERTS, N_EXPERTS), 0)
    x_c = lax.broadcasted_iota(jnp.int32, (N_EXPERTS, N_EXPERTS), 1)
    lower = jnp.where(x_c < x_r, 1.0, 0.0).astype(BF16)
    off_b = _dot(lower, n_up_b.astype(BF16))
    pos = rows_of(jnp.concatenate([off_b] * (tm // LANES), axis=1) + rank)
    metat_ref[...] = jnp.concatenate(
        [take(pos, e1), take(pos, e2), g1 / tot, g2 / tot, jnp.zeros((SUBLANES - 4, tm), F32)], axis=0)

    cnt = cnt_ref[...]
    tile_n_ref[...] = n_up_b.astype(jnp.int32)
    tile_cnt_ref[...] = cnt.astype(jnp.int32)
    cnt_ref[...] = cnt + n_up_b


def _project(x_tile, w_row_ref, b_row_ref, w_col_ref, b_col_ref, proj_ref, colp_ref):
    xb = x_tile.astype(BF16)
    proj_ref[...] = _dot(xb, w_row_ref[...]) + b_row_ref[...]
    colp_ref[...] = _dot_nt(w_col_ref[...], xb) + b_col_ref[...]


def _mix_tile(x, proj_ref, colp_ref, mnw_ref, gnw_ref, gnb_ref, ws_ref, bs_ref, cw_ref, cb_ref, cnw_ref,
              cnb_ref, w_out_ref, l1w_ref, l1b_ref, hcat_ref, cstate_ref, mstate_ref, cbuf_ref, shift_ref, tm):
    n_chunks = tm // CHUNK
    gates = colp_ref[ML_WIDTH:P_COL, :]
    lane_in_chunk = lax.broadcasted_iota(jnp.int32, gates.shape, 1) % CHUNK
    logf_all = _log_sigmoid(gates)
    bcum = logf_all
    d = 1
    while d < CHUNK:
        bcum = bcum + jnp.where(lane_in_chunk >= d, pltpu.roll(bcum, d, 1), 0.0)
        d *= 2

    row_i = lax.broadcasted_iota(jnp.int32, (CHUNK, CHUNK), 0)
    col_i = lax.broadcasted_iota(jnp.int32, (CHUNK, CHUNK), 1)
    causal = col_i <= row_i
    diag = col_i == row_i
    ones_col = jnp.where(col_i == 0, 1.0, 0.0).astype(BF16)

    for sc in range(STREAMS * n_chunks):
        st, c = divmod(sc, n_chunks)
        r0 = st * tm + c * CHUNK
        rows = slice(r0, r0 + CHUNK)
        m_rows = slice(st * SUBLANES, st * SUBLANES + ML_HEADS)
        logi = gates[0:ML_HEADS, r0:r0 + CHUNK]
        logf = logf_all[ML_HEADS:2 * ML_HEADS, r0:r0 + CHUNK]
        b_row = bcum[ML_HEADS:2 * ML_HEADS, r0:r0 + CHUNK]
        g = jnp.sum(logf, axis=-1, keepdims=True)
        m_prev = mstate_ref[m_rows, 0:1]
        a_row = g - b_row + logi
        m_new = jnp.maximum(g + m_prev, jnp.max(a_row, axis=-1, keepdims=True))
        w_row = jnp.exp(a_row - m_new) * K_SCALE
        decay = jnp.exp(g + m_prev - m_new)
        mstate_ref[m_rows, :] = jnp.broadcast_to(m_new, (ML_HEADS, LANES))

        for h in range(ML_HEADS):
            hs = slice(h * ML_HEAD_DIM, (h + 1) * ML_HEAD_DIM)
            q = proj_ref[rows, Q_LO + h * ML_HEAD_DIM:Q_LO + (h + 1) * ML_HEAD_DIM].astype(BF16)
            v = proj_ref[rows, V_LO + h * ML_HEAD_DIM:V_LO + (h + 1) * ML_HEAD_DIM].astype(BF16)
            o_gate = proj_ref[rows, O_LO + h * ML_HEAD_DIM:O_LO + (h + 1) * ML_HEAD_DIM]
            kt = colp_ref[hs, r0:r0 + CHUNK]
            v_ext = jnp.concatenate([v, ones_col], axis=1)

            b_r = b_row[h:h + 1, :]
            b_c = jnp.sum(jnp.where(diag, b_r, 0.0), axis=-1, keepdims=True)
            log_d = jnp.where(causal, b_c - b_r + logi[h:h + 1, :], NEG_INF)
            mp = m_prev[h:h + 1, :]
            log_inter = b_c + mp
            m_row = jnp.maximum(log_inter, jnp.max(log_d, axis=-1, keepdims=True))
            s = _dot(q, kt.astype(BF16))
            p = s * (jnp.exp(log_d - m_row) * K_SCALE)
            w_inter = jnp.exp(log_inter - m_row)
            c_ext = cstate_ref[st * ML_HEADS + h]
            intra = _dot(p.astype(BF16), v_ext)
            inter = _dot(q, c_ext.astype(BF16))
            tot = intra + w_inter * inter
            den = jnp.maximum(jnp.abs(tot[:, ML_HEAD_DIM:ML_HEAD_DIM + 1]), jnp.exp(-m_row))
            hh = tot[:, 0:ML_HEAD_DIM] / den

            ktw = (kt * w_row[h:h + 1, :]).astype(BF16)
            cstate_ref[st * ML_HEADS + h] = decay[h:h + 1, :] * c_ext + _dot(ktw, v_ext)

            mu = jnp.mean(hh, axis=-1, keepdims=True)
            hc = hh - mu
            var = jnp.mean(hc * hc, axis=-1, keepdims=True)
            hn = hc * lax.rsqrt(var + 1e-6) * mnw_ref[:, hs]
            hcat_ref[rows, hs] = (jax.nn.sigmoid(o_gate) * hn).astype(BF16)

        u = _gelu_tanh(proj_ref[rows, GU_LO:GU_LO + G_WIDTH])
        z = _layer_norm(_gelu_tanh(proj_ref[rows, GV_LO:GV_LO + G_WIDTH]), gnw_ref[...], gnb_ref[...], LN_EPS)
        lane_head = lax.broadcasted_iota(jnp.int32, (CHUNK, G_WIDTH), 1) // G_HEAD_DIM
        z_bd = jnp.concatenate(
            [jnp.where(lane_head == h, z, 0.0).astype(BF16) for h in range(G_HEADS)], axis=0)
        zs = _dot(ws_ref[...], z_bd) + bs_ref[...]
        hcat_ref[rows, ML_WIDTH:ML_WIDTH + G_WIDTH] = (u * zs).astype(BF16)

        ca = proj_ref[rows, CA_LO:CA_LO + C_WIDTH]
        cb = proj_ref[rows, CB_LO:CB_LO + C_WIDTH]
        cbuf_ref[st, CONV_HALO + c * CHUNK:CONV_HALO + (c + 1) * CHUNK, :] = ca * jax.nn.sigmoid(cb)

    first_tap = CONV_HALO - (CONV_WIDTH - 1)
    for st in range(STREAMS):
        for sh in range(1, SUBLANES):
            shift_ref[st, sh - 1] = cbuf_ref[st, sh:sh + tm + CONV_SPAN, :]
        for c in range(n_chunks):
            acc = jnp.zeros((CHUNK, C_WIDTH), F32) + cb_ref[...]
            for k in range(CONV_WIDTH):
                whole, sh = divmod(first_tap + k, SUBLANES)
                lo = c * CHUNK + whole * SUBLANES
                tap = cbuf_ref[st, lo:lo + CHUNK, :] if sh == 0 else shift_ref[st, sh - 1, lo:lo + CHUNK, :]
                acc = acc + cw_ref[k:k + 1, :] * tap
            cn = _layer_norm(acc, cnw_ref[...], cnb_ref[...], LN_EPS)
            r0 = st * tm + c * CHUNK
            hcat_ref[r0:r0 + CHUNK, ML_WIDTH + G_WIDTH:D_MODEL] = (cn * jax.nn.sigmoid(cn)).astype(BF16)
        cbuf_ref[st, 0:CONV_HALO, :] = cbuf_ref[st, tm:tm + CONV_HALO, :]

    y = _dot(hcat_ref[...], w_out_ref[...])
    x_new = _layer_norm(DEEPNORM_ALPHA * x + y, l1w_ref[...], l1b_ref[...], LN_EPS)
    return x_new


def _prepare_in_proj(w_in_ref, w_row_ref, w_col_ref):
    for r0 in range(0, D_MODEL, W_PREP_ROWS):
        rs = slice(r0, r0 + W_PREP_ROWS)
        w_row_ref[rs, Q_LO:V_LO] = w_in_ref[rs, IN_Q[0]:IN_Q[1]].astype(BF16)
        w_row_ref[rs, V_LO:GU_LO] = w_in_ref[rs, IN_VO[0]:IN_VO[1]].astype(BF16)
        w_row_ref[rs, GU_LO:P_ROW] = w_in_ref[rs, IN_GATES[0]:P_IN][:, IN_REST[0] - IN_GATES[0]:].astype(BF16)
        w_col_ref[0:ML_WIDTH, rs] = w_in_ref[rs, IN_K[0]:IN_K[1]].T.astype(BF16)
        gates_t = w_in_ref[rs, IN_GATES[0]:IN_GATES[0] + LANES].T
        keep = lax.broadcasted_iota(jnp.int32, (P_COL_PAD - ML_WIDTH, W_PREP_ROWS), 0) < 2 * ML_HEADS
        w_col_ref[ML_WIDTH:P_COL_PAD, rs] = jnp.where(keep, gates_t[0:P_COL_PAD - ML_WIDTH, :], 0.0).astype(BF16)


def _mixer_kernel(x_ref, w_in_ref, w_out_f32_ref, b_row_ref, b_col_ref, mnw_ref, gnw_ref, gnb_ref,
                  ws_ref, bs_ref, cw_ref, cb_ref, cnw_ref, cnb_ref, l1w_ref, l1b_ref, rw_ref, rb_ref,
                  o_ref, metat_ref, tile_n_ref, tile_cnt_ref,
                  w_row_ref, w_col_ref, w_out_ref, proj_ref, colp_ref, hcat_ref, cstate_ref, mstate_ref, cbuf_ref, shift_ref,
                  cnt_ref, *, tm, steps_per_seq):
    i = pl.program_id(0)

    @pl.when(i == 0)
    def _():
        cnt_ref[...] = jnp.zeros_like(cnt_ref)
        _prepare_in_proj(w_in_ref, w_row_ref, w_col_ref)
        for r0 in range(0, D_MODEL, W_PREP_ROWS):
            w_out_ref[r0:r0 + W_PREP_ROWS, :] = w_out_f32_ref[r0:r0 + W_PREP_ROWS, :].astype(BF16)

    @pl.when(i % steps_per_seq == 0)
    def _():
        cstate_ref[...] = jnp.zeros_like(cstate_ref)
        mstate_ref[...] = jnp.zeros_like(mstate_ref)
        cbuf_ref[:, 0:CONV_HALO, :] = jnp.zeros((STREAMS, CONV_HALO, C_WIDTH), F32)

    x = x_ref[...].reshape(STREAMS * tm, D_MODEL)
    _project(x, w_row_ref, b_row_ref, w_col_ref, b_col_ref, proj_ref, colp_ref)
    x_new = _mix_tile(x, proj_ref, colp_ref, mnw_ref, gnw_ref, gnb_ref, ws_ref, bs_ref, cw_ref, cb_ref, cnw_ref,
                      cnb_ref, w_out_ref, l1w_ref, l1b_ref, hcat_ref, cstate_ref, mstate_ref, cbuf_ref, shift_ref,
                      tm)
    o_ref[...] = x_new.reshape(STREAMS, tm, D_MODEL)
    logits = _router_logits(x_new, rw_ref)
    for st in range(STREAMS):
        _route_tile(logits[st * tm:(st + 1) * tm, :].T[0:N_EXPERTS, :], rb_ref, metat_ref.at[st], tile_n_ref.at[st],
                    tile_cnt_ref.at[st], cnt_ref)


def _full(shape):
    nd = len(shape)
    return pl.BlockSpec(shape, lambda i, _nd=nd: (0,) * _nd, pipeline_mode=pl.Buffered(1))


def _layer_of(stacked, layer):
    nd = stacked.ndim - 1
    return pl.BlockSpec((None,) + stacked.shape[1:], lambda i, _nd=nd: (layer,) + (0,) * _nd,
                        pipeline_mode=pl.Buffered(1))


def _mixer_layer(layer, x, w_in, w_out, b_row, b_col, mnw, gnw, gnb, ws_cat, bs_full, cw, cb, cnw, cnb,
                 l1w, l1b, rw_split, rb_col):
    batch, seq, _ = x.shape
    tm = MOE_TM
    steps_per_seq = seq // tm
    n_steps = batch // STREAMS * steps_per_seq
    n_tiles = batch * seq // tm
    rows = STREAMS * tm
    kernel = functools.partial(_mixer_kernel, tm=tm, steps_per_seq=steps_per_seq)
    weights = (b_row, b_col, mnw, gnw, gnb, ws_cat, bs_full, cw, cb, cnw, cnb, l1w, l1b, rw_split, rb_col)
    tile_spec = pl.BlockSpec((STREAMS, N_EXPERTS, LANES), lambda i: (i, 0, 0))
    step_spec = pl.BlockSpec((STREAMS, tm, D_MODEL), lambda i: (i // steps_per_seq, i % steps_per_seq, 0))
    return pl.pallas_call(
        kernel,
        out_shape=(
            jax.ShapeDtypeStruct(x.shape, F32),
            jax.ShapeDtypeStruct((n_tiles, SUBLANES, tm), F32),
            jax.ShapeDtypeStruct((n_tiles, N_EXPERTS, LANES), jnp.int32),
            jax.ShapeDtypeStruct((n_tiles, N_EXPERTS, LANES), jnp.int32),
        ),
        grid=(n_steps,),
        in_specs=[step_spec, _layer_of(w_in, layer), _layer_of(w_out, layer)] + [_full(w.shape) for w in weights],
        out_specs=(step_spec, pl.BlockSpec((STREAMS, SUBLANES, tm), lambda i: (i, 0, 0)), tile_spec, tile_spec),
        scratch_shapes=[
            pltpu.VMEM((D_MODEL, P_ROW), BF16),
            pltpu.VMEM((P_COL_PAD, D_MODEL), BF16),
            pltpu.VMEM((D_MODEL, D_MODEL), BF16),
            pltpu.VMEM((rows, P_ROW), F32),
            pltpu.VMEM((P_COL_PAD, rows), F32),
            pltpu.VMEM((rows, D_MODEL), BF16),
            pltpu.VMEM((STREAMS * ML_HEADS, ML_HEAD_DIM, 2 * ML_HEAD_DIM), F32),
            pltpu.VMEM((STREAMS * SUBLANES, LANES), F32),
            pltpu.VMEM((STREAMS, tm + CONV_HALO, C_WIDTH), F32),
            pltpu.VMEM((STREAMS, SUBLANES - 1, tm + CONV_SPAN, C_WIDTH), F32),
            pltpu.VMEM((N_EXPERTS, LANES), F32),
        ],
        compiler_params=pltpu.CompilerParams(
            dimension_semantics=("arbitrary",), vmem_limit_bytes=VMEM_LIMIT),
        name="mixer",
    )(x, w_in, w_out, *weights)


def _pow2_pieces(n, largest, act):
    piece = largest
    while piece >= SUBLANES:
        start = jnp.bitwise_and(n, -2 * piece)

        @pl.when(jnp.bitwise_and(n, piece) != 0)
        def _(piece=piece, start=start):
            act(start, piece)
        piece //= 2


def _for_each_run(n_tab, cnt_tab, base_tab, tile, make_copy, act):
    def body(e, off):
        n = n_tab[tile * N_EXPERTS + e]
        base = base_tab[e] + cnt_tab[tile * N_EXPERTS + e]
        _pow2_pieces(n, MOE_TM, lambda start, size: act(make_copy(
            pl.multiple_of(off + start, SUBLANES), pl.multiple_of(base + start, SUBLANES), size)))
        return off + n

    lax.fori_loop(0, N_EXPERTS, body, jnp.int32(0))


def _dispatch_kernel(n_tab, cnt_tab, base_tab, fill_tab, x_ref, metat_ref, xs_hbm, comp_ref, zero_ref, sem,
                     *, n_rows):
    i = pl.program_id(0)
    last = pl.num_programs(0) - 1
    slot = lax.rem(i, 2)

    def copies(step, slot_, act):
        for k in range(STREAMS):
            def make_copy(src_row, dst_row, rows, k=k):
                return pltpu.make_async_copy(comp_ref.at[slot_, k, pl.ds(src_row, rows)],
                                             xs_hbm.at[pl.ds(dst_row, rows)], sem.at[slot_])
            _for_each_run(n_tab, cnt_tab, base_tab, step * STREAMS + k, make_copy, act)

    def zero_fill(act):
        def zero_copy(dst_row, rows):
            return pltpu.make_async_copy(zero_ref.at[pl.ds(0, rows)],
                                         xs_hbm.at[pl.ds(pl.multiple_of(dst_row, SUBLANES), rows)], sem.at[2])

        def per_expert(e, carry):
            first = fill_tab[2 * e]
            _pow2_pieces(fill_tab[2 * e + 1], ROW_TILE // 2, lambda start, size: act(zero_copy(first + start, size)))
            return carry

        lax.fori_loop(0, N_EXPERTS, per_expert, jnp.int32(0))

        def per_half_tile(k, carry):
            act(zero_copy(fill_tab[2 * N_EXPERTS] + k * (ROW_TILE // 2), ROW_TILE // 2))
            return carry

        lax.fori_loop(0, (n_rows - fill_tab[2 * N_EXPERTS]) // (ROW_TILE // 2), per_half_tile, jnp.int32(0))

    @pl.when(i == 0)
    def _():
        zero_ref[...] = jnp.zeros_like(zero_ref)
        zero_fill(lambda cp: cp.start())

    @pl.when(i >= 2)
    def _():
        copies(i - 2, slot, lambda cp: cp.wait())

    row_f = lax.broadcasted_iota(jnp.int32, (COMPACT_ROWS, MOE_TM), 0).astype(F32)
    for k in range(STREAMS):
        mt = metat_ref[k]
        hit1 = row_f == mt[0:1, :]
        hit2 = row_f == mt[1:2, :]
        onehot = jnp.where(hit1, 1.0, jnp.where(hit2, 1.0, 0.0)).astype(BF16)
        comp_ref[slot, k, :, 0:D_MODEL] = _dot(onehot, x_ref[k].astype(BF16))
        gate = jnp.sum(jnp.where(hit1, mt[2:3, :], jnp.where(hit2, mt[3:4, :], 0.0)), axis=-1, keepdims=True)
        comp_ref[slot, k, :, D_MODEL:PAYLOAD_W] = jnp.broadcast_to(gate, (COMPACT_ROWS, LANES))
    copies(i, slot, lambda cp: cp.start())

    @pl.when(i == last)
    def _():
        @pl.when(i >= 1)
        def _():
            copies(i - 1, 1 - slot, lambda cp: cp.wait())
        copies(i, slot, lambda cp: cp.wait())
        zero_fill(lambda cp: cp.wait())


def _token_step_spec(seq):
    tiles_per_seq = seq // MOE_TM
    return pl.BlockSpec((STREAMS, MOE_TM, D_MODEL), lambda i, *_: (i // tiles_per_seq, i % tiles_per_seq, 0))


_ROUTE_STEP_SPEC = pl.BlockSpec((STREAMS, SUBLANES, MOE_TM), lambda i, *_: (i, 0, 0))


def _dispatch_layer(n_tab, cnt_tab, base_tab, fill_tab, x, metat, n_rows):
    batch, seq, _ = x.shape
    return pl.pallas_call(
        functools.partial(_dispatch_kernel, n_rows=n_rows),
        out_shape=jax.ShapeDtypeStruct((n_rows, PAYLOAD_W), F32),
        grid_spec=pltpu.PrefetchScalarGridSpec(
            num_scalar_prefetch=4,
            grid=(batch * seq // (STREAMS * MOE_TM),),
            in_specs=[_token_step_spec(seq), _ROUTE_STEP_SPEC],
            out_specs=pl.BlockSpec(memory_space=pl.ANY),
            scratch_shapes=[
                pltpu.VMEM((2, STREAMS, COMPACT_ROWS, PAYLOAD_W), F32),
                pltpu.VMEM((ROW_TILE // 2, PAYLOAD_W), F32),
                pltpu.SemaphoreType.DMA((3,)),
            ],
        ),
        compiler_params=pltpu.CompilerParams(dimension_semantics=("arbitrary",), vmem_limit_bytes=VMEM_LIMIT),
        name="dispatch",
    )(n_tab, cnt_tab, base_tab, fill_tab, x, metat)


def _expert_kernel(tile_e, tile_valid, xs_ref, wg_ref, wu_ref, wd_ref, ys_ref, wgb_ref, wub_ref, wdb_ref):
    i = pl.program_id(0)
    valid = tile_valid[i]
    new_expert = jnp.logical_or(i == 0, tile_e[i] != tile_e[jnp.maximum(i - 1, 0)])

    @pl.when(jnp.logical_and(valid > 0, new_expert))
    def _():
        wgb_ref[...] = wg_ref[...].astype(BF16)
        wub_ref[...] = wu_ref[...].astype(BF16)
        wdb_ref[...] = wd_ref[...].astype(BF16)

    @pl.when(valid > 0)
    def _():
        half = ROW_TILE // 2
        rows = [slice(0, half), slice(half, ROW_TILE)]
        xb = [xs_ref[r, 0:D_MODEL].astype(BF16) for r in rows]
        g = [_dot(xb[h], wgb_ref[...]) for h in range(2)]
        u = [_dot(xb[h], wub_ref[...]) for h in range(2)]
        for h in range(2):
            hid = (g[h] * jax.nn.sigmoid(g[h]) * u[h]).astype(BF16)
            ys_ref[rows[h], :] = xs_ref[rows[h], D_MODEL:D_MODEL + 1] * _dot(hid, wdb_ref[...])

    @pl.when(valid == 0)
    def _():
        ys_ref[...] = jnp.zeros_like(ys_ref)


def _expert_layer(layer, tile_e, tile_valid, xs, wg, wu, wd):
    n_steps = tile_e.shape[0]
    return pl.pallas_call(
        _expert_kernel,
        out_shape=jax.ShapeDtypeStruct((xs.shape[0], D_MODEL), F32),
        grid_spec=pltpu.PrefetchScalarGridSpec(
            num_scalar_prefetch=2,
            grid=(n_steps,),
            in_specs=[
                pl.BlockSpec((ROW_TILE, PAYLOAD_W), lambda i, e, v: (i, 0)),
                pl.BlockSpec((None, None, D_MODEL, D_EXPERT), lambda i, e, v: (layer, e[i], 0, 0)),
                pl.BlockSpec((None, None, D_MODEL, D_EXPERT), lambda i, e, v: (layer, e[i], 0, 0)),
                pl.BlockSpec((None, None, D_EXPERT, D_MODEL), lambda i, e, v: (layer, e[i], 0, 0)),
            ],
            out_specs=pl.BlockSpec((ROW_TILE, D_MODEL), lambda i, e, v: (i, 0)),
            scratch_shapes=[
                pltpu.VMEM((D_MODEL, D_EXPERT), BF16),
                pltpu.VMEM((D_MODEL, D_EXPERT), BF16),
                pltpu.VMEM((D_EXPERT, D_MODEL), BF16),
            ],
        ),
        compiler_params=pltpu.CompilerParams(dimension_semantics=("arbitrary",), vmem_limit_bytes=VMEM_LIMIT),
        name="experts",
    )(tile_e, tile_valid, xs, wg, wu, wd)


def _combine_kernel(n_tab, cnt_tab, base_tab, x_ref, metat_ref, ys_hbm, l2w_ref, l2b_ref, o_ref, yc_ref, sem):
    i = pl.program_id(0)
    n_steps = pl.num_programs(0)
    slot = lax.rem(i, 2)

    def copies(step, slot_, act):
        for k in range(STREAMS):
            def make_copy(buf_row, ys_row, rows, k=k):
                return pltpu.make_async_copy(ys_hbm.at[pl.ds(ys_row, rows)],
                                             yc_ref.at[slot_, k, pl.ds(buf_row, rows)], sem.at[slot_])
            _for_each_run(n_tab, cnt_tab, base_tab, step * STREAMS + k, make_copy, act)

    @pl.when(i == 0)
    def _():
        yc_ref[...] = jnp.zeros_like(yc_ref)
        copies(0, 0, lambda cp: cp.start())

    @pl.when(i + 1 < n_steps)
    def _():
        copies(i + 1, 1 - slot, lambda cp: cp.start())

    copies(i, slot, lambda cp: cp.wait())

    t_r = lax.broadcasted_iota(jnp.int32, (MOE_TM, MOE_TM), 0)
    t_c = lax.broadcasted_iota(jnp.int32, (MOE_TM, MOE_TM), 1)
    as_col = lambda row: jnp.sum(jnp.where(t_r == t_c, row, 0.0), axis=-1, keepdims=True)
    col_f = lax.broadcasted_iota(jnp.int32, (MOE_TM, COMPACT_ROWS), 1).astype(F32)
    for k in range(STREAMS):
        mt = metat_ref[k]
        onehot = jnp.where(col_f == as_col(mt[0:1, :]), 1.0,
                           jnp.where(col_f == as_col(mt[1:2, :]), 1.0, 0.0)).astype(BF16)
        y = _dot(onehot, yc_ref[slot, k].astype(BF16))
        o_ref[k] = _layer_norm(DEEPNORM_ALPHA * x_ref[k] + y, l2w_ref[...], l2b_ref[...], LN_EPS)


def _combine_layer(n_tab, cnt_tab, base_tab, x, metat, ys, l2w, l2b):
    batch, seq, _ = x.shape
    return pl.pallas_call(
        _combine_kernel,
        out_shape=jax.ShapeDtypeStruct(x.shape, F32),
        grid_spec=pltpu.PrefetchScalarGridSpec(
            num_scalar_prefetch=3,
            grid=(batch * seq // (STREAMS * MOE_TM),),
            in_specs=[
                _token_step_spec(seq),
                _ROUTE_STEP_SPEC,
                pl.BlockSpec(memory_space=pl.ANY),
                pl.BlockSpec((1, D_MODEL), lambda i, *_: (0, 0)),
                pl.BlockSpec((1, D_MODEL), lambda i, *_: (0, 0)),
            ],
            out_specs=_token_step_spec(seq),
            scratch_shapes=[pltpu.VMEM((2, STREAMS, COMPACT_ROWS, D_MODEL), F32), pltpu.SemaphoreType.DMA((2,))],
        ),
        compiler_params=pltpu.CompilerParams(dimension_semantics=("arbitrary",), vmem_limit_bytes=VMEM_LIMIT),
        name="combine",
    )(n_tab, cnt_tab, base_tab, x, metat, ys, l2w, l2b)


def _expert_plan(counts, n_steps):
    tiles_e = (counts + (ROW_TILE - 1)) // ROW_TILE
    cum = jnp.cumsum(tiles_e)
    first_tile = cum - tiles_e
    total = cum[-1]
    step = jnp.arange(n_steps, dtype=jnp.int32)
    owner = (step[:, None] >= cum[None, :]).sum(axis=1).astype(jnp.int32)
    is_owner = owner[:, None] == jnp.arange(N_EXPERTS, dtype=jnp.int32)[None, :]
    local = step - jnp.where(is_owner, first_tile[None, :], 0).sum(axis=1)
    rows_left = jnp.where(is_owner, counts[None, :], 0).sum(axis=1) - local * ROW_TILE
    valid = jnp.where(step < total, jnp.clip(rows_left, 0, ROW_TILE), 0).astype(jnp.int32)
    tile_e = jnp.minimum(owner, N_EXPERTS - 1)
    base = (first_tile * ROW_TILE).astype(jnp.int32)
    fill = jnp.stack([base + counts, tiles_e * ROW_TILE - counts], axis=1).reshape(-1)
    fill_tab = jnp.concatenate([fill, (total * ROW_TILE)[None]]).astype(jnp.int32)
    return base, tile_e, valid, fill_tab


def _moe_layer(layer, x, metat, tile_n, tile_cnt, wg, wu, wd, l2w, l2b):
    n_tok = x.shape[0] * x.shape[1]
    n_tiles = n_tok // MOE_TM
    n_steps = (2 * n_tok + N_EXPERTS * (SUBLANES - 1) * n_tiles) // ROW_TILE + N_EXPERTS
    n_tab = tile_n[:, :, 0].reshape(-1)
    cnt_tab = tile_cnt[:, :, 0].reshape(-1)
    counts = tile_cnt[-1, :, 0] + tile_n[-1, :, 0]
    base_tab, tile_e, tile_valid, fill_tab = _expert_plan(counts, n_steps)
    xs = _dispatch_layer(n_tab, cnt_tab, base_tab, fill_tab, x, metat, n_steps * ROW_TILE)
    ys = _expert_layer(layer, tile_e, tile_valid, xs, wg, wu, wd)
    return _combine_layer(n_tab, cnt_tab, base_tab, x, metat, ys, l2w, l2b)


def kernel(x, w_in, b_in, mlstm_norm_w, gmlp_norm_w, gmlp_norm_b, gmlp_ws, gmlp_bs, conv_w, conv_b,
           conv_norm_w, conv_norm_b, w_out, ln1_w, ln1_b, router_w, router_b, w_gate, w_up, w_down,
           ln2_w, ln2_b):
    batch, seq, d = x.shape

    q_lo, k_lo, v_lo = 0, ML_WIDTH, 2 * ML_WIDTH
    gate_lo = 4 * ML_WIDTH
    rest_lo = gate_lo + 2 * ML_HEADS

    def row_part(t):
        return jnp.concatenate([t[..., q_lo:k_lo], t[..., v_lo:gate_lo], t[..., rest_lo:]], axis=-1)

    def col_part(t):
        return jnp.concatenate([t[..., k_lo:v_lo], t[..., gate_lo:rest_lo]], axis=-1)

    tril = jnp.tril(jnp.ones((CHUNK, CHUNK), gmlp_ws.dtype))
    rw_pad = jnp.pad(router_w.astype(F32), ((0, 0), (0, LANES - N_EXPERTS)))
    rw_head = rw_pad.astype(BF16)
    rw_split = jnp.concatenate([rw_head, (rw_pad - rw_head.astype(F32)).astype(BF16)], axis=1)
    rb_col = router_b.astype(F32).reshape(N_EXPERTS, 1)

    for l in range(DEPTH):
        b_row = row_part(b_in[l]).reshape(1, P_ROW)
        b_col = jnp.pad(col_part(b_in[l]), (0, P_COL_PAD - P_COL)).reshape(P_COL_PAD, 1)
        ws_cat = jnp.transpose(gmlp_ws[l] * tril, (1, 0, 2)).reshape(CHUNK, G_HEADS * CHUNK).astype(BF16)
        bs_full = jnp.repeat(gmlp_bs[l].T, G_HEAD_DIM, axis=1)
        cw = jnp.pad(conv_w[l], ((0, 1), (0, 0)))
        x, metat, tile_n, tile_cnt = _mixer_layer(
            l, x, w_in, w_out, b_row, b_col, mlstm_norm_w[l].reshape(1, -1), gmlp_norm_w[l].reshape(1, -1),
            gmlp_norm_b[l].reshape(1, -1), ws_cat, bs_full, cw, conv_b[l].reshape(1, -1),
            conv_norm_w[l].reshape(1, -1), conv_norm_b[l].reshape(1, -1),
            ln1_w[l].reshape(1, -1), ln1_b[l].reshape(1, -1), rw_split, rb_col)
        x = _moe_layer(l, x, metat, tile_n, tile_cnt, w_gate, w_up, w_down,
                       ln2_w[l].reshape(1, -1), ln2_b[l].reshape(1, -1))
    return x
```

```python
import functools

import jax
import jax.numpy as jnp
from jax import lax
from jax.experimental import pallas as pl
from jax.experimental.pallas import tpu as pltpu

D_MODEL = 1024
DEPTH = 4
ML_WIDTH = 512
ML_HEADS = 4
ML_HEAD_DIM = 128
CHUNK = 128
G_WIDTH = 256
G_HEADS = 4
G_HEAD_DIM = 64
C_WIDTH = 256
CONV_WIDTH = 31
N_EXPERTS = 16
N_GROUPS = 4
EXPERTS_PER_GROUP = 4
D_EXPERT = 512
DEEPNORM_ALPHA = (2.0 * DEPTH) ** 0.25
LN_EPS = 1e-5
K_SCALE = ML_HEAD_DIM ** -0.5

Q_LO, V_LO, O_LO, GU_LO, GV_LO, CA_LO, CB_LO = 0, 512, 1024, 1536, 1792, 2048, 2304
P_ROW = 2560
P_COL = ML_WIDTH + 2 * ML_HEADS
P_COL_PAD = ML_WIDTH + 16
IN_Q, IN_K, IN_VO, IN_GATES, IN_REST = (0, 512), (512, 1024), (1024, 2048), (2048, 2056), (2056, 3080)
P_IN = 3080
W_PREP_ROWS = 256

LANES = 128
SUBLANES = 8
MIX_TM = 256
ROW_ALIGN = 16
ROW_TILE = 512
EXPERT_ROW_BLOCKS = 2
PAYLOAD_W = D_MODEL + LANES
CONV_HALO = 32
CONV_SPAN = CONV_HALO - SUBLANES
STREAMS = 2
MOE_TM = STREAMS * MIX_TM
COMPACT_ROWS = 2 * MOE_TM + 2 * LANES
assert COMPACT_ROWS >= 2 * MOE_TM + N_EXPERTS * (ROW_ALIGN - 1)
GATE_PIECES = 3
VMEM_LIMIT = 58 * 1024 * 1024

F32 = jnp.float32
BF16 = jnp.bfloat16
NEG_INF = float("-inf")


def _layer_norm(x, w, b, eps):
    mu = jnp.mean(x, axis=-1, keepdims=True)
    xc = x - mu
    var = jnp.mean(xc * xc, axis=-1, keepdims=True)
    return xc * lax.rsqrt(var + eps) * w + b


def _gelu_tanh(x):
    return 0.5 * x * (1.0 + jnp.tanh(0.7978845608028654 * (x + 0.044715 * (x * x * x))))


def _log_sigmoid(x):
    return jnp.minimum(x, 0.0) - jnp.log1p(jnp.exp(-jnp.abs(x)))


def _dot(a, b):
    return jnp.dot(a, b, preferred_element_type=F32)


def _dot_nt(a, b):
    return lax.dot_general(a, b, (((1,), (1,)), ((), ())), preferred_element_type=F32)


def _router_logits(x_new, rw_ref):
    xh = x_new.astype(BF16)
    xl = (x_new - xh.astype(F32)).astype(BF16)
    head = _dot(xh, rw_ref[...])
    return head[:, 0:LANES] + head[:, LANES:2 * LANES] + _dot(xl, rw_ref[:, 0:LANES])


def _route_tile(logits, rb_ref, metat_ref, tile_n_ref, tile_cnt_ref, cnt_ref):
    tm = logits.shape[1]
    s_all = jax.nn.sigmoid(logits)
    sel_all = s_all + rb_ref[...]
    rows_of = lambda v: [v[k:k + 1, :] for k in range(N_EXPERTS)]
    s, sel = rows_of(s_all), rows_of(sel_all)

    best = None
    for g in range(N_GROUPS):
        r = sel[EXPERTS_PER_GROUP * g:EXPERTS_PER_GROUP * (g + 1)]
        hi01, lo01 = jnp.maximum(r[0], r[1]), jnp.minimum(r[0], r[1])
        hi23, lo23 = jnp.maximum(r[2], r[3]), jnp.minimum(r[2], r[3])
        score = jnp.maximum(hi01, hi23) + jnp.maximum(jnp.minimum(hi01, hi23), jnp.maximum(lo01, lo23))
        if best is None:
            best, gidx = score, jnp.zeros(score.shape, jnp.int32)
        else:
            better = score > best
            gidx = jnp.where(better, g, gidx)
            best = jnp.where(better, score, best)

    def of_group(rows):
        out = []
        for j in range(EXPERTS_PER_GROUP):
            v = rows[j]
            for g in range(1, N_GROUPS):
                v = jnp.where(gidx == g, rows[EXPERTS_PER_GROUP * g + j], v)
            out.append(v)
        return out

    def first_max(vals):
        best_v, best_j = vals[0], jnp.zeros(vals[0].shape, F32)
        for j in range(1, len(vals)):
            better = vals[j] > best_v
            best_j = jnp.where(better, float(j), best_j)
            best_v = jnp.where(better, vals[j], best_v)
        return best_j

    def take(vals, idx):
        v = vals[0]
        for j in range(1, len(vals)):
            v = jnp.where(idx == float(j), vals[j], v)
        return v

    cand, cand_s = of_group(sel), of_group(s)
    j1 = first_max(cand)
    j2 = first_max([jnp.where(j1 == float(j), NEG_INF, cand[j]) for j in range(EXPERTS_PER_GROUP)])
    g1, g2 = take(cand_s, j1), take(cand_s, j2)
    tot = g1 + g2
    first_of_group = gidx.astype(F32) * float(EXPERTS_PER_GROUP)
    e1, e2 = first_of_group + j1, first_of_group + j2

    e_f = lax.broadcasted_iota(jnp.int32, (N_EXPERTS, tm), 0).astype(F32)
    assign = jnp.where(e_f == e1, 1.0, jnp.where(e_f == e2, 1.0, 0.0)).astype(BF16)
    t_r = lax.broadcasted_iota(jnp.int32, (tm, tm), 0)
    t_c = lax.broadcasted_iota(jnp.int32, (tm, tm), 1)
    earlier = jnp.where(t_r < t_c, 1.0, 0.0).astype(BF16)
    rank = _dot(assign, earlier)
    n_b = _dot(assign, jnp.ones((tm, LANES), BF16))
    n_up_b = jnp.floor((n_b + (ROW_ALIGN - 1.0)) * (1.0 / ROW_ALIGN)) * ROW_ALIGN
    x_r = lax.broadcasted_iota(jnp.int32, (N_EXPERTS, N_EXPERTS), 0)
    x_c = lax.broadcasted_iota(jnp.int32, (N_EXPERTS, N_EXPERTS), 1)
    lower = jnp.where(x_c < x_r, 1.0, 0.0).astype(BF16)
    off_b = _dot(lower, n_up_b.astype(BF16))
    pos = rows_of(jnp.concatenate([off_b] * (tm // LANES), axis=1) + rank)
    metat_ref[...] = jnp.concatenate(
        [take(pos, e1), take(pos, e2), g1 / tot, g2 / tot, jnp.zeros((SUBLANES - 4, tm), F32)], axis=0)

    cnt = cnt_ref[...]
    tile_n_ref[...] = n_up_b.astype(jnp.int32)
    tile_cnt_ref[...] = cnt.astype(jnp.int32)
    cnt_ref[...] = cnt + n_up_b


def _project(x_tile, w_row_ref, b_row_ref, w_col_ref, b_col_ref, proj_ref, colp_ref):
    xb = x_tile.astype(BF16)
    proj_ref[...] = _dot(xb, w_row_ref[...]) + b_row_ref[...]
    colp_ref[...] = _dot_nt(w_col_ref[...], xb) + b_col_ref[...]


def _mix_tile(x, proj_ref, colp_ref, mnw_ref, gnw_ref, gnb_ref, ws_ref, bs_ref, cw_ref, cb_ref, cnw_ref,
              cnb_ref, w_out_ref, l1w_ref, l1b_ref, hcat_ref, cstate_ref, mstate_ref, cbuf_ref, shift_ref, tm):
    n_chunks = tm // CHUNK
    gates = colp_ref[ML_WIDTH:P_COL, :]
    lane_in_chunk = lax.broadcasted_iota(jnp.int32, gates.shape, 1) % CHUNK
    logf_all = _log_sigmoid(gates)
    bcum = logf_all
    d = 1
    while d < CHUNK:
        bcum = bcum + jnp.where(lane_in_chunk >= d, pltpu.roll(bcum, d, 1), 0.0)
        d *= 2

    row_i = lax.broadcasted_iota(jnp.int32, (CHUNK, CHUNK), 0)
    col_i = lax.broadcasted_iota(jnp.int32, (CHUNK, CHUNK), 1)
    causal = col_i <= row_i
    diag = col_i == row_i
    ones_col = jnp.where(col_i == 0, 1.0, 0.0).astype(BF16)

    for sc in range(STREAMS * n_chunks):
        st, c = divmod(sc, n_chunks)
        r0 = st * tm + c * CHUNK
        rows = slice(r0, r0 + CHUNK)
        m_rows = slice(st * SUBLANES, st * SUBLANES + ML_HEADS)
        logi = gates[0:ML_HEADS, r0:r0 + CHUNK]
        logf = logf_all[ML_HEADS:2 * ML_HEADS, r0:r0 + CHUNK]
        b_row = bcum[ML_HEADS:2 * ML_HEADS, r0:r0 + CHUNK]
        g = jnp.sum(logf, axis=-1, keepdims=True)
        m_prev = mstate_ref[m_rows, 0:1]
        a_row = g - b_row + logi
        m_new = jnp.maximum(g + m_prev, jnp.max(a_row, axis=-1, keepdims=True))
        w_row = jnp.exp(a_row - m_new) * K_SCALE
        decay = jnp.exp(g + m_prev - m_new)
        mstate_ref[m_rows, :] = jnp.broadcast_to(m_new, (ML_HEADS, LANES))

        for h in range(ML_HEADS):
            hs = slice(h * ML_HEAD_DIM, (h + 1) * ML_HEAD_DIM)
            q = proj_ref[rows, Q_LO + h * ML_HEAD_DIM:Q_LO + (h + 1) * ML_HEAD_DIM].astype(BF16)
            v = proj_ref[rows, V_LO + h * ML_HEAD_DIM:V_LO + (h + 1) * ML_HEAD_DIM].astype(BF16)
            o_gate = proj_ref[rows, O_LO + h * ML_HEAD_DIM:O_LO + (h + 1) * ML_HEAD_DIM]
            kt = colp_ref[hs, r0:r0 + CHUNK]
            v_ext = jnp.concatenate([v, ones_col], axis=1)

            b_r = b_row[h:h + 1, :]
            b_c = jnp.sum(jnp.where(diag, b_r, 0.0), axis=-1, keepdims=True)
            log_d = jnp.where(causal, b_c - b_r + logi[h:h + 1, :], NEG_INF)
            mp = m_prev[h:h + 1, :]
            log_inter = b_c + mp
            m_row = jnp.maximum(log_inter, jnp.max(log_d, axis=-1, keepdims=True))
            s = _dot(q, kt.astype(BF16))
            p = s * (jnp.exp(log_d - m_row) * K_SCALE)
            w_inter = jnp.exp(log_inter - m_row)
            c_ext = cstate_ref[st * ML_HEADS + h]
            intra = _dot(p.astype(BF16), v_ext)
            inter = _dot(q, c_ext.astype(BF16))
            tot = intra + w_inter * inter
            den = jnp.maximum(jnp.abs(tot[:, ML_HEAD_DIM:ML_HEAD_DIM + 1]), jnp.exp(-m_row))
            hh = tot[:, 0:ML_HEAD_DIM] / den

            ktw = (kt * w_row[h:h + 1, :]).astype(BF16)
            cstate_ref[st * ML_HEADS + h] = decay[h:h + 1, :] * c_ext + _dot(ktw, v_ext)

            mu = jnp.mean(hh, axis=-1, keepdims=True)
            hc = hh - mu
            var = jnp.mean(hc * hc, axis=-1, keepdims=True)
            hn = hc * lax.rsqrt(var + 1e-6) * mnw_ref[:, hs]
            hcat_ref[rows, hs] = (jax.nn.sigmoid(o_gate) * hn).astype(BF16)

        u = _gelu_tanh(proj_ref[rows, GU_LO:GU_LO + G_WIDTH])
        z = _layer_norm(_gelu_tanh(proj_ref[rows, GV_LO:GV_LO + G_WIDTH]), gnw_ref[...], gnb_ref[...], LN_EPS)
        lane_head = lax.broadcasted_iota(jnp.int32, (CHUNK, G_WIDTH), 1) // G_HEAD_DIM
        z_bd = jnp.concatenate(
            [jnp.where(lane_head == h, z, 0.0).astype(BF16) for h in range(G_HEADS)], axis=0)
        zs = _dot(ws_ref[...], z_bd) + bs_ref[...]
        hcat_ref[rows, ML_WIDTH:ML_WIDTH + G_WIDTH] = (u * zs).astype(BF16)

        ca = proj_ref[rows, CA_LO:CA_LO + C_WIDTH]
        cb = proj_ref[rows, CB_LO:CB_LO + C_WIDTH]
        cbuf_ref[st, CONV_HALO + c * CHUNK:CONV_HALO + (c + 1) * CHUNK, :] = ca * jax.nn.sigmoid(cb)

    first_tap = CONV_HALO - (CONV_WIDTH - 1)
    for st in range(STREAMS):
        for sh in range(1, SUBLANES):
            shift_ref[st, sh - 1] = cbuf_ref[st, sh:sh + tm + CONV_SPAN, :]
        for c in range(n_chunks):
            acc = jnp.zeros((CHUNK, C_WIDTH), F32) + cb_ref[...]
            for k in range(CONV_WIDTH):
                whole, sh = divmod(first_tap + k, SUBLANES)
                lo = c * CHUNK + whole * SUBLANES
                tap = cbuf_ref[st, lo:lo + CHUNK, :] if sh == 0 else shift_ref[st, sh - 1, lo:lo + CHUNK, :]
                acc = acc + cw_ref[k:k + 1, :] * tap
            cn = _layer_norm(acc, cnw_ref[...], cnb_ref[...], LN_EPS)
            r0 = st * tm + c * CHUNK
            hcat_ref[r0:r0 + CHUNK, ML_WIDTH + G_WIDTH:D_MODEL] = (cn * jax.nn.sigmoid(cn)).astype(BF16)
        cbuf_ref[st, 0:CONV_HALO, :] = cbuf_ref[st, tm:tm + CONV_HALO, :]

    y = _dot(hcat_ref[...], w_out_ref[...])
    x_new = _layer_norm(DEEPNORM_ALPHA * x + y, l1w_ref[...], l1b_ref[...], LN_EPS)
    return x_new


def _prepare_in_proj(w_in_ref, w_row_ref, w_col_ref):
    for r0 in range(0, D_MODEL, W_PREP_ROWS):
        rs = slice(r0, r0 + W_PREP_ROWS)
        w_row_ref[rs, Q_LO:V_LO] = w_in_ref[rs, IN_Q[0]:IN_Q[1]].astype(BF16)
        w_row_ref[rs, V_LO:GU_LO] = w_in_ref[rs, IN_VO[0]:IN_VO[1]].astype(BF16)
        w_row_ref[rs, GU_LO:P_ROW] = w_in_ref[rs, IN_GATES[0]:P_IN][:, IN_REST[0] - IN_GATES[0]:].astype(BF16)
        w_col_ref[0:ML_WIDTH, rs] = w_in_ref[rs, IN_K[0]:IN_K[1]].T.astype(BF16)
        gates_t = w_in_ref[rs, IN_GATES[0]:IN_GATES[0] + LANES].T
        keep = lax.broadcasted_iota(jnp.int32, (P_COL_PAD - ML_WIDTH, W_PREP_ROWS), 0) < 2 * ML_HEADS
        w_col_ref[ML_WIDTH:P_COL_PAD, rs] = jnp.where(keep, gates_t[0:P_COL_PAD - ML_WIDTH, :], 0.0).astype(BF16)


def _mixer_kernel(x_ref, w_in_ref, w_out_f32_ref, b_row_ref, b_col_ref, mnw_ref, gnw_ref, gnb_ref,
                  ws_ref, bs_ref, cw_ref, cb_ref, cnw_ref, cnb_ref, l1w_ref, l1b_ref, rw_ref, rb_ref,
                  o_ref, metat_ref, tile_n_ref, tile_cnt_ref,
                  w_row_ref, w_col_ref, w_out_ref, proj_ref, colp_ref, hcat_ref, cstate_ref, mstate_ref, cbuf_ref, shift_ref,
                  cnt_ref, *, tm, steps_per_seq):
    i = pl.program_id(0)

    @pl.when(i == 0)
    def _():
        cnt_ref[...] = jnp.zeros_like(cnt_ref)
        _prepare_in_proj(w_in_ref, w_row_ref, w_col_ref)
        for r0 in range(0, D_MODEL, W_PREP_ROWS):
            w_out_ref[r0:r0 + W_PREP_ROWS, :] = w_out_f32_ref[r0:r0 + W_PREP_ROWS, :].astype(BF16)

    @pl.when(i % steps_per_seq == 0)
    def _():
        cstate_ref[...] = jnp.zeros_like(cstate_ref)
        mstate_ref[...] = jnp.zeros_like(mstate_ref)
        cbuf_ref[:, 0:CONV_HALO, :] = jnp.zeros((STREAMS, CONV_HALO, C_WIDTH), F32)

    x = x_ref[...].reshape(STREAMS * tm, D_MODEL)
    _project(x, w_row_ref, b_row_ref, w_col_ref, b_col_ref, proj_ref, colp_ref)
    x_new = _mix_tile(x, proj_ref, colp_ref, mnw_ref, gnw_ref, gnb_ref, ws_ref, bs_ref, cw_ref, cb_ref, cnw_ref,
                      cnb_ref, w_out_ref, l1w_ref, l1b_ref, hcat_ref, cstate_ref, mstate_ref, cbuf_ref, shift_ref,
                      tm)
    o_ref[...] = x_new.reshape(STREAMS, tm, D_MODEL)
    _route_tile(_router_logits(x_new, rw_ref).T[0:N_EXPERTS, :], rb_ref, metat_ref, tile_n_ref, tile_cnt_ref, cnt_ref)


def _full(shape):
    nd = len(shape)
    return pl.BlockSpec(shape, lambda i, _nd=nd: (0,) * _nd, pipeline_mode=pl.Buffered(1))


def _layer_of(stacked, layer):
    nd = stacked.ndim - 1
    return pl.BlockSpec((None,) + stacked.shape[1:], lambda i, _nd=nd: (layer,) + (0,) * _nd,
                        pipeline_mode=pl.Buffered(1))


def _mixer_layer(layer, x, w_in, w_out, b_row, b_col, mnw, gnw, gnb, ws_cat, bs_full, cw, cb, cnw, cnb,
                 l1w, l1b, rw_split, rb_col):
    batch, seq, _ = x.shape
    tm = MIX_TM
    steps_per_seq = seq // tm
    n_steps = batch // STREAMS * steps_per_seq
    rows = STREAMS * tm
    kernel = functools.partial(_mixer_kernel, tm=tm, steps_per_seq=steps_per_seq)
    weights = (b_row, b_col, mnw, gnw, gnb, ws_cat, bs_full, cw, cb, cnw, cnb, l1w, l1b, rw_split, rb_col)
    tile_spec = pl.BlockSpec((None, N_EXPERTS, LANES), lambda i: (i, 0, 0))
    step_spec = pl.BlockSpec((STREAMS, tm, D_MODEL), lambda i: (i // steps_per_seq, i % steps_per_seq, 0))
    return pl.pallas_call(
        kernel,
        out_shape=(
            jax.ShapeDtypeStruct(x.shape, F32),
            jax.ShapeDtypeStruct((n_steps, SUBLANES, rows), F32),
            jax.ShapeDtypeStruct((n_steps, N_EXPERTS, LANES), jnp.int32),
            jax.ShapeDtypeStruct((n_steps, N_EXPERTS, LANES), jnp.int32),
        ),
        grid=(n_steps,),
        in_specs=[step_spec, _layer_of(w_in, layer), _layer_of(w_out, layer)] + [_full(w.shape) for w in weights],
        out_specs=(step_spec, pl.BlockSpec((None, SUBLANES, rows), lambda i: (i, 0, 0)), tile_spec, tile_spec),
        scratch_shapes=[
            pltpu.VMEM((D_MODEL, P_ROW), BF16),
            pltpu.VMEM((P_COL_PAD, D_MODEL), BF16),
            pltpu.VMEM((D_MODEL, D_MODEL), BF16),
            pltpu.VMEM((rows, P_ROW), F32),
            pltpu.VMEM((P_COL_PAD, rows), F32),
            pltpu.VMEM((rows, D_MODEL), BF16),
            pltpu.VMEM((STREAMS * ML_HEADS, ML_HEAD_DIM, 2 * ML_HEAD_DIM), F32),
            pltpu.VMEM((STREAMS * SUBLANES, LANES), F32),
            pltpu.VMEM((STREAMS, tm + CONV_HALO, C_WIDTH), F32),
            pltpu.VMEM((STREAMS, SUBLANES - 1, tm + CONV_SPAN, C_WIDTH), F32),
            pltpu.VMEM((N_EXPERTS, LANES), F32),
        ],
        compiler_params=pltpu.CompilerParams(
            dimension_semantics=("arbitrary",), vmem_limit_bytes=VMEM_LIMIT),
        name="mixer",
    )(x, w_in, w_out, *weights)


def _pow2_pieces(n, largest, act):
    piece = largest
    while piece >= ROW_ALIGN:
        start = jnp.bitwise_and(n, -2 * piece)

        @pl.when(jnp.bitwise_and(n, piece) != 0)
        def _(piece=piece, start=start):
            act(start, piece)
        piece //= 2


def _for_each_run(n_tab, cnt_tab, base_tab, tile, make_copy, act):
    def body(e, off):
        n = n_tab[tile * N_EXPERTS + e]
        base = base_tab[e] + cnt_tab[tile * N_EXPERTS + e]
        _pow2_pieces(n, MOE_TM, lambda start, size: act(make_copy(
            pl.multiple_of(off + start, ROW_ALIGN), pl.multiple_of(base + start, ROW_ALIGN), size)))
        return off + n

    lax.fori_loop(0, N_EXPERTS, body, jnp.int32(0))


def _dispatch_kernel(n_tab, cnt_tab, base_tab, fill_tab, x_ref, metat_ref, xs_hbm, comp_ref, zero_ref, sem,
                     *, n_rows):
    i = pl.program_id(0)
    last = pl.num_programs(0) - 1
    slot = lax.rem(i, 2)

    def copies(tile, slot_, act):
        def make_copy(src_row, dst_row, rows):
            return pltpu.make_async_copy(comp_ref.at[slot_, pl.ds(src_row, rows)],
                                         xs_hbm.at[pl.ds(dst_row, rows)], sem.at[slot_])
        _for_each_run(n_tab, cnt_tab, base_tab, tile, make_copy, act)

    def zero_fill(act):
        def zero_copy(dst_row, rows):
            return pltpu.make_async_copy(zero_ref.at[pl.ds(0, rows)],
                                         xs_hbm.at[pl.ds(pl.multiple_of(dst_row, ROW_ALIGN), rows)], sem.at[2])

        def per_expert(e, carry):
            first = fill_tab[2 * e]
            _pow2_pieces(fill_tab[2 * e + 1], ROW_TILE // 2, lambda start, size: act(zero_copy(first + start, size)))
            return carry

        lax.fori_loop(0, N_EXPERTS, per_expert, jnp.int32(0))

        def per_half_tile(k, carry):
            act(zero_copy(fill_tab[2 * N_EXPERTS] + k * (ROW_TILE // 2), ROW_TILE // 2))
            return carry

        lax.fori_loop(0, (n_rows - fill_tab[2 * N_EXPERTS]) // (ROW_TILE // 2), per_half_tile, jnp.int32(0))

    @pl.when(i == 0)
    def _():
        zero_ref[...] = jnp.zeros_like(zero_ref)
        zero_fill(lambda cp: cp.start())

    @pl.when(i >= 2)
    def _():
        copies(i - 2, slot, lambda cp: cp.wait())

    mt = metat_ref[...]
    row_f = lax.broadcasted_iota(jnp.int32, (COMPACT_ROWS, MOE_TM), 0).astype(F32)
    hit1 = row_f == mt[0:1, :]
    hit2 = row_f == mt[1:2, :]
    onehot = jnp.where(hit1, 1.0, jnp.where(hit2, 1.0, 0.0)).astype(BF16)
    x_tile = x_ref[...].reshape(MOE_TM, D_MODEL).astype(BF16)
    comp_ref[slot, :, 0:D_MODEL] = _dot(onehot, x_tile).astype(BF16)
    gate = jnp.sum(jnp.where(hit1, mt[2:3, :], jnp.where(hit2, mt[3:4, :], 0.0)), axis=-1, keepdims=True)
    lane = lax.broadcasted_iota(jnp.int32, (COMPACT_ROWS, LANES), 1)
    packed, rest = jnp.zeros((COMPACT_ROWS, LANES), F32), gate
    for piece in range(GATE_PIECES):
        head = rest.astype(BF16).astype(F32)
        packed = jnp.where(lane == piece, head, packed)
        rest = rest - head
    comp_ref[slot, :, D_MODEL:PAYLOAD_W] = packed.astype(BF16)
    copies(i, slot, lambda cp: cp.start())

    @pl.when(i == last)
    def _():
        @pl.when(i >= 1)
        def _():
            copies(i - 1, 1 - slot, lambda cp: cp.wait())
        copies(i, slot, lambda cp: cp.wait())
        zero_fill(lambda cp: cp.wait())


def _token_step_spec(seq):
    tiles_per_seq = seq // MIX_TM
    return pl.BlockSpec((STREAMS, MIX_TM, D_MODEL), lambda i, *_: (i // tiles_per_seq, i % tiles_per_seq, 0))


_ROUTE_STEP_SPEC = pl.BlockSpec((None, SUBLANES, MOE_TM), lambda i, *_: (i, 0, 0))


def _dispatch_layer(n_tab, cnt_tab, base_tab, fill_tab, x, metat, n_rows):
    batch, seq, _ = x.shape
    return pl.pallas_call(
        functools.partial(_dispatch_kernel, n_rows=n_rows),
        out_shape=jax.ShapeDtypeStruct((n_rows, PAYLOAD_W), BF16),
        grid_spec=pltpu.PrefetchScalarGridSpec(
            num_scalar_prefetch=4,
            grid=(batch * seq // MOE_TM,),
            in_specs=[_token_step_spec(seq), _ROUTE_STEP_SPEC],
            out_specs=pl.BlockSpec(memory_space=pl.ANY),
            scratch_shapes=[
                pltpu.VMEM((2, COMPACT_ROWS, PAYLOAD_W), BF16),
                pltpu.VMEM((ROW_TILE // 2, PAYLOAD_W), BF16),
                pltpu.SemaphoreType.DMA((3,)),
            ],
        ),
        compiler_params=pltpu.CompilerParams(dimension_semantics=("arbitrary",), vmem_limit_bytes=VMEM_LIMIT),
        name="dispatch",
    )(n_tab, cnt_tab, base_tab, fill_tab, x, metat)


def _expert_kernel(tile_e, tile_valid, xs_ref, wg_ref, wu_ref, wd_ref, ys_ref, wgu_ref, wdb_ref):
    i = pl.program_id(0)
    valid = tile_valid[i]
    new_expert = jnp.logical_or(i == 0, tile_e[i] != tile_e[jnp.maximum(i - 1, 0)])

    @pl.when(jnp.logical_and(valid > 0, new_expert))
    def _():
        wgu_ref[:, 0:D_EXPERT] = wg_ref[...].astype(BF16)
        wgu_ref[:, D_EXPERT:2 * D_EXPERT] = wu_ref[...].astype(BF16)
        wdb_ref[...] = wd_ref[...].astype(BF16)

    @pl.when(valid > 0)
    def _():
        blk = ROW_TILE // EXPERT_ROW_BLOCKS
        rows = [slice(b * blk, (b + 1) * blk) for b in range(EXPERT_ROW_BLOCKS)]
        gu = [_dot(xs_ref[r, 0:D_MODEL], wgu_ref[...]) for r in rows]
        for r, gu_b in zip(rows, gu):
            g, u = gu_b[:, 0:D_EXPERT], gu_b[:, D_EXPERT:2 * D_EXPERT]
            hid = (g * jax.nn.sigmoid(g) * u).astype(BF16)
            gate = jnp.sum(xs_ref[r, D_MODEL:PAYLOAD_W].astype(F32), axis=-1, keepdims=True)
            ys_ref[r, :] = (gate * _dot(hid, wdb_ref[...])).astype(BF16)

    @pl.when(valid == 0)
    def _():
        ys_ref[...] = jnp.zeros_like(ys_ref)


def _expert_layer(layer, tile_e, tile_valid, xs, wg, wu, wd):
    n_steps = tile_e.shape[0]
    return pl.pallas_call(
        _expert_kernel,
        out_shape=jax.ShapeDtypeStruct((xs.shape[0], D_MODEL), BF16),
        grid_spec=pltpu.PrefetchScalarGridSpec(
            num_scalar_prefetch=2,
            grid=(n_steps,),
            in_specs=[
                pl.BlockSpec((ROW_TILE, PAYLOAD_W), lambda i, e, v: (i, 0)),
                pl.BlockSpec((None, None, D_MODEL, D_EXPERT), lambda i, e, v: (layer, e[i], 0, 0)),
                pl.BlockSpec((None, None, D_MODEL, D_EXPERT), lambda i, e, v: (layer, e[i], 0, 0)),
                pl.BlockSpec((None, None, D_EXPERT, D_MODEL), lambda i, e, v: (layer, e[i], 0, 0)),
            ],
            out_specs=pl.BlockSpec((ROW_TILE, D_MODEL), lambda i, e, v: (i, 0)),
            scratch_shapes=[
                pltpu.VMEM((D_MODEL, 2 * D_EXPERT), BF16),
                pltpu.VMEM((D_EXPERT, D_MODEL), BF16),
            ],
        ),
        compiler_params=pltpu.CompilerParams(dimension_semantics=("arbitrary",), vmem_limit_bytes=VMEM_LIMIT),
        name="experts",
    )(tile_e, tile_valid, xs, wg, wu, wd)


def _combine_kernel(n_tab, cnt_tab, base_tab, x_ref, metat_ref, ys_hbm, l2w_ref, l2b_ref, o_ref, yc_ref, sem):
    i = pl.program_id(0)
    n_steps = pl.num_programs(0)
    slot = lax.rem(i, 2)

    def copies(tile, slot_, act):
        def make_copy(buf_row, ys_row, rows):
            return pltpu.make_async_copy(ys_hbm.at[pl.ds(ys_row, rows)],
                                         yc_ref.at[slot_, pl.ds(buf_row, rows)], sem.at[slot_])
        _for_each_run(n_tab, cnt_tab, base_tab, tile, make_copy, act)

    @pl.when(i == 0)
    def _():
        yc_ref[...] = jnp.zeros_like(yc_ref)
        copies(0, 0, lambda cp: cp.start())

    @pl.when(i + 1 < n_steps)
    def _():
        copies(i + 1, 1 - slot, lambda cp: cp.start())

    copies(i, slot, lambda cp: cp.wait())

    t_r = lax.broadcasted_iota(jnp.int32, (MOE_TM, MOE_TM), 0)
    t_c = lax.broadcasted_iota(jnp.int32, (MOE_TM, MOE_TM), 1)
    as_col = lambda row: jnp.sum(jnp.where(t_r == t_c, row, 0.0), axis=-1, keepdims=True)
    col_f = lax.broadcasted_iota(jnp.int32, (MOE_TM, COMPACT_ROWS), 1).astype(F32)
    mt = metat_ref[...]
    onehot = jnp.where(col_f == as_col(mt[0:1, :]), 1.0,
                       jnp.where(col_f == as_col(mt[1:2, :]), 1.0, 0.0)).astype(BF16)
    y = _dot(onehot, yc_ref[slot])
    x = x_ref[...].reshape(MOE_TM, D_MODEL)
    o_ref[...] = _layer_norm(DEEPNORM_ALPHA * x + y, l2w_ref[...], l2b_ref[...], LN_EPS).reshape(o_ref.shape)


def _combine_layer(n_tab, cnt_tab, base_tab, x, metat, ys, l2w, l2b):
    batch, seq, _ = x.shape
    return pl.pallas_call(
        _combine_kernel,
        out_shape=jax.ShapeDtypeStruct(x.shape, F32),
        grid_spec=pltpu.PrefetchScalarGridSpec(
            num_scalar_prefetch=3,
            grid=(batch * seq // MOE_TM,),
            in_specs=[
                _token_step_spec(seq),
                _ROUTE_STEP_SPEC,
                pl.BlockSpec(memory_space=pl.ANY),
                pl.BlockSpec((1, D_MODEL), lambda i, *_: (0, 0)),
                pl.BlockSpec((1, D_MODEL), lambda i, *_: (0, 0)),
            ],
            out_specs=_token_step_spec(seq),
            scratch_shapes=[pltpu.VMEM((2, COMPACT_ROWS, D_MODEL), BF16), pltpu.SemaphoreType.DMA((2,))],
        ),
        compiler_params=pltpu.CompilerParams(dimension_semantics=("arbitrary",), vmem_limit_bytes=VMEM_LIMIT),
        name="combine",
    )(n_tab, cnt_tab, base_tab, x, metat, ys, l2w, l2b)


def _expert_plan(counts, n_steps):
    tiles_e = (counts + (ROW_TILE - 1)) // ROW_TILE
    cum = jnp.cumsum(tiles_e)
    first_tile = cum - tiles_e
    total = cum[-1]
    step = jnp.arange(n_steps, dtype=jnp.int32)
    owner = (step[:, None] >= cum[None, :]).sum(axis=1).astype(jnp.int32)
    is_owner = owner[:, None] == jnp.arange(N_EXPERTS, dtype=jnp.int32)[None, :]
    local = step - jnp.where(is_owner, first_tile[None, :], 0).sum(axis=1)
    rows_left = jnp.where(is_owner, counts[None, :], 0).sum(axis=1) - local * ROW_TILE
    valid = jnp.where(step < total, jnp.clip(rows_left, 0, ROW_TILE), 0).astype(jnp.int32)
    tile_e = jnp.minimum(owner, N_EXPERTS - 1)
    base = (first_tile * ROW_TILE).astype(jnp.int32)
    fill = jnp.stack([base + counts, tiles_e * ROW_TILE - counts], axis=1).reshape(-1)
    fill_tab = jnp.concatenate([fill, (total * ROW_TILE)[None]]).astype(jnp.int32)
    return base, tile_e, valid, fill_tab


def _moe_layer(layer, x, metat, tile_n, tile_cnt, wg, wu, wd, l2w, l2b):
    n_tok = x.shape[0] * x.shape[1]
    n_tiles = n_tok // MOE_TM
    n_steps = (2 * n_tok + N_EXPERTS * (ROW_ALIGN - 1) * n_tiles) // ROW_TILE + N_EXPERTS
    n_tab = tile_n[:, :, 0].reshape(-1)
    cnt_tab = tile_cnt[:, :, 0].reshape(-1)
    counts = tile_cnt[-1, :, 0] + tile_n[-1, :, 0]
    base_tab, tile_e, tile_valid, fill_tab = _expert_plan(counts, n_steps)
    xs = _dispatch_layer(n_tab, cnt_tab, base_tab, fill_tab, x, metat, n_steps * ROW_TILE)
    ys = _expert_layer(layer, tile_e, tile_valid, xs, wg, wu, wd)
    return _combine_layer(n_tab, cnt_tab, base_tab, x, metat, ys, l2w, l2b)


def kernel(x, w_in, b_in, mlstm_norm_w, gmlp_norm_w, gmlp_norm_b, gmlp_ws, gmlp_bs, conv_w, conv_b,
           conv_norm_w, conv_norm_b, w_out, ln1_w, ln1_b, router_w, router_b, w_gate, w_up, w_down,
           ln2_w, ln2_b):
    batch, seq, d = x.shape

    q_lo, k_lo, v_lo = 0, ML_WIDTH, 2 * ML_WIDTH
    gate_lo = 4 * ML_WIDTH
    rest_lo = gate_lo + 2 * ML_HEADS

    def row_part(t):
        return jnp.concatenate([t[..., q_lo:k_lo], t[..., v_lo:gate_lo], t[..., rest_lo:]], axis=-1)

    def col_part(t):
        return jnp.concatenate([t[..., k_lo:v_lo], t[..., gate_lo:rest_lo]], axis=-1)

    tril = jnp.tril(jnp.ones((CHUNK, CHUNK), gmlp_ws.dtype))
    rw_pad = jnp.pad(router_w.astype(F32), ((0, 0), (0, LANES - N_EXPERTS)))
    rw_head = rw_pad.astype(BF16)
    rw_split = jnp.concatenate([rw_head, (rw_pad - rw_head.astype(F32)).astype(BF16)], axis=1)
    rb_col = router_b.astype(F32).reshape(N_EXPERTS, 1)

    for l in range(DEPTH):
        b_row = row_part(b_in[l]).reshape(1, P_ROW)
        b_col = jnp.pad(col_part(b_in[l]), (0, P_COL_PAD - P_COL)).reshape(P_COL_PAD, 1)
        ws_cat = jnp.transpose(gmlp_ws[l] * tril, (1, 0, 2)).reshape(CHUNK, G_HEADS * CHUNK).astype(BF16)
        bs_full = jnp.repeat(gmlp_bs[l].T, G_HEAD_DIM, axis=1)
        cw = jnp.pad(conv_w[l], ((0, 1), (0, 0)))
        x, metat, tile_n, tile_cnt = _mixer_layer(
            l, x, w_in, w_out, b_row, b_col, mlstm_norm_w[l].reshape(1, -1), gmlp_norm_w[l].reshape(1, -1),
            gmlp_norm_b[l].reshape(1, -1), ws_cat, bs_full, cw, conv_b[l].reshape(1, -1),
            conv_norm_w[l].reshape(1, -1), conv_norm_b[l].reshape(1, -1),
            ln1_w[l].reshape(1, -1), ln1_b[l].reshape(1, -1), rw_split, rb_col)
        x = _moe_layer(l, x, metat, tile_n, tile_cnt, w_gate, w_up, w_down,
                       ln2_w[l].reshape(1, -1), ln2_b[l].reshape(1, -1))
    return x
```

```python
import functools

import jax
import jax.numpy as jnp
from jax import lax
from jax.experimental import pallas as pl
from jax.experimental.pallas import tpu as pltpu

D_MODEL = 1024
DEPTH = 4
ML_WIDTH = 512
ML_HEADS = 4
ML_HEAD_DIM = 128
CHUNK = 128
G_WIDTH = 256
G_HEADS = 4
G_HEAD_DIM = 64
C_WIDTH = 256
CONV_WIDTH = 31
N_EXPERTS = 16
N_GROUPS = 4
EXPERTS_PER_GROUP = 4
D_EXPERT = 512
DEEPNORM_ALPHA = (2.0 * DEPTH) ** 0.25
LN_EPS = 1e-5
K_SCALE = ML_HEAD_DIM ** -0.5

Q_LO, V_LO, O_LO, GU_LO, GV_LO, CA_LO, CB_LO = 0, 512, 1024, 1536, 1792, 2048, 2304
P_ROW = 2560
P_COL = ML_WIDTH + 2 * ML_HEADS
P_COL_PAD = ML_WIDTH + 16
IN_Q, IN_K, IN_VO, IN_GATES, IN_REST = (0, 512), (512, 1024), (1024, 2048), (2048, 2056), (2056, 3080)
P_IN = 3080
W_PREP_ROWS = 256

LANES = 128
SUBLANES = 8
MIX_TM = 256
ROW_ALIGN = 16
ROW_TILE = 512
EXPERT_ROW_BLOCKS = 2
PAYLOAD_W = D_MODEL + LANES
CONV_HALO = 32
CONV_SPAN = CONV_HALO - SUBLANES
STREAMS = 2
MOE_TM = STREAMS * MIX_TM
COMPACT_ROWS = 2 * MOE_TM + 2 * LANES
assert COMPACT_ROWS >= 2 * MOE_TM + N_EXPERTS * (ROW_ALIGN - 1)
GATE_PIECES = 3
VMEM_LIMIT = 58 * 1024 * 1024

F32 = jnp.float32
BF16 = jnp.bfloat16
NEG_INF = float("-inf")


def _layer_norm(x, w, b, eps):
    mu = jnp.mean(x, axis=-1, keepdims=True)
    xc = x - mu
    var = jnp.mean(xc * xc, axis=-1, keepdims=True)
    return xc * lax.rsqrt(var + eps) * w + b


def _gelu_tanh(x):
    return 0.5 * x * (1.0 + jnp.tanh(0.7978845608028654 * (x + 0.044715 * (x * x * x))))


def _log_sigmoid(x):
    return jnp.minimum(x, 0.0) - jnp.log1p(jnp.exp(-jnp.abs(x)))


def _dot(a, b):
    return jnp.dot(a, b, preferred_element_type=F32)


def _dot_nt(a, b):
    return lax.dot_general(a, b, (((1,), (1,)), ((), ())), preferred_element_type=F32)


def _router_logits(x_new, rw_ref):
    xh = x_new.astype(BF16)
    xl = (x_new - xh.astype(F32)).astype(BF16)
    head = _dot(xh, rw_ref[...])
    return head[:, 0:LANES] + head[:, LANES:2 * LANES] + _dot(xl, rw_ref[:, 0:LANES])


def _route_tile(logits, rb_ref, metat_ref, tile_n_ref, tile_cnt_ref, cnt_ref):
    tm = logits.shape[1]
    s_all = jax.nn.sigmoid(logits)
    sel_all = s_all + rb_ref[...]
    rows_of = lambda v: [v[k:k + 1, :] for k in range(N_EXPERTS)]
    s, sel = rows_of(s_all), rows_of(sel_all)

    best = None
    for g in range(N_GROUPS):
        r = sel[EXPERTS_PER_GROUP * g:EXPERTS_PER_GROUP * (g + 1)]
        hi01, lo01 = jnp.maximum(r[0], r[1]), jnp.minimum(r[0], r[1])
        hi23, lo23 = jnp.maximum(r[2], r[3]), jnp.minimum(r[2], r[3])
        score = jnp.maximum(hi01, hi23) + jnp.maximum(jnp.minimum(hi01, hi23), jnp.maximum(lo01, lo23))
        if best is None:
            best, gidx = score, jnp.zeros(score.shape, jnp.int32)
        else:
            better = score > best
            gidx = jnp.where(better, g, gidx)
            best = jnp.where(better, score, best)

    def of_group(rows):
        out = []
        for j in range(EXPERTS_PER_GROUP):
            v = rows[j]
            for g in range(1, N_GROUPS):
                v = jnp.where(gidx == g, rows[EXPERTS_PER_GROUP * g + j], v)
            out.append(v)
        return out

    def first_max(vals):
        best_v, best_j = vals[0], jnp.zeros(vals[0].shape, F32)
        for j in range(1, len(vals)):
            better = vals[j] > best_v
            best_j = jnp.where(better, float(j), best_j)
            best_v = jnp.where(better, vals[j], best_v)
        return best_j

    def take(vals, idx):
        v = vals[0]
        for j in range(1, len(vals)):
            v = jnp.where(idx == float(j), vals[j], v)
        return v

    cand, cand_s = of_group(sel), of_group(s)
    j1 = first_max(cand)
    j2 = first_max([jnp.where(j1 == float(j), NEG_INF, cand[j]) for j in range(EXPERTS_PER_GROUP)])
    g1, g2 = take(cand_s, j1), take(cand_s, j2)
    tot = g1 + g2
    first_of_group = gidx.astype(F32) * float(EXPERTS_PER_GROUP)
    e1, e2 = first_of_group + j1, first_of_group + j2

    e_f = lax.broadcasted_iota(jnp.int32, (N_EXPERTS, tm), 0).astype(F32)
    assign = jnp.where(e_f == e1, 1.0, jnp.where(e_f == e2, 1.0, 0.0)).astype(BF16)
    t_r = lax.broadcasted_iota(jnp.int32, (tm, tm), 0)
    t_c = lax.broadcasted_iota(jnp.int32, (tm, tm), 1)
    earlier = jnp.where(t_r < t_c, 1.0, 0.0).astype(BF16)
    rank = _dot(assign, earlier)
    n_b = _dot(assign, jnp.ones((tm, LANES), BF16))
    n_up_b = jnp.floor((n_b + (ROW_ALIGN - 1.0)) * (1.0 / ROW_ALIGN)) * ROW_ALIGN
    x_r = lax.broadcasted_iota(jnp.int32, (N_EXPERTS, N_EXPERTS), 0)
    x_c = lax.broadcasted_iota(jnp.int32, (N_EXPERTS, N_EXPERTS), 1)
    lower = jnp.where(x_c < x_r, 1.0, 0.0).astype(BF16)
    off_b = _dot(lower, n_up_b.astype(BF16))
    pos = rows_of(jnp.concatenate([off_b] * (tm // LANES), axis=1) + rank)
    metat_ref[...] = jnp.concatenate(
        [take(pos, e1), take(pos, e2), g1 / tot, g2 / tot, jnp.zeros((SUBLANES - 4, tm), F32)], axis=0)

    cnt = cnt_ref[...]
    tile_n_ref[...] = n_up_b.astype(jnp.int32)
    tile_cnt_ref[...] = cnt.astype(jnp.int32)
    cnt_ref[...] = cnt + n_up_b


def _project(x_tile, w_row_ref, b_row_ref, w_col_ref, b_col_ref, proj_ref, colp_ref):
    xb = x_tile.astype(BF16)
    proj_ref[...] = _dot(xb, w_row_ref[...]) + b_row_ref[...]
    colp_ref[...] = _dot_nt(w_col_ref[...], xb) + b_col_ref[...]


def _mix_tile(x, proj_ref, colp_ref, mnw_ref, gnw_ref, gnb_ref, ws_ref, bs_ref, cw_ref, cb_ref, cnw_ref,
              cnb_ref, w_out_ref, l1w_ref, l1b_ref, hcat_ref, cstate_ref, mstate_ref, cbuf_ref, shift_ref, tm):
    n_chunks = tm // CHUNK
    gates = colp_ref[ML_WIDTH:P_COL, :]
    lane_in_chunk = lax.broadcasted_iota(jnp.int32, gates.shape, 1) % CHUNK
    logf_all = _log_sigmoid(gates)
    bcum = logf_all
    d = 1
    while d < CHUNK:
        bcum = bcum + jnp.where(lane_in_chunk >= d, pltpu.roll(bcum, d, 1), 0.0)
        d *= 2

    row_i = lax.broadcasted_iota(jnp.int32, (CHUNK, CHUNK), 0)
    col_i = lax.broadcasted_iota(jnp.int32, (CHUNK, CHUNK), 1)
    causal = col_i <= row_i
    diag = col_i == row_i
    ones_col = jnp.where(col_i == 0, 1.0, 0.0).astype(BF16)

    for sc in range(STREAMS * n_chunks):
        st, c = divmod(sc, n_chunks)
        r0 = st * tm + c * CHUNK
        rows = slice(r0, r0 + CHUNK)
        m_rows = slice(st * SUBLANES, st * SUBLANES + ML_HEADS)
        logi = gates[0:ML_HEADS, r0:r0 + CHUNK]
        logf = logf_all[ML_HEADS:2 * ML_HEADS, r0:r0 + CHUNK]
        b_row = bcum[ML_HEADS:2 * ML_HEADS, r0:r0 + CHUNK]
        g = jnp.sum(logf, axis=-1, keepdims=True)
        m_prev = mstate_ref[m_rows, 0:1]
        a_row = g - b_row + logi
        m_new = jnp.maximum(g + m_prev, jnp.max(a_row, axis=-1, keepdims=True))
        w_row = jnp.exp(a_row - m_new) * K_SCALE
        decay = jnp.exp(g + m_prev - m_new)
        mstate_ref[m_rows, :] = jnp.broadcast_to(m_new, (ML_HEADS, LANES))

        for h in range(ML_HEADS):
            hs = slice(h * ML_HEAD_DIM, (h + 1) * ML_HEAD_DIM)
            q = proj_ref[rows, Q_LO + h * ML_HEAD_DIM:Q_LO + (h + 1) * ML_HEAD_DIM].astype(BF16)
            v = proj_ref[rows, V_LO + h * ML_HEAD_DIM:V_LO + (h + 1) * ML_HEAD_DIM].astype(BF16)
            o_gate = proj_ref[rows, O_LO + h * ML_HEAD_DIM:O_LO + (h + 1) * ML_HEAD_DIM]
            kt = colp_ref[hs, r0:r0 + CHUNK]
            v_ext = jnp.concatenate([v, ones_col], axis=1)

            b_r = b_row[h:h + 1, :]
            b_c = jnp.sum(jnp.where(diag, b_r, 0.0), axis=-1, keepdims=True)
            log_d = jnp.where(causal, b_c - b_r + logi[h:h + 1, :], NEG_INF)
            mp = m_prev[h:h + 1, :]
            log_inter = b_c + mp
            m_row = jnp.maximum(log_inter, jnp.max(log_d, axis=-1, keepdims=True))
            c_ext = cstate_ref[st * ML_HEADS + h]
            from_q = _dot(q, jnp.concatenate([kt.astype(BF16), c_ext.astype(BF16)], axis=1))
            p = from_q[:, 0:CHUNK] * (jnp.exp(log_d - m_row) * K_SCALE)
            w_inter = jnp.exp(log_inter - m_row)
            ktw = (kt * w_row[h:h + 1, :]).astype(BF16)
            onto_v = _dot(jnp.concatenate([p.astype(BF16), ktw], axis=0), v_ext)
            tot = onto_v[0:CHUNK, :] + w_inter * from_q[:, CHUNK:]
            den = jnp.maximum(jnp.abs(tot[:, ML_HEAD_DIM:ML_HEAD_DIM + 1]), jnp.exp(-m_row))
            hh = tot[:, 0:ML_HEAD_DIM] / den
            cstate_ref[st * ML_HEADS + h] = decay[h:h + 1, :] * c_ext + onto_v[CHUNK:, :]

            mu = jnp.mean(hh, axis=-1, keepdims=True)
            hc = hh - mu
            var = jnp.mean(hc * hc, axis=-1, keepdims=True)
            hn = hc * lax.rsqrt(var + 1e-6) * mnw_ref[:, hs]
            hcat_ref[rows, hs] = (jax.nn.sigmoid(o_gate) * hn).astype(BF16)

        u = _gelu_tanh(proj_ref[rows, GU_LO:GU_LO + G_WIDTH])
        z = _layer_norm(_gelu_tanh(proj_ref[rows, GV_LO:GV_LO + G_WIDTH]), gnw_ref[...], gnb_ref[...], LN_EPS)
        lane_head = lax.broadcasted_iota(jnp.int32, (CHUNK, G_WIDTH), 1) // G_HEAD_DIM
        z_bd = jnp.concatenate(
            [jnp.where(lane_head == h, z, 0.0).astype(BF16) for h in range(G_HEADS)], axis=0)
        zs = _dot(ws_ref[...], z_bd) + bs_ref[...]
        hcat_ref[rows, ML_WIDTH:ML_WIDTH + G_WIDTH] = (u * zs).astype(BF16)

        ca = proj_ref[rows, CA_LO:CA_LO + C_WIDTH]
        cb = proj_ref[rows, CB_LO:CB_LO + C_WIDTH]
        cbuf_ref[st, CONV_HALO + c * CHUNK:CONV_HALO + (c + 1) * CHUNK, :] = ca * jax.nn.sigmoid(cb)

    first_tap = CONV_HALO - (CONV_WIDTH - 1)
    for st in range(STREAMS):
        for sh in range(1, SUBLANES):
            shift_ref[st, sh - 1] = cbuf_ref[st, sh:sh + tm + CONV_SPAN, :]
        for c in range(n_chunks):
            acc = jnp.zeros((CHUNK, C_WIDTH), F32) + cb_ref[...]
            for k in range(CONV_WIDTH):
                whole, sh = divmod(first_tap + k, SUBLANES)
                lo = c * CHUNK + whole * SUBLANES
                tap = cbuf_ref[st, lo:lo + CHUNK, :] if sh == 0 else shift_ref[st, sh - 1, lo:lo + CHUNK, :]
                acc = acc + cw_ref[k:k + 1, :] * tap
            cn = _layer_norm(acc, cnw_ref[...], cnb_ref[...], LN_EPS)
            r0 = st * tm + c * CHUNK
            hcat_ref[r0:r0 + CHUNK, ML_WIDTH + G_WIDTH:D_MODEL] = (cn * jax.nn.sigmoid(cn)).astype(BF16)
        cbuf_ref[st, 0:CONV_HALO, :] = cbuf_ref[st, tm:tm + CONV_HALO, :]

    y = _dot(hcat_ref[...], w_out_ref[...])
    x_new = _layer_norm(DEEPNORM_ALPHA * x + y, l1w_ref[...], l1b_ref[...], LN_EPS)
    return x_new


def _prepare_in_proj(w_in_ref, w_row_ref, w_col_ref):
    for r0 in range(0, D_MODEL, W_PREP_ROWS):
        rs = slice(r0, r0 + W_PREP_ROWS)
        w_row_ref[rs, Q_LO:V_LO] = w_in_ref[rs, IN_Q[0]:IN_Q[1]].astype(BF16)
        w_row_ref[rs, V_LO:GU_LO] = w_in_ref[rs, IN_VO[0]:IN_VO[1]].astype(BF16)
        w_row_ref[rs, GU_LO:P_ROW] = w_in_ref[rs, IN_GATES[0]:P_IN][:, IN_REST[0] - IN_GATES[0]:].astype(BF16)
        w_col_ref[0:ML_WIDTH, rs] = w_in_ref[rs, IN_K[0]:IN_K[1]].T.astype(BF16)
        gates_t = w_in_ref[rs, IN_GATES[0]:IN_GATES[0] + LANES].T
        keep = lax.broadcasted_iota(jnp.int32, (P_COL_PAD - ML_WIDTH, W_PREP_ROWS), 0) < 2 * ML_HEADS
        w_col_ref[ML_WIDTH:P_COL_PAD, rs] = jnp.where(keep, gates_t[0:P_COL_PAD - ML_WIDTH, :], 0.0).astype(BF16)


def _mixer_kernel(x_ref, w_in_ref, w_out_f32_ref, b_row_ref, b_col_ref, mnw_ref, gnw_ref, gnb_ref,
                  ws_ref, bs_ref, cw_ref, cb_ref, cnw_ref, cnb_ref, l1w_ref, l1b_ref, rw_ref, rb_ref,
                  o_ref, metat_ref, tile_n_ref, tile_cnt_ref,
                  w_row_ref, w_col_ref, w_out_ref, proj_ref, colp_ref, hcat_ref, cstate_ref, mstate_ref, cbuf_ref, shift_ref,
                  cnt_ref, *, tm, steps_per_seq):
    i = pl.program_id(0)

    @pl.when(i == 0)
    def _():
        cnt_ref[...] = jnp.zeros_like(cnt_ref)
        _prepare_in_proj(w_in_ref, w_row_ref, w_col_ref)
        for r0 in range(0, D_MODEL, W_PREP_ROWS):
            w_out_ref[r0:r0 + W_PREP_ROWS, :] = w_out_f32_ref[r0:r0 + W_PREP_ROWS, :].astype(BF16)

    @pl.when(i % steps_per_seq == 0)
    def _():
        cstate_ref[...] = jnp.zeros_like(cstate_ref)
        mstate_ref[...] = jnp.zeros_like(mstate_ref)
        cbuf_ref[:, 0:CONV_HALO, :] = jnp.zeros((STREAMS, CONV_HALO, C_WIDTH), F32)

    x = x_ref[...].reshape(STREAMS * tm, D_MODEL)
    _project(x, w_row_ref, b_row_ref, w_col_ref, b_col_ref, proj_ref, colp_ref)
    x_new = _mix_tile(x, proj_ref, colp_ref, mnw_ref, gnw_ref, gnb_ref, ws_ref, bs_ref, cw_ref, cb_ref, cnw_ref,
                      cnb_ref, w_out_ref, l1w_ref, l1b_ref, hcat_ref, cstate_ref, mstate_ref, cbuf_ref, shift_ref,
                      tm)
    o_ref[...] = x_new.reshape(STREAMS, tm, D_MODEL)
    _route_tile(_router_logits(x_new, rw_ref).T[0:N_EXPERTS, :], rb_ref, metat_ref, tile_n_ref, tile_cnt_ref, cnt_ref)


def _full(shape):
    nd = len(shape)
    return pl.BlockSpec(shape, lambda i, _nd=nd: (0,) * _nd, pipeline_mode=pl.Buffered(1))


def _layer_of(stacked, layer):
    nd = stacked.ndim - 1
    return pl.BlockSpec((None,) + stacked.shape[1:], lambda i, _nd=nd: (layer,) + (0,) * _nd,
                        pipeline_mode=pl.Buffered(1))


def _mixer_layer(layer, x, w_in, w_out, b_row, b_col, mnw, gnw, gnb, ws_cat, bs_full, cw, cb, cnw, cnb,
                 l1w, l1b, rw_split, rb_col):
    batch, seq, _ = x.shape
    tm = MIX_TM
    steps_per_seq = seq // tm
    n_steps = batch // STREAMS * steps_per_seq
    rows = STREAMS * tm
    kernel = functools.partial(_mixer_kernel, tm=tm, steps_per_seq=steps_per_seq)
    weights = (b_row, b_col, mnw, gnw, gnb, ws_cat, bs_full, cw, cb, cnw, cnb, l1w, l1b, rw_split, rb_col)
    tile_spec = pl.BlockSpec((None, N_EXPERTS, LANES), lambda i: (i, 0, 0))
    step_spec = pl.BlockSpec((STREAMS, tm, D_MODEL), lambda i: (i // steps_per_seq, i % steps_per_seq, 0))
    return pl.pallas_call(
        kernel,
        out_shape=(
            jax.ShapeDtypeStruct(x.shape, F32),
            jax.ShapeDtypeStruct((n_steps, SUBLANES, rows), F32),
            jax.ShapeDtypeStruct((n_steps, N_EXPERTS, LANES), jnp.int32),
            jax.ShapeDtypeStruct((n_steps, N_EXPERTS, LANES), jnp.int32),
        ),
        grid=(n_steps,),
        in_specs=[step_spec, _layer_of(w_in, layer), _layer_of(w_out, layer)] + [_full(w.shape) for w in weights],
        out_specs=(step_spec, pl.BlockSpec((None, SUBLANES, rows), lambda i: (i, 0, 0)), tile_spec, tile_spec),
        scratch_shapes=[
            pltpu.VMEM((D_MODEL, P_ROW), BF16),
            pltpu.VMEM((P_COL_PAD, D_MODEL), BF16),
            pltpu.VMEM((D_MODEL, D_MODEL), BF16),
            pltpu.VMEM((rows, P_ROW), F32),
            pltpu.VMEM((P_COL_PAD, rows), F32),
            pltpu.VMEM((rows, D_MODEL), BF16),
            pltpu.VMEM((STREAMS * ML_HEADS, ML_HEAD_DIM, 2 * ML_HEAD_DIM), F32),
            pltpu.VMEM((STREAMS * SUBLANES, LANES), F32),
            pltpu.VMEM((STREAMS, tm + CONV_HALO, C_WIDTH), F32),
            pltpu.VMEM((STREAMS, SUBLANES - 1, tm + CONV_SPAN, C_WIDTH), F32),
            pltpu.VMEM((N_EXPERTS, LANES), F32),
        ],
        compiler_params=pltpu.CompilerParams(
            dimension_semantics=("arbitrary",), vmem_limit_bytes=VMEM_LIMIT),
        name="mixer",
    )(x, w_in, w_out, *weights)


def _pow2_pieces(n, largest, act):
    piece = largest
    while piece >= ROW_ALIGN:
        start = jnp.bitwise_and(n, -2 * piece)

        @pl.when(jnp.bitwise_and(n, piece) != 0)
        def _(piece=piece, start=start):
            act(start, piece)
        piece //= 2


def _for_each_run(n_tab, cnt_tab, base_tab, tile, make_copy, act):
    def body(e, off):
        n = n_tab[tile * N_EXPERTS + e]
        base = base_tab[e] + cnt_tab[tile * N_EXPERTS + e]
        _pow2_pieces(n, MOE_TM, lambda start, size: act(make_copy(
            pl.multiple_of(off + start, ROW_ALIGN), pl.multiple_of(base + start, ROW_ALIGN), size)))
        return off + n

    lax.fori_loop(0, N_EXPERTS, body, jnp.int32(0))


def _dispatch_kernel(n_tab, cnt_tab, base_tab, fill_tab, x_ref, metat_ref, xs_hbm, comp_ref, zero_ref, sem,
                     *, n_rows):
    i = pl.program_id(0)
    last = pl.num_programs(0) - 1
    slot = lax.rem(i, 2)

    def copies(tile, slot_, act):
        def make_copy(src_row, dst_row, rows):
            return pltpu.make_async_copy(comp_ref.at[slot_, pl.ds(src_row, rows)],
                                         xs_hbm.at[pl.ds(dst_row, rows)], sem.at[slot_])
        _for_each_run(n_tab, cnt_tab, base_tab, tile, make_copy, act)

    def zero_fill(act):
        def zero_copy(dst_row, rows):
            return pltpu.make_async_copy(zero_ref.at[pl.ds(0, rows)],
                                         xs_hbm.at[pl.ds(pl.multiple_of(dst_row, ROW_ALIGN), rows)], sem.at[2])

        def per_expert(e, carry):
            first = fill_tab[2 * e]
            _pow2_pieces(fill_tab[2 * e + 1], ROW_TILE // 2, lambda start, size: act(zero_copy(first + start, size)))
            return carry

        lax.fori_loop(0, N_EXPERTS, per_expert, jnp.int32(0))

        def per_half_tile(k, carry):
            act(zero_copy(fill_tab[2 * N_EXPERTS] + k * (ROW_TILE // 2), ROW_TILE // 2))
            return carry

        lax.fori_loop(0, (n_rows - fill_tab[2 * N_EXPERTS]) // (ROW_TILE // 2), per_half_tile, jnp.int32(0))

    @pl.when(i == 0)
    def _():
        zero_ref[...] = jnp.zeros_like(zero_ref)
        zero_fill(lambda cp: cp.start())

    @pl.when(i >= 2)
    def _():
        copies(i - 2, slot, lambda cp: cp.wait())

    mt = metat_ref[...]
    row_f = lax.broadcasted_iota(jnp.int32, (COMPACT_ROWS, MOE_TM), 0).astype(F32)
    hit1 = row_f == mt[0:1, :]
    hit2 = row_f == mt[1:2, :]
    onehot = jnp.where(hit1, 1.0, jnp.where(hit2, 1.0, 0.0)).astype(BF16)
    x_tile = x_ref[...].reshape(MOE_TM, D_MODEL).astype(BF16)
    comp_ref[slot, :, 0:D_MODEL] = _dot(onehot, x_tile).astype(BF16)
    gate = jnp.sum(jnp.where(hit1, mt[2:3, :], jnp.where(hit2, mt[3:4, :], 0.0)), axis=-1, keepdims=True)
    lane = lax.broadcasted_iota(jnp.int32, (COMPACT_ROWS, LANES), 1)
    packed, rest = jnp.zeros((COMPACT_ROWS, LANES), F32), gate
    for piece in range(GATE_PIECES):
        head = rest.astype(BF16).astype(F32)
        packed = jnp.where(lane == piece, head, packed)
        rest = rest - head
    comp_ref[slot, :, D_MODEL:PAYLOAD_W] = packed.astype(BF16)
    copies(i, slot, lambda cp: cp.start())

    @pl.when(i == last)
    def _():
        @pl.when(i >= 1)
        def _():
            copies(i - 1, 1 - slot, lambda cp: cp.wait())
        copies(i, slot, lambda cp: cp.wait())
        zero_fill(lambda cp: cp.wait())


def _token_step_spec(seq):
    tiles_per_seq = seq // MIX_TM
    return pl.BlockSpec((STREAMS, MIX_TM, D_MODEL), lambda i, *_: (i // tiles_per_seq, i % tiles_per_seq, 0))


_ROUTE_STEP_SPEC = pl.BlockSpec((None, SUBLANES, MOE_TM), lambda i, *_: (i, 0, 0))


def _dispatch_layer(n_tab, cnt_tab, base_tab, fill_tab, x, metat, n_rows):
    batch, seq, _ = x.shape
    return pl.pallas_call(
        functools.partial(_dispatch_kernel, n_rows=n_rows),
        out_shape=jax.ShapeDtypeStruct((n_rows, PAYLOAD_W), BF16),
        grid_spec=pltpu.PrefetchScalarGridSpec(
            num_scalar_prefetch=4,
            grid=(batch * seq // MOE_TM,),
            in_specs=[_token_step_spec(seq), _ROUTE_STEP_SPEC],
            out_specs=pl.BlockSpec(memory_space=pl.ANY),
            scratch_shapes=[
                pltpu.VMEM((2, COMPACT_ROWS, PAYLOAD_W), BF16),
                pltpu.VMEM((ROW_TILE // 2, PAYLOAD_W), BF16),
                pltpu.SemaphoreType.DMA((3,)),
            ],
        ),
        compiler_params=pltpu.CompilerParams(dimension_semantics=("arbitrary",), vmem_limit_bytes=VMEM_LIMIT),
        name="dispatch",
    )(n_tab, cnt_tab, base_tab, fill_tab, x, metat)


def _expert_kernel(tile_e, tile_valid, xs_ref, wg_ref, wu_ref, wd_ref, ys_ref, wgu_ref, wdb_ref):
    i = pl.program_id(0)
    valid = tile_valid[i]
    new_expert = jnp.logical_or(i == 0, tile_e[i] != tile_e[jnp.maximum(i - 1, 0)])

    @pl.when(jnp.logical_and(valid > 0, new_expert))
    def _():
        wgu_ref[:, 0:D_EXPERT] = wg_ref[...].astype(BF16)
        wgu_ref[:, D_EXPERT:2 * D_EXPERT] = wu_ref[...].astype(BF16)
        wdb_ref[...] = wd_ref[...].astype(BF16)

    @pl.when(valid > 0)
    def _():
        blk = ROW_TILE // EXPERT_ROW_BLOCKS
        rows = [slice(b * blk, (b + 1) * blk) for b in range(EXPERT_ROW_BLOCKS)]
        gu = [_dot(xs_ref[r, 0:D_MODEL], wgu_ref[...]) for r in rows]
        for r, gu_b in zip(rows, gu):
            g, u = gu_b[:, 0:D_EXPERT], gu_b[:, D_EXPERT:2 * D_EXPERT]
            hid = (g * jax.nn.sigmoid(g) * u).astype(BF16)
            gate = jnp.sum(xs_ref[r, D_MODEL:PAYLOAD_W].astype(F32), axis=-1, keepdims=True)
            ys_ref[r, :] = (gate * _dot(hid, wdb_ref[...])).astype(BF16)

    @pl.when(valid == 0)
    def _():
        ys_ref[...] = jnp.zeros_like(ys_ref)


def _expert_layer(layer, tile_e, tile_valid, xs, wg, wu, wd):
    n_steps = tile_e.shape[0]
    return pl.pallas_call(
        _expert_kernel,
        out_shape=jax.ShapeDtypeStruct((xs.shape[0], D_MODEL), BF16),
        grid_spec=pltpu.PrefetchScalarGridSpec(
            num_scalar_prefetch=2,
            grid=(n_steps,),
            in_specs=[
                pl.BlockSpec((ROW_TILE, PAYLOAD_W), lambda i, e, v: (i, 0)),
                pl.BlockSpec((None, None, D_MODEL, D_EXPERT), lambda i, e, v: (layer, e[i], 0, 0)),
                pl.BlockSpec((None, None, D_MODEL, D_EXPERT), lambda i, e, v: (layer, e[i], 0, 0)),
                pl.BlockSpec((None, None, D_EXPERT, D_MODEL), lambda i, e, v: (layer, e[i], 0, 0)),
            ],
            out_specs=pl.BlockSpec((ROW_TILE, D_MODEL), lambda i, e, v: (i, 0)),
            scratch_shapes=[
                pltpu.VMEM((D_MODEL, 2 * D_EXPERT), BF16),
                pltpu.VMEM((D_EXPERT, D_MODEL), BF16),
            ],
        ),
        compiler_params=pltpu.CompilerParams(dimension_semantics=("arbitrary",), vmem_limit_bytes=VMEM_LIMIT),
        name="experts",
    )(tile_e, tile_valid, xs, wg, wu, wd)


def _combine_kernel(n_tab, cnt_tab, base_tab, x_ref, metat_ref, ys_hbm, l2w_ref, l2b_ref, o_ref, yc_ref, sem):
    i = pl.program_id(0)
    n_steps = pl.num_programs(0)
    slot = lax.rem(i, 2)

    def copies(tile, slot_, act):
        def make_copy(buf_row, ys_row, rows):
            return pltpu.make_async_copy(ys_hbm.at[pl.ds(ys_row, rows)],
                                         yc_ref.at[slot_, pl.ds(buf_row, rows)], sem.at[slot_])
        _for_each_run(n_tab, cnt_tab, base_tab, tile, make_copy, act)

    @pl.when(i == 0)
    def _():
        yc_ref[...] = jnp.zeros_like(yc_ref)
        copies(0, 0, lambda cp: cp.start())

    @pl.when(i + 1 < n_steps)
    def _():
        copies(i + 1, 1 - slot, lambda cp: cp.start())

    copies(i, slot, lambda cp: cp.wait())

    t_r = lax.broadcasted_iota(jnp.int32, (MOE_TM, MOE_TM), 0)
    t_c = lax.broadcasted_iota(jnp.int32, (MOE_TM, MOE_TM), 1)
    as_col = lambda row: jnp.sum(jnp.where(t_r == t_c, row, 0.0), axis=-1, keepdims=True)
    col_f = lax.broadcasted_iota(jnp.int32, (MOE_TM, COMPACT_ROWS), 1).astype(F32)
    mt = metat_ref[...]
    onehot = jnp.where(col_f == as_col(mt[0:1, :]), 1.0,
                       jnp.where(col_f == as_col(mt[1:2, :]), 1.0, 0.0)).astype(BF16)
    y = _dot(onehot, yc_ref[slot])
    x = x_ref[...].reshape(MOE_TM, D_MODEL)
    o_ref[...] = _layer_norm(DEEPNORM_ALPHA * x + y, l2w_ref[...], l2b_ref[...], LN_EPS).reshape(o_ref.shape)


def _combine_layer(n_tab, cnt_tab, base_tab, x, metat, ys, l2w, l2b):
    batch, seq, _ = x.shape
    return pl.pallas_call(
        _combine_kernel,
        out_shape=jax.ShapeDtypeStruct(x.shape, F32),
        grid_spec=pltpu.PrefetchScalarGridSpec(
            num_scalar_prefetch=3,
            grid=(batch * seq // MOE_TM,),
            in_specs=[
                _token_step_spec(seq),
                _ROUTE_STEP_SPEC,
                pl.BlockSpec(memory_space=pl.ANY),
                pl.BlockSpec((1, D_MODEL), lambda i, *_: (0, 0)),
                pl.BlockSpec((1, D_MODEL), lambda i, *_: (0, 0)),
            ],
            out_specs=_token_step_spec(seq),
            scratch_shapes=[pltpu.VMEM((2, COMPACT_ROWS, D_MODEL), BF16), pltpu.SemaphoreType.DMA((2,))],
        ),
        compiler_params=pltpu.CompilerParams(dimension_semantics=("arbitrary",), vmem_limit_bytes=VMEM_LIMIT),
        name="combine",
    )(n_tab, cnt_tab, base_tab, x, metat, ys, l2w, l2b)


def _expert_plan(counts, n_steps):
    tiles_e = (counts + (ROW_TILE - 1)) // ROW_TILE
    cum = jnp.cumsum(tiles_e)
    first_tile = cum - tiles_e
    total = cum[-1]
    step = jnp.arange(n_steps, dtype=jnp.int32)
    owner = (step[:, None] >= cum[None, :]).sum(axis=1).astype(jnp.int32)
    is_owner = owner[:, None] == jnp.arange(N_EXPERTS, dtype=jnp.int32)[None, :]
    local = step - jnp.where(is_owner, first_tile[None, :], 0).sum(axis=1)
    rows_left = jnp.where(is_owner, counts[None, :], 0).sum(axis=1) - local * ROW_TILE
    valid = jnp.where(step < total, jnp.clip(rows_left, 0, ROW_TILE), 0).astype(jnp.int32)
    tile_e = jnp.minimum(owner, N_EXPERTS - 1)
    base = (first_tile * ROW_TILE).astype(jnp.int32)
    fill = jnp.stack([base + counts, tiles_e * ROW_TILE - counts], axis=1).reshape(-1)
    fill_tab = jnp.concatenate([fill, (total * ROW_TILE)[None]]).astype(jnp.int32)
    return base, tile_e, valid, fill_tab


def _moe_layer(layer, x, metat, tile_n, tile_cnt, wg, wu, wd, l2w, l2b):
    n_tok = x.shape[0] * x.shape[1]
    n_tiles = n_tok // MOE_TM
    n_steps = (2 * n_tok + N_EXPERTS * (ROW_ALIGN - 1) * n_tiles) // ROW_TILE + N_EXPERTS
    n_tab = tile_n[:, :, 0].reshape(-1)
    cnt_tab = tile_cnt[:, :, 0].reshape(-1)
    counts = tile_cnt[-1, :, 0] + tile_n[-1, :, 0]
    base_tab, tile_e, tile_valid, fill_tab = _expert_plan(counts, n_steps)
    xs = _dispatch_layer(n_tab, cnt_tab, base_tab, fill_tab, x, metat, n_steps * ROW_TILE)
    ys = _expert_layer(layer, tile_e, tile_valid, xs, wg, wu, wd)
    return _combine_layer(n_tab, cnt_tab, base_tab, x, metat, ys, l2w, l2b)


def kernel(x, w_in, b_in, mlstm_norm_w, gmlp_norm_w, gmlp_norm_b, gmlp_ws, gmlp_bs, conv_w, conv_b,
           conv_norm_w, conv_norm_b, w_out, ln1_w, ln1_b, router_w, router_b, w_gate, w_up, w_down,
           ln2_w, ln2_b):
    batch, seq, d = x.shape

    q_lo, k_lo, v_lo = 0, ML_WIDTH, 2 * ML_WIDTH
    gate_lo = 4 * ML_WIDTH
    rest_lo = gate_lo + 2 * ML_HEADS

    def row_part(t):
        return jnp.concatenate([t[..., q_lo:k_lo], t[..., v_lo:gate_lo], t[..., rest_lo:]], axis=-1)

    def col_part(t):
        return jnp.concatenate([t[..., k_lo:v_lo], t[..., gate_lo:rest_lo]], axis=-1)

    tril = jnp.tril(jnp.ones((CHUNK, CHUNK), gmlp_ws.dtype))
    rw_pad = jnp.pad(router_w.astype(F32), ((0, 0), (0, LANES - N_EXPERTS)))
    rw_head = rw_pad.astype(BF16)
    rw_split = jnp.concatenate([rw_head, (rw_pad - rw_head.astype(F32)).astype(BF16)], axis=1)
    rb_col = router_b.astype(F32).reshape(N_EXPERTS, 1)

    for l in range(DEPTH):
        b_row = row_part(b_in[l]).reshape(1, P_ROW)
        b_col = jnp.pad(col_part(b_in[l]), (0, P_COL_PAD - P_COL)).reshape(P_COL_PAD, 1)
        ws_cat = jnp.transpose(gmlp_ws[l] * tril, (1, 0, 2)).reshape(CHUNK, G_HEADS * CHUNK).astype(BF16)
        bs_full = jnp.repeat(gmlp_bs[l].T, G_HEAD_DIM, axis=1)
        cw = jnp.pad(conv_w[l], ((0, 1), (0, 0)))
        x, metat, tile_n, tile_cnt = _mixer_layer(
            l, x, w_in, w_out, b_row, b_col, mlstm_norm_w[l].reshape(1, -1), gmlp_norm_w[l].reshape(1, -1),
            gmlp_norm_b[l].reshape(1, -1), ws_cat, bs_full, cw, conv_b[l].reshape(1, -1),
            conv_norm_w[l].reshape(1, -1), conv_norm_b[l].reshape(1, -1),
            ln1_w[l].reshape(1, -1), ln1_b[l].reshape(1, -1), rw_split, rb_col)
        x = _moe_layer(l, x, metat, tile_n, tile_cnt, w_gate, w_up, w_down,
                       ln2_w[l].reshape(1, -1), ln2_b[l].reshape(1, -1))
    return x
```

```python
import functools

import jax
import jax.numpy as jnp
from jax import lax
from jax.experimental import pallas as pl
from jax.experimental.pallas import tpu as pltpu

D_MODEL = 1024
DEPTH = 4
ML_WIDTH = 512
ML_HEADS = 4
ML_HEAD_DIM = 128
CHUNK = 128
G_WIDTH = 256
G_HEADS = 4
G_HEAD_DIM = 64
C_WIDTH = 256
CONV_WIDTH = 31
N_EXPERTS = 16
N_GROUPS = 4
EXPERTS_PER_GROUP = 4
D_EXPERT = 512
DEEPNORM_ALPHA = (2.0 * DEPTH) ** 0.25
LN_EPS = 1e-5
K_SCALE = ML_HEAD_DIM ** -0.5

Q_LO, V_LO, O_LO, GU_LO, GV_LO, CA_LO, CB_LO = 0, 512, 1024, 1536, 1792, 2048, 2304
P_ROW = 2560
P_COL = ML_WIDTH + 2 * ML_HEADS
P_COL_PAD = ML_WIDTH + 16
IN_Q, IN_K, IN_VO, IN_GATES, IN_REST = (0, 512), (512, 1024), (1024, 2048), (2048, 2056), (2056, 3080)
P_IN = 3080
W_PREP_ROWS = 256

LANES = 128
SUBLANES = 8
MIX_TM = 256
ROW_ALIGN = 16
ROW_TILE = 512
EXPERT_ROW_BLOCKS = 2
PAYLOAD_W = D_MODEL + LANES
CONV_HALO = 32
CONV_SPAN = CONV_HALO - SUBLANES
STREAMS = 2
MOE_TM = STREAMS * MIX_TM
COMPACT_ROWS = 2 * MOE_TM + 2 * LANES
assert COMPACT_ROWS >= 2 * MOE_TM + N_EXPERTS * (ROW_ALIGN - 1)
GATE_PIECES = 3
VMEM_LIMIT = 58 * 1024 * 1024

F32 = jnp.float32
BF16 = jnp.bfloat16
NEG_INF = float("-inf")


def _layer_norm(x, w, b, eps):
    mu = jnp.mean(x, axis=-1, keepdims=True)
    xc = x - mu
    var = jnp.mean(xc * xc, axis=-1, keepdims=True)
    return xc * lax.rsqrt(var + eps) * w + b


def _gelu_tanh(x):
    return 0.5 * x * (1.0 + jnp.tanh(0.7978845608028654 * (x + 0.044715 * (x * x * x))))


def _log_sigmoid(x):
    return jnp.minimum(x, 0.0) - jnp.log1p(jnp.exp(-jnp.abs(x)))


def _dot(a, b):
    return jnp.dot(a, b, preferred_element_type=F32)


def _router_logits(x_new, rw_ref):
    xh = x_new.astype(BF16)
    xl = (x_new - xh.astype(F32)).astype(BF16)
    head = _dot(xh, rw_ref[...])
    return head[:, 0:LANES] + head[:, LANES:2 * LANES] + _dot(xl, rw_ref[:, 0:LANES])


def _route_tile(logits, rb_ref, metat_ref, tile_n_ref, tile_cnt_ref, cnt_ref):
    tm = logits.shape[1]
    s_all = jax.nn.sigmoid(logits)
    sel_all = s_all + rb_ref[...]
    rows_of = lambda v: [v[k:k + 1, :] for k in range(N_EXPERTS)]
    s, sel = rows_of(s_all), rows_of(sel_all)

    best = None
    for g in range(N_GROUPS):
        r = sel[EXPERTS_PER_GROUP * g:EXPERTS_PER_GROUP * (g + 1)]
        hi01, lo01 = jnp.maximum(r[0], r[1]), jnp.minimum(r[0], r[1])
        hi23, lo23 = jnp.maximum(r[2], r[3]), jnp.minimum(r[2], r[3])
        score = jnp.maximum(hi01, hi23) + jnp.maximum(jnp.minimum(hi01, hi23), jnp.maximum(lo01, lo23))
        if best is None:
            best, gidx = score, jnp.zeros(score.shape, jnp.int32)
        else:
            better = score > best
            gidx = jnp.where(better, g, gidx)
            best = jnp.where(better, score, best)

    def of_group(rows):
        out = []
        for j in range(EXPERTS_PER_GROUP):
            v = rows[j]
            for g in range(1, N_GROUPS):
                v = jnp.where(gidx == g, rows[EXPERTS_PER_GROUP * g + j], v)
            out.append(v)
        return out

    def first_max(vals):
        best_v, best_j = vals[0], jnp.zeros(vals[0].shape, F32)
        for j in range(1, len(vals)):
            better = vals[j] > best_v
            best_j = jnp.where(better, float(j), best_j)
            best_v = jnp.where(better, vals[j], best_v)
        return best_j

    def take(vals, idx):
        v = vals[0]
        for j in range(1, len(vals)):
            v = jnp.where(idx == float(j), vals[j], v)
        return v

    cand, cand_s = of_group(sel), of_group(s)
    j1 = first_max(cand)
    j2 = first_max([jnp.where(j1 == float(j), NEG_INF, cand[j]) for j in range(EXPERTS_PER_GROUP)])
    g1, g2 = take(cand_s, j1), take(cand_s, j2)
    tot = g1 + g2
    first_of_group = gidx.astype(F32) * float(EXPERTS_PER_GROUP)
    e1, e2 = first_of_group + j1, first_of_group + j2

    e_f = lax.broadcasted_iota(jnp.int32, (N_EXPERTS, tm), 0).astype(F32)
    assign = jnp.where(e_f == e1, 1.0, jnp.where(e_f == e2, 1.0, 0.0)).astype(BF16)
    t_r = lax.broadcasted_iota(jnp.int32, (tm, tm), 0)
    t_c = lax.broadcasted_iota(jnp.int32, (tm, tm), 1)
    earlier = jnp.where(t_r < t_c, 1.0, 0.0).astype(BF16)
    rank = _dot(assign, earlier)
    n_b = _dot(assign, jnp.ones((tm, LANES), BF16))
    n_up_b = jnp.floor((n_b + (ROW_ALIGN - 1.0)) * (1.0 / ROW_ALIGN)) * ROW_ALIGN
    x_r = lax.broadcasted_iota(jnp.int32, (N_EXPERTS, N_EXPERTS), 0)
    x_c = lax.broadcasted_iota(jnp.int32, (N_EXPERTS, N_EXPERTS), 1)
    lower = jnp.where(x_c < x_r, 1.0, 0.0).astype(BF16)
    off_b = _dot(lower, n_up_b.astype(BF16))
    pos = rows_of(jnp.concatenate([off_b] * (tm // LANES), axis=1) + rank)
    metat_ref[...] = jnp.concatenate(
        [take(pos, e1), take(pos, e2), g1 / tot, g2 / tot, jnp.zeros((SUBLANES - 4, tm), F32)], axis=0)

    cnt = cnt_ref[...]
    tile_n_ref[...] = n_up_b.astype(jnp.int32)
    tile_cnt_ref[...] = cnt.astype(jnp.int32)
    cnt_ref[...] = cnt + n_up_b


def _project(x_tile, w_row_ref, b_row_ref, w_col_ref, b_col_ref, proj_ref, colp_ref):
    xb = x_tile.astype(BF16)
    proj_ref[...] = _dot(xb, w_row_ref[...]) + b_row_ref[...]
    colp_ref[...] = _dot(w_col_ref[...], x_tile.T.astype(BF16)) + b_col_ref[...]


def _mix_tile(x, proj_ref, colp_ref, mnw_ref, gnw_ref, gnb_ref, ws_ref, bs_ref, cw_ref, cb_ref, cnw_ref,
              cnb_ref, w_out_ref, l1w_ref, l1b_ref, hcat_ref, cstate_ref, mstate_ref, cbuf_ref, shift_ref, tm):
    n_chunks = tm // CHUNK
    gates = colp_ref[ML_WIDTH:P_COL, :]
    lane_in_chunk = lax.broadcasted_iota(jnp.int32, gates.shape, 1) % CHUNK
    logf_all = _log_sigmoid(gates)
    bcum = logf_all
    d = 1
    while d < CHUNK:
        bcum = bcum + jnp.where(lane_in_chunk >= d, pltpu.roll(bcum, d, 1), 0.0)
        d *= 2

    row_i = lax.broadcasted_iota(jnp.int32, (CHUNK, CHUNK), 0)
    col_i = lax.broadcasted_iota(jnp.int32, (CHUNK, CHUNK), 1)
    causal = col_i <= row_i
    diag = col_i == row_i
    ones_col = jnp.where(col_i == 0, 1.0, 0.0).astype(BF16)

    for sc in range(STREAMS * n_chunks):
        st, c = divmod(sc, n_chunks)
        r0 = st * tm + c * CHUNK
        rows = slice(r0, r0 + CHUNK)
        m_rows = slice(st * SUBLANES, st * SUBLANES + ML_HEADS)
        logi = gates[0:ML_HEADS, r0:r0 + CHUNK]
        logf = logf_all[ML_HEADS:2 * ML_HEADS, r0:r0 + CHUNK]
        b_row = bcum[ML_HEADS:2 * ML_HEADS, r0:r0 + CHUNK]
        g = jnp.sum(logf, axis=-1, keepdims=True)
        m_prev = mstate_ref[m_rows, 0:1]
        a_row = g - b_row + logi
        m_new = jnp.maximum(g + m_prev, jnp.max(a_row, axis=-1, keepdims=True))
        w_row = jnp.exp(a_row - m_new) * K_SCALE
        decay = jnp.exp(g + m_prev - m_new)
        mstate_ref[m_rows, :] = jnp.broadcast_to(m_new, (ML_HEADS, LANES))

        for h in range(ML_HEADS):
            hs = slice(h * ML_HEAD_DIM, (h + 1) * ML_HEAD_DIM)
            q = proj_ref[rows, Q_LO + h * ML_HEAD_DIM:Q_LO + (h + 1) * ML_HEAD_DIM].astype(BF16)
            v = proj_ref[rows, V_LO + h * ML_HEAD_DIM:V_LO + (h + 1) * ML_HEAD_DIM].astype(BF16)
            o_gate = proj_ref[rows, O_LO + h * ML_HEAD_DIM:O_LO + (h + 1) * ML_HEAD_DIM]
            kt = colp_ref[hs, r0:r0 + CHUNK]
            v_ext = jnp.concatenate([v, ones_col], axis=1)

            b_r = b_row[h:h + 1, :]
            b_c = jnp.sum(jnp.where(diag, b_r, 0.0), axis=-1, keepdims=True)
            log_d = jnp.where(causal, b_c - b_r + logi[h:h + 1, :], NEG_INF)
            mp = m_prev[h:h + 1, :]
            log_inter = b_c + mp
            m_row = jnp.maximum(log_inter, jnp.max(log_d, axis=-1, keepdims=True))
            c_ext = cstate_ref[st * ML_HEADS + h]
            from_q = _dot(q, jnp.concatenate([kt.astype(BF16), c_ext.astype(BF16)], axis=1))
            p = from_q[:, 0:CHUNK] * (jnp.exp(log_d - m_row) * K_SCALE)
            w_inter = jnp.exp(log_inter - m_row)
            ktw = (kt * w_row[h:h + 1, :]).astype(BF16)
            onto_v = _dot(jnp.concatenate([p.astype(BF16), ktw], axis=0), v_ext)
            tot = onto_v[0:CHUNK, :] + w_inter * from_q[:, CHUNK:]
            den = jnp.maximum(jnp.abs(tot[:, ML_HEAD_DIM:ML_HEAD_DIM + 1]), jnp.exp(-m_row))
            hh = tot[:, 0:ML_HEAD_DIM] / den
            cstate_ref[st * ML_HEADS + h] = decay[h:h + 1, :] * c_ext + onto_v[CHUNK:, :]

            mu = jnp.mean(hh, axis=-1, keepdims=True)
            hc = hh - mu
            var = jnp.mean(hc * hc, axis=-1, keepdims=True)
            hn = hc * lax.rsqrt(var + 1e-6) * mnw_ref[:, hs]
            hcat_ref[rows, hs] = (jax.nn.sigmoid(o_gate) * hn).astype(BF16)

        u = _gelu_tanh(proj_ref[rows, GU_LO:GU_LO + G_WIDTH])
        z = _layer_norm(_gelu_tanh(proj_ref[rows, GV_LO:GV_LO + G_WIDTH]), gnw_ref[...], gnb_ref[...], LN_EPS)
        lane_head = lax.broadcasted_iota(jnp.int32, (CHUNK, G_WIDTH), 1) // G_HEAD_DIM
        z_bd = jnp.concatenate(
            [jnp.where(lane_head == h, z, 0.0).astype(BF16) for h in range(G_HEADS)], axis=0)
        zs = _dot(ws_ref[...], z_bd) + bs_ref[...]
        hcat_ref[rows, ML_WIDTH:ML_WIDTH + G_WIDTH] = (u * zs).astype(BF16)

        ca = proj_ref[rows, CA_LO:CA_LO + C_WIDTH]
        cb = proj_ref[rows, CB_LO:CB_LO + C_WIDTH]
        cbuf_ref[st, CONV_HALO + c * CHUNK:CONV_HALO + (c + 1) * CHUNK, :] = ca * jax.nn.sigmoid(cb)

    first_tap = CONV_HALO - (CONV_WIDTH - 1)
    for st in range(STREAMS):
        for sh in range(1, SUBLANES):
            shift_ref[st, sh - 1] = cbuf_ref[st, sh:sh + tm + CONV_SPAN, :]
        for c in range(n_chunks):
            acc = jnp.zeros((CHUNK, C_WIDTH), F32) + cb_ref[...]
            for k in range(CONV_WIDTH):
                whole, sh = divmod(first_tap + k, SUBLANES)
                lo = c * CHUNK + whole * SUBLANES
                tap = cbuf_ref[st, lo:lo + CHUNK, :] if sh == 0 else shift_ref[st, sh - 1, lo:lo + CHUNK, :]
                acc = acc + cw_ref[k:k + 1, :] * tap
            cn = _layer_norm(acc, cnw_ref[...], cnb_ref[...], LN_EPS)
            r0 = st * tm + c * CHUNK
            hcat_ref[r0:r0 + CHUNK, ML_WIDTH + G_WIDTH:D_MODEL] = (cn * jax.nn.sigmoid(cn)).astype(BF16)
        cbuf_ref[st, 0:CONV_HALO, :] = cbuf_ref[st, tm:tm + CONV_HALO, :]

    y = _dot(hcat_ref[...], w_out_ref[...])
    x_new = _layer_norm(DEEPNORM_ALPHA * x + y, l1w_ref[...], l1b_ref[...], LN_EPS)
    return x_new


def _prepare_in_proj(w_in_ref, w_row_ref, w_col_ref):
    for r0 in range(0, D_MODEL, W_PREP_ROWS):
        rs = slice(r0, r0 + W_PREP_ROWS)
        w_row_ref[rs, Q_LO:V_LO] = w_in_ref[rs, IN_Q[0]:IN_Q[1]].astype(BF16)
        w_row_ref[rs, V_LO:GU_LO] = w_in_ref[rs, IN_VO[0]:IN_VO[1]].astype(BF16)
        w_row_ref[rs, GU_LO:P_ROW] = w_in_ref[rs, IN_GATES[0]:P_IN][:, IN_REST[0] - IN_GATES[0]:].astype(BF16)
        w_col_ref[0:ML_WIDTH, rs] = w_in_ref[rs, IN_K[0]:IN_K[1]].T.astype(BF16)
        gates_t = w_in_ref[rs, IN_GATES[0]:IN_GATES[0] + LANES].T
        keep = lax.broadcasted_iota(jnp.int32, (P_COL_PAD - ML_WIDTH, W_PREP_ROWS), 0) < 2 * ML_HEADS
        w_col_ref[ML_WIDTH:P_COL_PAD, rs] = jnp.where(keep, gates_t[0:P_COL_PAD - ML_WIDTH, :], 0.0).astype(BF16)


def _mixer_kernel(x_ref, w_in_ref, w_out_f32_ref, b_row_ref, b_col_ref, mnw_ref, gnw_ref, gnb_ref,
                  ws_ref, bs_ref, cw_ref, cb_ref, cnw_ref, cnb_ref, l1w_ref, l1b_ref, rw_ref, rb_ref,
                  o_ref, metat_ref, tile_n_ref, tile_cnt_ref,
                  w_row_ref, w_col_ref, w_out_ref, proj_ref, colp_ref, hcat_ref, cstate_ref, mstate_ref, cbuf_ref, shift_ref,
                  cnt_ref, *, tm, steps_per_seq):
    i = pl.program_id(0)

    @pl.when(i == 0)
    def _():
        cnt_ref[...] = jnp.zeros_like(cnt_ref)
        _prepare_in_proj(w_in_ref, w_row_ref, w_col_ref)
        for r0 in range(0, D_MODEL, W_PREP_ROWS):
            w_out_ref[r0:r0 + W_PREP_ROWS, :] = w_out_f32_ref[r0:r0 + W_PREP_ROWS, :].astype(BF16)

    @pl.when(i % steps_per_seq == 0)
    def _():
        cstate_ref[...] = jnp.zeros_like(cstate_ref)
        mstate_ref[...] = jnp.zeros_like(mstate_ref)
        cbuf_ref[:, 0:CONV_HALO, :] = jnp.zeros((STREAMS, CONV_HALO, C_WIDTH), F32)

    x = x_ref[...].reshape(STREAMS * tm, D_MODEL)
    _project(x, w_row_ref, b_row_ref, w_col_ref, b_col_ref, proj_ref, colp_ref)
    x_new = _mix_tile(x, proj_ref, colp_ref, mnw_ref, gnw_ref, gnb_ref, ws_ref, bs_ref, cw_ref, cb_ref, cnw_ref,
                      cnb_ref, w_out_ref, l1w_ref, l1b_ref, hcat_ref, cstate_ref, mstate_ref, cbuf_ref, shift_ref,
                      tm)
    o_ref[...] = x_new.reshape(STREAMS, tm, D_MODEL)
    _route_tile(_router_logits(x_new, rw_ref).T[0:N_EXPERTS, :], rb_ref, metat_ref, tile_n_ref, tile_cnt_ref, cnt_ref)


def _full(shape):
    nd = len(shape)
    return pl.BlockSpec(shape, lambda i, _nd=nd: (0,) * _nd, pipeline_mode=pl.Buffered(1))


def _layer_of(stacked, layer):
    nd = stacked.ndim - 1
    return pl.BlockSpec((None,) + stacked.shape[1:], lambda i, _nd=nd: (layer,) + (0,) * _nd,
                        pipeline_mode=pl.Buffered(1))


def _mixer_layer(layer, x, w_in, w_out, b_row, b_col, mnw, gnw, gnb, ws_cat, bs_full, cw, cb, cnw, cnb,
                 l1w, l1b, rw_split, rb_col):
    batch, seq, _ = x.shape
    tm = MIX_TM
    steps_per_seq = seq // tm
    n_steps = batch // STREAMS * steps_per_seq
    rows = STREAMS * tm
    kernel = functools.partial(_mixer_kernel, tm=tm, steps_per_seq=steps_per_seq)
    weights = (b_row, b_col, mnw, gnw, gnb, ws_cat, bs_full, cw, cb, cnw, cnb, l1w, l1b, rw_split, rb_col)
    tile_spec = pl.BlockSpec((None, N_EXPERTS, LANES), lambda i: (i, 0, 0))
    step_spec = pl.BlockSpec((STREAMS, tm, D_MODEL), lambda i: (i // steps_per_seq, i % steps_per_seq, 0))
    return pl.pallas_call(
        kernel,
        out_shape=(
            jax.ShapeDtypeStruct(x.shape, F32),
            jax.ShapeDtypeStruct((n_steps, SUBLANES, rows), F32),
            jax.ShapeDtypeStruct((n_steps, N_EXPERTS, LANES), jnp.int32),
            jax.ShapeDtypeStruct((n_steps, N_EXPERTS, LANES), jnp.int32),
        ),
        grid=(n_steps,),
        in_specs=[step_spec, _layer_of(w_in, layer), _layer_of(w_out, layer)] + [_full(w.shape) for w in weights],
        out_specs=(step_spec, pl.BlockSpec((None, SUBLANES, rows), lambda i: (i, 0, 0)), tile_spec, tile_spec),
        scratch_shapes=[
            pltpu.VMEM((D_MODEL, P_ROW), BF16),
            pltpu.VMEM((P_COL_PAD, D_MODEL), BF16),
            pltpu.VMEM((D_MODEL, D_MODEL), BF16),
            pltpu.VMEM((rows, P_ROW), F32),
            pltpu.VMEM((P_COL_PAD, rows), F32),
            pltpu.VMEM((rows, D_MODEL), BF16),
            pltpu.VMEM((STREAMS * ML_HEADS, ML_HEAD_DIM, 2 * ML_HEAD_DIM), F32),
            pltpu.VMEM((STREAMS * SUBLANES, LANES), F32),
            pltpu.VMEM((STREAMS, tm + CONV_HALO, C_WIDTH), F32),
            pltpu.VMEM((STREAMS, SUBLANES - 1, tm + CONV_SPAN, C_WIDTH), F32),
            pltpu.VMEM((N_EXPERTS, LANES), F32),
        ],
        compiler_params=pltpu.CompilerParams(
            dimension_semantics=("arbitrary",), vmem_limit_bytes=VMEM_LIMIT),
        name="mixer",
    )(x, w_in, w_out, *weights)


def _pow2_pieces(n, largest, act):
    piece = largest
    while piece >= ROW_ALIGN:
        start = jnp.bitwise_and(n, -2 * piece)

        @pl.when(jnp.bitwise_and(n, piece) != 0)
        def _(piece=piece, start=start):
            act(start, piece)
        piece //= 2


def _for_each_run(n_tab, cnt_tab, base_tab, tile, make_copy, act):
    def body(e, off):
        n = n_tab[tile * N_EXPERTS + e]
        base = base_tab[e] + cnt_tab[tile * N_EXPERTS + e]
        _pow2_pieces(n, MOE_TM, lambda start, size: act(make_copy(
            pl.multiple_of(off + start, ROW_ALIGN), pl.multiple_of(base + start, ROW_ALIGN), size)))
        return off + n

    lax.fori_loop(0, N_EXPERTS, body, jnp.int32(0))


def _dispatch_kernel(n_tab, cnt_tab, base_tab, fill_tab, x_ref, metat_ref, xs_hbm, comp_ref, zero_ref, sem,
                     *, n_rows):
    i = pl.program_id(0)
    last = pl.num_programs(0) - 1
    slot = lax.rem(i, 2)

    def copies(tile, slot_, act):
        def make_copy(src_row, dst_row, rows):
            return pltpu.make_async_copy(comp_ref.at[slot_, pl.ds(src_row, rows)],
                                         xs_hbm.at[pl.ds(dst_row, rows)], sem.at[slot_])
        _for_each_run(n_tab, cnt_tab, base_tab, tile, make_copy, act)

    def zero_fill(act):
        def zero_copy(dst_row, rows):
            return pltpu.make_async_copy(zero_ref.at[pl.ds(0, rows)],
                                         xs_hbm.at[pl.ds(pl.multiple_of(dst_row, ROW_ALIGN), rows)], sem.at[2])

        def per_expert(e, carry):
            first = fill_tab[2 * e]
            _pow2_pieces(fill_tab[2 * e + 1], ROW_TILE // 2, lambda start, size: act(zero_copy(first + start, size)))
            return carry

        lax.fori_loop(0, N_EXPERTS, per_expert, jnp.int32(0))

        def per_half_tile(k, carry):
            act(zero_copy(fill_tab[2 * N_EXPERTS] + k * (ROW_TILE // 2), ROW_TILE // 2))
            return carry

        lax.fori_loop(0, (n_rows - fill_tab[2 * N_EXPERTS]) // (ROW_TILE // 2), per_half_tile, jnp.int32(0))

    @pl.when(i == 0)
    def _():
        zero_ref[...] = jnp.zeros_like(zero_ref)
        zero_fill(lambda cp: cp.start())

    @pl.when(i >= 2)
    def _():
        copies(i - 2, slot, lambda cp: cp.wait())

    mt = metat_ref[...]
    row_f = lax.broadcasted_iota(jnp.int32, (COMPACT_ROWS, MOE_TM), 0).astype(F32)
    hit1 = row_f == mt[0:1, :]
    hit2 = row_f == mt[1:2, :]
    onehot = jnp.where(hit1, 1.0, jnp.where(hit2, 1.0, 0.0)).astype(BF16)
    x_tile = x_ref[...].reshape(MOE_TM, D_MODEL).astype(BF16)
    comp_ref[slot, :, 0:D_MODEL] = _dot(onehot, x_tile).astype(BF16)
    gate = jnp.sum(jnp.where(hit1, mt[2:3, :], jnp.where(hit2, mt[3:4, :], 0.0)), axis=-1, keepdims=True)
    lane = lax.broadcasted_iota(jnp.int32, (COMPACT_ROWS, LANES), 1)
    packed, rest = jnp.zeros((COMPACT_ROWS, LANES), F32), gate
    for piece in range(GATE_PIECES):
        head = rest.astype(BF16).astype(F32)
        packed = jnp.where(lane == piece, head, packed)
        rest = rest - head
    comp_ref[slot, :, D_MODEL:PAYLOAD_W] = packed.astype(BF16)
    copies(i, slot, lambda cp: cp.start())

    @pl.when(i == last)
    def _():
        @pl.when(i >= 1)
        def _():
            copies(i - 1, 1 - slot, lambda cp: cp.wait())
        copies(i, slot, lambda cp: cp.wait())
        zero_fill(lambda cp: cp.wait())


def _token_step_spec(seq):
    tiles_per_seq = seq // MIX_TM
    return pl.BlockSpec((STREAMS, MIX_TM, D_MODEL), lambda i, *_: (i // tiles_per_seq, i % tiles_per_seq, 0))


_ROUTE_STEP_SPEC = pl.BlockSpec((None, SUBLANES, MOE_TM), lambda i, *_: (i, 0, 0))


def _dispatch_layer(n_tab, cnt_tab, base_tab, fill_tab, x, metat, n_rows):
    batch, seq, _ = x.shape
    return pl.pallas_call(
        functools.partial(_dispatch_kernel, n_rows=n_rows),
        out_shape=jax.ShapeDtypeStruct((n_rows, PAYLOAD_W), BF16),
        grid_spec=pltpu.PrefetchScalarGridSpec(
            num_scalar_prefetch=4,
            grid=(batch * seq // MOE_TM,),
            in_specs=[_token_step_spec(seq), _ROUTE_STEP_SPEC],
            out_specs=pl.BlockSpec(memory_space=pl.ANY),
            scratch_shapes=[
                pltpu.VMEM((2, COMPACT_ROWS, PAYLOAD_W), BF16),
                pltpu.VMEM((ROW_TILE // 2, PAYLOAD_W), BF16),
                pltpu.SemaphoreType.DMA((3,)),
            ],
        ),
        compiler_params=pltpu.CompilerParams(dimension_semantics=("arbitrary",), vmem_limit_bytes=VMEM_LIMIT),
        name="dispatch",
    )(n_tab, cnt_tab, base_tab, fill_tab, x, metat)


def _expert_kernel(tile_e, tile_valid, xs_ref, wg_ref, wu_ref, wd_ref, ys_ref, wgu_ref, wdb_ref):
    i = pl.program_id(0)
    valid = tile_valid[i]
    new_expert = jnp.logical_or(i == 0, tile_e[i] != tile_e[jnp.maximum(i - 1, 0)])

    @pl.when(jnp.logical_and(valid > 0, new_expert))
    def _():
        wgu_ref[:, 0:D_EXPERT] = wg_ref[...].astype(BF16)
        wgu_ref[:, D_EXPERT:2 * D_EXPERT] = wu_ref[...].astype(BF16)
        wdb_ref[...] = wd_ref[...].astype(BF16)

    @pl.when(valid > 0)
    def _():
        blk = ROW_TILE // EXPERT_ROW_BLOCKS
        rows = [slice(b * blk, (b + 1) * blk) for b in range(EXPERT_ROW_BLOCKS)]
        gu = [_dot(xs_ref[r, 0:D_MODEL], wgu_ref[...]) for r in rows]
        for r, gu_b in zip(rows, gu):
            g, u = gu_b[:, 0:D_EXPERT], gu_b[:, D_EXPERT:2 * D_EXPERT]
            hid = (g * jax.nn.sigmoid(g) * u).astype(BF16)
            gate = jnp.sum(xs_ref[r, D_MODEL:PAYLOAD_W].astype(F32), axis=-1, keepdims=True)
            ys_ref[r, :] = (gate * _dot(hid, wdb_ref[...])).astype(BF16)

    @pl.when(valid == 0)
    def _():
        ys_ref[...] = jnp.zeros_like(ys_ref)


def _expert_layer(layer, tile_e, tile_valid, xs, wg, wu, wd):
    n_steps = tile_e.shape[0]
    return pl.pallas_call(
        _expert_kernel,
        out_shape=jax.ShapeDtypeStruct((xs.shape[0], D_MODEL), BF16),
        grid_spec=pltpu.PrefetchScalarGridSpec(
            num_scalar_prefetch=2,
            grid=(n_steps,),
            in_specs=[
                pl.BlockSpec((ROW_TILE, PAYLOAD_W), lambda i, e, v: (i, 0)),
                pl.BlockSpec((None, None, D_MODEL, D_EXPERT), lambda i, e, v: (layer, e[i], 0, 0)),
                pl.BlockSpec((None, None, D_MODEL, D_EXPERT), lambda i, e, v: (layer, e[i], 0, 0)),
                pl.BlockSpec((None, None, D_EXPERT, D_MODEL), lambda i, e, v: (layer, e[i], 0, 0)),
            ],
            out_specs=pl.BlockSpec((ROW_TILE, D_MODEL), lambda i, e, v: (i, 0)),
            scratch_shapes=[
                pltpu.VMEM((D_MODEL, 2 * D_EXPERT), BF16),
                pltpu.VMEM((D_EXPERT, D_MODEL), BF16),
            ],
        ),
        compiler_params=pltpu.CompilerParams(dimension_semantics=("arbitrary",), vmem_limit_bytes=VMEM_LIMIT),
        name="experts",
    )(tile_e, tile_valid, xs, wg, wu, wd)


def _combine_kernel(n_tab, cnt_tab, base_tab, x_ref, metat_ref, ys_hbm, l2w_ref, l2b_ref, o_ref, yc_ref, sem):
    i = pl.program_id(0)
    n_steps = pl.num_programs(0)
    slot = lax.rem(i, 2)

    def copies(tile, slot_, act):
        def make_copy(buf_row, ys_row, rows):
            return pltpu.make_async_copy(ys_hbm.at[pl.ds(ys_row, rows)],
                                         yc_ref.at[slot_, pl.ds(buf_row, rows)], sem.at[slot_])
        _for_each_run(n_tab, cnt_tab, base_tab, tile, make_copy, act)

    @pl.when(i == 0)
    def _():
        yc_ref[...] = jnp.zeros_like(yc_ref)
        copies(0, 0, lambda cp: cp.start())

    @pl.when(i + 1 < n_steps)
    def _():
        copies(i + 1, 1 - slot, lambda cp: cp.start())

    copies(i, slot, lambda cp: cp.wait())

    t_r = lax.broadcasted_iota(jnp.int32, (MOE_TM, MOE_TM), 0)
    t_c = lax.broadcasted_iota(jnp.int32, (MOE_TM, MOE_TM), 1)
    as_col = lambda row: jnp.sum(jnp.where(t_r == t_c, row, 0.0), axis=-1, keepdims=True)
    col_f = lax.broadcasted_iota(jnp.int32, (MOE_TM, COMPACT_ROWS), 1).astype(F32)
    mt = metat_ref[...]
    onehot = jnp.where(col_f == as_col(mt[0:1, :]), 1.0,
                       jnp.where(col_f == as_col(mt[1:2, :]), 1.0, 0.0)).astype(BF16)
    y = _dot(onehot, yc_ref[slot])
    x = x_ref[...].reshape(MOE_TM, D_MODEL)
    o_ref[...] = _layer_norm(DEEPNORM_ALPHA * x + y, l2w_ref[...], l2b_ref[...], LN_EPS).reshape(o_ref.shape)


def _combine_layer(n_tab, cnt_tab, base_tab, x, metat, ys, l2w, l2b):
    batch, seq, _ = x.shape
    return pl.pallas_call(
        _combine_kernel,
        out_shape=jax.ShapeDtypeStruct(x.shape, F32),
        grid_spec=pltpu.PrefetchScalarGridSpec(
            num_scalar_prefetch=3,
            grid=(batch * seq // MOE_TM,),
            in_specs=[
                _token_step_spec(seq),
                _ROUTE_STEP_SPEC,
                pl.BlockSpec(memory_space=pl.ANY),
                pl.BlockSpec((1, D_MODEL), lambda i, *_: (0, 0)),
                pl.BlockSpec((1, D_MODEL), lambda i, *_: (0, 0)),
            ],
            out_specs=_token_step_spec(seq),
            scratch_shapes=[pltpu.VMEM((2, COMPACT_ROWS, D_MODEL), BF16), pltpu.SemaphoreType.DMA((2,))],
        ),
        compiler_params=pltpu.CompilerParams(dimension_semantics=("arbitrary",), vmem_limit_bytes=VMEM_LIMIT),
        name="combine",
    )(n_tab, cnt_tab, base_tab, x, metat, ys, l2w, l2b)


def _expert_plan(counts, n_steps):
    tiles_e = (counts + (ROW_TILE - 1)) // ROW_TILE
    cum = jnp.cumsum(tiles_e)
    first_tile = cum - tiles_e
    total = cum[-1]
    step = jnp.arange(n_steps, dtype=jnp.int32)
    owner = (step[:, None] >= cum[None, :]).sum(axis=1).astype(jnp.int32)
    is_owner = owner[:, None] == jnp.arange(N_EXPERTS, dtype=jnp.int32)[None, :]
    local = step - jnp.where(is_owner, first_tile[None, :], 0).sum(axis=1)
    rows_left = jnp.where(is_owner, counts[None, :], 0).sum(axis=1) - local * ROW_TILE
    valid = jnp.where(step < total, jnp.clip(rows_left, 0, ROW_TILE), 0).astype(jnp.int32)
    tile_e = jnp.minimum(owner, N_EXPERTS - 1)
    base = (first_tile * ROW_TILE).astype(jnp.int32)
    fill = jnp.stack([base + counts, tiles_e * ROW_TILE - counts], axis=1).reshape(-1)
    fill_tab = jnp.concatenate([fill, (total * ROW_TILE)[None]]).astype(jnp.int32)
    return base, tile_e, valid, fill_tab


def _moe_layer(layer, x, metat, tile_n, tile_cnt, wg, wu, wd, l2w, l2b):
    n_tok = x.shape[0] * x.shape[1]
    n_tiles = n_tok // MOE_TM
    n_steps = (2 * n_tok + N_EXPERTS * (ROW_ALIGN - 1) * n_tiles) // ROW_TILE + N_EXPERTS
    n_tab = tile_n[:, :, 0].reshape(-1)
    cnt_tab = tile_cnt[:, :, 0].reshape(-1)
    counts = tile_cnt[-1, :, 0] + tile_n[-1, :, 0]
    base_tab, tile_e, tile_valid, fill_tab = _expert_plan(counts, n_steps)
    xs = _dispatch_layer(n_tab, cnt_tab, base_tab, fill_tab, x, metat, n_steps * ROW_TILE)
    ys = _expert_layer(layer, tile_e, tile_valid, xs, wg, wu, wd)
    return _combine_layer(n_tab, cnt_tab, base_tab, x, metat, ys, l2w, l2b)


def kernel(x, w_in, b_in, mlstm_norm_w, gmlp_norm_w, gmlp_norm_b, gmlp_ws, gmlp_bs, conv_w, conv_b,
           conv_norm_w, conv_norm_b, w_out, ln1_w, ln1_b, router_w, router_b, w_gate, w_up, w_down,
           ln2_w, ln2_b):
    batch, seq, d = x.shape

    q_lo, k_lo, v_lo = 0, ML_WIDTH, 2 * ML_WIDTH
    gate_lo = 4 * ML_WIDTH
    rest_lo = gate_lo + 2 * ML_HEADS

    def row_part(t):
        return jnp.concatenate([t[..., q_lo:k_lo], t[..., v_lo:gate_lo], t[..., rest_lo:]], axis=-1)

    def col_part(t):
        return jnp.concatenate([t[..., k_lo:v_lo], t[..., gate_lo:rest_lo]], axis=-1)

    tril = jnp.tril(jnp.ones((CHUNK, CHUNK), gmlp_ws.dtype))
    rw_pad = jnp.pad(router_w.astype(F32), ((0, 0), (0, LANES - N_EXPERTS)))
    rw_head = rw_pad.astype(BF16)
    rw_split = jnp.concatenate([rw_head, (rw_pad - rw_head.astype(F32)).astype(BF16)], axis=1)
    rb_col = router_b.astype(F32).reshape(N_EXPERTS, 1)

    for l in range(DEPTH):
        b_row = row_part(b_in[l]).reshape(1, P_ROW)
        b_col = jnp.pad(col_part(b_in[l]), (0, P_COL_PAD - P_COL)).reshape(P_COL_PAD, 1)
        ws_cat = jnp.transpose(gmlp_ws[l] * tril, (1, 0, 2)).reshape(CHUNK, G_HEADS * CHUNK).astype(BF16)
        bs_full = jnp.repeat(gmlp_bs[l].T, G_HEAD_DIM, axis=1)
        cw = jnp.pad(conv_w[l], ((0, 1), (0, 0)))
        x, metat, tile_n, tile_cnt = _mixer_layer(
            l, x, w_in, w_out, b_row, b_col, mlstm_norm_w[l].reshape(1, -1), gmlp_norm_w[l].reshape(1, -1),
            gmlp_norm_b[l].reshape(1, -1), ws_cat, bs_full, cw, conv_b[l].reshape(1, -1),
            conv_norm_w[l].reshape(1, -1), conv_norm_b[l].reshape(1, -1),
            ln1_w[l].reshape(1, -1), ln1_b[l].reshape(1, -1), rw_split, rb_col)
        x = _moe_layer(l, x, metat, tile_n, tile_cnt, w_gate, w_up, w_down,
                       ln2_w[l].reshape(1, -1), ln2_b[l].reshape(1, -1))
    return x
```

```python
import functools

import jax
import jax.numpy as jnp
from jax import lax
from jax.experimental import pallas as pl
from jax.experimental.pallas import tpu as pltpu

D_MODEL = 1024
DEPTH = 4
ML_WIDTH = 512
ML_HEADS = 4
ML_HEAD_DIM = 128
CHUNK = 128
G_WIDTH = 256
G_HEADS = 4
G_HEAD_DIM = 64
C_WIDTH = 256
CONV_WIDTH = 31
N_EXPERTS = 16
N_GROUPS = 4
EXPERTS_PER_GROUP = 4
D_EXPERT = 512
DEEPNORM_ALPHA = (2.0 * DEPTH) ** 0.25
LN_EPS = 1e-5
K_SCALE = ML_HEAD_DIM ** -0.5

Q_LO, V_LO, O_LO, GU_LO, GV_LO, CA_LO, CB_LO = 0, 512, 1024, 1536, 1792, 2048, 2304
P_ROW = 2560
P_COL = ML_WIDTH + 2 * ML_HEADS
P_COL_PAD = ML_WIDTH + 16
IN_Q, IN_K, IN_VO, IN_GATES, IN_REST = (0, 512), (512, 1024), (1024, 2048), (2048, 2056), (2056, 3080)
P_IN = 3080
W_PREP_ROWS = 256

LANES = 128
SUBLANES = 8
MIX_TM = 256
ROW_ALIGN = 16
ROW_TILE = 512
EXPERT_ROW_BLOCKS = 2
PAYLOAD_W = D_MODEL + LANES
CONV_HALO = 32
CONV_SPAN = CONV_HALO - SUBLANES
STREAMS = 2
MOE_TM = STREAMS * MIX_TM
COMPACT_ROWS = 2 * MOE_TM + 2 * LANES
assert COMPACT_ROWS >= 2 * MOE_TM + N_EXPERTS * (ROW_ALIGN - 1)
GATE_PIECES = 3
VMEM_LIMIT = 58 * 1024 * 1024

F32 = jnp.float32
BF16 = jnp.bfloat16
NEG_INF = float("-inf")


def _layer_norm(x, w, b, eps):
    mu = jnp.mean(x, axis=-1, keepdims=True)
    xc = x - mu
    var = jnp.mean(xc * xc, axis=-1, keepdims=True)
    return xc * lax.rsqrt(var + eps) * w + b


def _gelu_tanh(x):
    return 0.5 * x * (1.0 + jnp.tanh(0.7978845608028654 * (x + 0.044715 * (x * x * x))))


def _log_sigmoid(x):
    return jnp.minimum(x, 0.0) - jnp.log1p(jnp.exp(-jnp.abs(x)))


def _dot(a, b):
    return jnp.dot(a, b, preferred_element_type=F32)


def _router_logits(x_new, rw_ref):
    xh = x_new.astype(BF16)
    xl = (x_new - xh.astype(F32)).astype(BF16)
    head = _dot(xh, rw_ref[...])
    return head[:, 0:LANES] + head[:, LANES:2 * LANES] + _dot(xl, rw_ref[:, 0:LANES])


def _route_tile(logits, rb_ref, metat_ref, tile_n_ref, tile_cnt_ref, cnt_ref):
    tm = logits.shape[1]
    s_all = jax.nn.sigmoid(logits)
    sel_all = s_all + rb_ref[...]
    rows_of = lambda v: [v[k:k + 1, :] for k in range(N_EXPERTS)]
    s, sel = rows_of(s_all), rows_of(sel_all)

    best = None
    for g in range(N_GROUPS):
        r = sel[EXPERTS_PER_GROUP * g:EXPERTS_PER_GROUP * (g + 1)]
        hi01, lo01 = jnp.maximum(r[0], r[1]), jnp.minimum(r[0], r[1])
        hi23, lo23 = jnp.maximum(r[2], r[3]), jnp.minimum(r[2], r[3])
        score = jnp.maximum(hi01, hi23) + jnp.maximum(jnp.minimum(hi01, hi23), jnp.maximum(lo01, lo23))
        if best is None:
            best, gidx = score, jnp.zeros(score.shape, jnp.int32)
        else:
            better = score > best
            gidx = jnp.where(better, g, gidx)
            best = jnp.where(better, score, best)

    def of_group(rows):
        out = []
        for j in range(EXPERTS_PER_GROUP):
            v = rows[j]
            for g in range(1, N_GROUPS):
                v = jnp.where(gidx == g, rows[EXPERTS_PER_GROUP * g + j], v)
            out.append(v)
        return out

    def first_max(vals):
        best_v, best_j = vals[0], jnp.zeros(vals[0].shape, F32)
        for j in range(1, len(vals)):
            better = vals[j] > best_v
            best_j = jnp.where(better, float(j), best_j)
            best_v = jnp.where(better, vals[j], best_v)
        return best_j

    def take(vals, idx):
        v = vals[0]
        for j in range(1, len(vals)):
            v = jnp.where(idx == float(j), vals[j], v)
        return v

    cand, cand_s = of_group(sel), of_group(s)
    j1 = first_max(cand)
    j2 = first_max([jnp.where(j1 == float(j), NEG_INF, cand[j]) for j in range(EXPERTS_PER_GROUP)])
    g1, g2 = take(cand_s, j1), take(cand_s, j2)
    tot = g1 + g2
    first_of_group = gidx.astype(F32) * float(EXPERTS_PER_GROUP)
    e1, e2 = first_of_group + j1, first_of_group + j2

    e_f = lax.broadcasted_iota(jnp.int32, (N_EXPERTS, tm), 0).astype(F32)
    assign = jnp.where(e_f == e1, 1.0, jnp.where(e_f == e2, 1.0, 0.0)).astype(BF16)
    t_r = lax.broadcasted_iota(jnp.int32, (tm, tm), 0)
    t_c = lax.broadcasted_iota(jnp.int32, (tm, tm), 1)
    earlier = jnp.where(t_r < t_c, 1.0, 0.0).astype(BF16)
    rank = _dot(assign, earlier)
    n_b = _dot(assign, jnp.ones((tm, LANES), BF16))
    n_up_b = jnp.floor((n_b + (ROW_ALIGN - 1.0)) * (1.0 / ROW_ALIGN)) * ROW_ALIGN
    x_r = lax.broadcasted_iota(jnp.int32, (N_EXPERTS, N_EXPERTS), 0)
    x_c = lax.broadcasted_iota(jnp.int32, (N_EXPERTS, N_EXPERTS), 1)
    lower = jnp.where(x_c < x_r, 1.0, 0.0).astype(BF16)
    off_b = _dot(lower, n_up_b.astype(BF16))
    pos = rows_of(jnp.concatenate([off_b] * (tm // LANES), axis=1) + rank)
    metat_ref[...] = jnp.concatenate(
        [take(pos, e1), take(pos, e2), g1 / tot, g2 / tot, jnp.zeros((SUBLANES - 4, tm), F32)], axis=0)

    cnt = cnt_ref[...]
    tile_n_ref[...] = n_up_b.astype(jnp.int32)
    tile_cnt_ref[...] = cnt.astype(jnp.int32)
    cnt_ref[...] = cnt + n_up_b


def _project(x_tile, w_row_ref, b_row_ref, w_col_ref, b_col_ref, proj_ref, colp_ref):
    xb = x_tile.astype(BF16)
    proj_ref[...] = _dot(xb, w_row_ref[...]) + b_row_ref[...]
    colp_ref[...] = _dot(w_col_ref[...], x_tile.T.astype(BF16)) + b_col_ref[...]


def _mix_tile(x, proj_ref, colp_ref, mnw_ref, gnw_ref, gnb_ref, ws_ref, bs_ref, cw_ref, cb_ref, cnw_ref,
              cnb_ref, w_out_ref, l1w_ref, l1b_ref, hcat_ref, cstate_ref, mstate_ref, cbuf_ref, shift_ref, tm):
    n_chunks = tm // CHUNK
    gates = colp_ref[ML_WIDTH:P_COL, :]
    lane_in_chunk = lax.broadcasted_iota(jnp.int32, gates.shape, 1) % CHUNK
    logf_all = _log_sigmoid(gates)
    bcum = logf_all
    d = 1
    while d < CHUNK:
        bcum = bcum + jnp.where(lane_in_chunk >= d, pltpu.roll(bcum, d, 1), 0.0)
        d *= 2

    row_i = lax.broadcasted_iota(jnp.int32, (CHUNK, CHUNK), 0)
    col_i = lax.broadcasted_iota(jnp.int32, (CHUNK, CHUNK), 1)
    causal = col_i <= row_i
    diag = col_i == row_i
    ones_col = jnp.where(col_i == 0, 1.0, 0.0).astype(BF16)

    for sc in range(STREAMS * n_chunks):
        st, c = divmod(sc, n_chunks)
        r0 = st * tm + c * CHUNK
        rows = slice(r0, r0 + CHUNK)
        m_rows = slice(st * SUBLANES, st * SUBLANES + ML_HEADS)
        logi = gates[0:ML_HEADS, r0:r0 + CHUNK]
        logf = logf_all[ML_HEADS:2 * ML_HEADS, r0:r0 + CHUNK]
        b_row = bcum[ML_HEADS:2 * ML_HEADS, r0:r0 + CHUNK]
        g = jnp.sum(logf, axis=-1, keepdims=True)
        m_prev = mstate_ref[m_rows, 0:1]
        a_row = g - b_row + logi
        m_new = jnp.maximum(g + m_prev, jnp.max(a_row, axis=-1, keepdims=True))
        w_row = jnp.exp(a_row - m_new) * K_SCALE
        decay = jnp.exp(g + m_prev - m_new)
        mstate_ref[m_rows, :] = jnp.broadcast_to(m_new, (ML_HEADS, LANES))

        for h in range(ML_HEADS):
            hs = slice(h * ML_HEAD_DIM, (h + 1) * ML_HEAD_DIM)
            q = proj_ref[rows, Q_LO + h * ML_HEAD_DIM:Q_LO + (h + 1) * ML_HEAD_DIM].astype(BF16)
            v = proj_ref[rows, V_LO + h * ML_HEAD_DIM:V_LO + (h + 1) * ML_HEAD_DIM].astype(BF16)
            o_gate = proj_ref[rows, O_LO + h * ML_HEAD_DIM:O_LO + (h + 1) * ML_HEAD_DIM]
            kt = colp_ref[hs, r0:r0 + CHUNK]
            v_ext = jnp.concatenate([v, ones_col], axis=1)

            b_r = b_row[h:h + 1, :]
            b_c = jnp.sum(jnp.where(diag, b_r, 0.0), axis=-1, keepdims=True)
            log_d = jnp.where(causal, b_c - b_r + logi[h:h + 1, :], NEG_INF)
            mp = m_prev[h:h + 1, :]
            log_inter = b_c + mp
            m_row = jnp.maximum(log_inter, jnp.max(log_d, axis=-1, keepdims=True))
            c_ext = cstate_ref[st * ML_HEADS + h]
            from_q = _dot(q, jnp.concatenate([kt.astype(BF16), c_ext.astype(BF16)], axis=1))
            p = from_q[:, 0:CHUNK] * (jnp.exp(log_d - m_row) * K_SCALE)
            w_inter = jnp.exp(log_inter - m_row)
            ktw = (kt * w_row[h:h + 1, :]).astype(BF16)
            onto_v = _dot(jnp.concatenate([p.astype(BF16), ktw], axis=0), v_ext)
            tot = onto_v[0:CHUNK, :] + w_inter * from_q[:, CHUNK:]
            den = jnp.maximum(jnp.abs(tot[:, ML_HEAD_DIM:ML_HEAD_DIM + 1]), jnp.exp(-m_row))
            hh = tot[:, 0:ML_HEAD_DIM] / den
            cstate_ref[st * ML_HEADS + h] = decay[h:h + 1, :] * c_ext + onto_v[CHUNK:, :]

            mu = jnp.mean(hh, axis=-1, keepdims=True)
            hc = hh - mu
            var = jnp.mean(hc * hc, axis=-1, keepdims=True)
            hn = hc * lax.rsqrt(var + 1e-6) * mnw_ref[:, hs]
            hcat_ref[rows, hs] = (jax.nn.sigmoid(o_gate) * hn).astype(BF16)

        u = _gelu_tanh(proj_ref[rows, GU_LO:GU_LO + G_WIDTH])
        z = _layer_norm(_gelu_tanh(proj_ref[rows, GV_LO:GV_LO + G_WIDTH]), gnw_ref[...], gnb_ref[...], LN_EPS)
        lane_head = lax.broadcasted_iota(jnp.int32, (CHUNK, G_WIDTH), 1) // G_HEAD_DIM
        z_bd = jnp.concatenate(
            [jnp.where(lane_head == h, z, 0.0).astype(BF16) for h in range(G_HEADS)], axis=0)
        zs = _dot(ws_ref[...], z_bd) + bs_ref[...]
        hcat_ref[rows, ML_WIDTH:ML_WIDTH + G_WIDTH] = (u * zs).astype(BF16)

        ca = proj_ref[rows, CA_LO:CA_LO + C_WIDTH]
        cb = proj_ref[rows, CB_LO:CB_LO + C_WIDTH]
        cbuf_ref[st, CONV_HALO + c * CHUNK:CONV_HALO + (c + 1) * CHUNK, :] = ca * jax.nn.sigmoid(cb)

    first_tap = CONV_HALO - (CONV_WIDTH - 1)
    for st in range(STREAMS):
        for sh in range(1, SUBLANES):
            shift_ref[st, sh - 1] = cbuf_ref[st, sh:sh + tm + CONV_SPAN, :]
        for c in range(n_chunks):
            acc = jnp.zeros((CHUNK, C_WIDTH), F32) + cb_ref[...]
            for k in range(CONV_WIDTH):
                whole, sh = divmod(first_tap + k, SUBLANES)
                lo = c * CHUNK + whole * SUBLANES
                tap = cbuf_ref[st, lo:lo + CHUNK, :] if sh == 0 else shift_ref[st, sh - 1, lo:lo + CHUNK, :]
                acc = acc + cw_ref[k:k + 1, :] * tap
            cn = _layer_norm(acc, cnw_ref[...], cnb_ref[...], LN_EPS)
            r0 = st * tm + c * CHUNK
            hcat_ref[r0:r0 + CHUNK, ML_WIDTH + G_WIDTH:D_MODEL] = (cn * jax.nn.sigmoid(cn)).astype(BF16)
        cbuf_ref[st, 0:CONV_HALO, :] = cbuf_ref[st, tm:tm + CONV_HALO, :]

    y = _dot(hcat_ref[...], w_out_ref[...])
    x_new = _layer_norm(DEEPNORM_ALPHA * x + y, l1w_ref[...], l1b_ref[...], LN_EPS)
    return x_new


def _prepare_in_proj(w_in_ref, w_row_ref, w_col_ref):
    for r0 in range(0, D_MODEL, W_PREP_ROWS):
        rs = slice(r0, r0 + W_PREP_ROWS)
        w_row_ref[rs, Q_LO:V_LO] = w_in_ref[rs, IN_Q[0]:IN_Q[1]].astype(BF16)
        w_row_ref[rs, V_LO:GU_LO] = w_in_ref[rs, IN_VO[0]:IN_VO[1]].astype(BF16)
        w_row_ref[rs, GU_LO:P_ROW] = w_in_ref[rs, IN_GATES[0]:P_IN][:, IN_REST[0] - IN_GATES[0]:].astype(BF16)
        w_col_ref[0:ML_WIDTH, rs] = w_in_ref[rs, IN_K[0]:IN_K[1]].T.astype(BF16)
        gates_t = w_in_ref[rs, IN_GATES[0]:IN_GATES[0] + LANES].T
        keep = lax.broadcasted_iota(jnp.int32, (P_COL_PAD - ML_WIDTH, W_PREP_ROWS), 0) < 2 * ML_HEADS
        w_col_ref[ML_WIDTH:P_COL_PAD, rs] = jnp.where(keep, gates_t[0:P_COL_PAD - ML_WIDTH, :], 0.0).astype(BF16)


def _mixer_kernel(x_ref, w_in_ref, w_out_f32_ref, b_row_ref, b_col_ref, mnw_ref, gnw_ref, gnb_ref,
                  ws_ref, bs_ref, cw_ref, cb_ref, cnw_ref, cnb_ref, l1w_ref, l1b_ref, rw_ref, rb_ref,
                  o_ref, metat_ref, tile_n_ref, tile_cnt_ref,
                  w_row_ref, w_col_ref, w_out_ref, proj_ref, colp_ref, hcat_ref, cstate_ref, mstate_ref, cbuf_ref, shift_ref,
                  cnt_ref, *, tm, steps_per_seq):
    i = pl.program_id(0)

    @pl.when(i == 0)
    def _():
        cnt_ref[...] = jnp.zeros_like(cnt_ref)
        _prepare_in_proj(w_in_ref, w_row_ref, w_col_ref)
        for r0 in range(0, D_MODEL, W_PREP_ROWS):
            w_out_ref[r0:r0 + W_PREP_ROWS, :] = w_out_f32_ref[r0:r0 + W_PREP_ROWS, :].astype(BF16)

    @pl.when(i % steps_per_seq == 0)
    def _():
        cstate_ref[...] = jnp.zeros_like(cstate_ref)
        mstate_ref[...] = jnp.zeros_like(mstate_ref)
        cbuf_ref[:, 0:CONV_HALO, :] = jnp.zeros((STREAMS, CONV_HALO, C_WIDTH), F32)

    x = x_ref[...].reshape(STREAMS * tm, D_MODEL)
    _project(x, w_row_ref, b_row_ref, w_col_ref, b_col_ref, proj_ref, colp_ref)
    x_new = _mix_tile(x, proj_ref, colp_ref, mnw_ref, gnw_ref, gnb_ref, ws_ref, bs_ref, cw_ref, cb_ref, cnw_ref,
                      cnb_ref, w_out_ref, l1w_ref, l1b_ref, hcat_ref, cstate_ref, mstate_ref, cbuf_ref, shift_ref,
                      tm)
    o_ref[...] = x_new.reshape(STREAMS, tm, D_MODEL)
    _route_tile(_router_logits(x_new, rw_ref).T[0:N_EXPERTS, :], rb_ref, metat_ref, tile_n_ref, tile_cnt_ref, cnt_ref)


def _full(shape):
    nd = len(shape)
    return pl.BlockSpec(shape, lambda i, _nd=nd: (0,) * _nd, pipeline_mode=pl.Buffered(1))


def _layer_of(stacked, layer):
    nd = stacked.ndim - 1
    return pl.BlockSpec((None,) + stacked.shape[1:], lambda i, _nd=nd: (layer,) + (0,) * _nd,
                        pipeline_mode=pl.Buffered(1))


def _mixer_layer(layer, x, w_in, w_out, stacked_small, rw_split, rb_col):
    batch, seq, _ = x.shape
    tm = MIX_TM
    steps_per_seq = seq // tm
    n_steps = batch // STREAMS * steps_per_seq
    rows = STREAMS * tm
    kernel = functools.partial(_mixer_kernel, tm=tm, steps_per_seq=steps_per_seq)
    shared = (rw_split, rb_col)
    tile_spec = pl.BlockSpec((None, N_EXPERTS, LANES), lambda i: (i, 0, 0))
    step_spec = pl.BlockSpec((STREAMS, tm, D_MODEL), lambda i: (i // steps_per_seq, i % steps_per_seq, 0))
    return pl.pallas_call(
        kernel,
        out_shape=(
            jax.ShapeDtypeStruct(x.shape, F32),
            jax.ShapeDtypeStruct((n_steps, SUBLANES, rows), F32),
            jax.ShapeDtypeStruct((n_steps, N_EXPERTS, LANES), jnp.int32),
            jax.ShapeDtypeStruct((n_steps, N_EXPERTS, LANES), jnp.int32),
        ),
        grid=(n_steps,),
        in_specs=[step_spec] + [_layer_of(w, layer) for w in (w_in, w_out) + tuple(stacked_small)]
        + [_full(w.shape) for w in shared],
        out_specs=(step_spec, pl.BlockSpec((None, SUBLANES, rows), lambda i: (i, 0, 0)), tile_spec, tile_spec),
        scratch_shapes=[
            pltpu.VMEM((D_MODEL, P_ROW), BF16),
            pltpu.VMEM((P_COL_PAD, D_MODEL), BF16),
            pltpu.VMEM((D_MODEL, D_MODEL), BF16),
            pltpu.VMEM((rows, P_ROW), F32),
            pltpu.VMEM((P_COL_PAD, rows), F32),
            pltpu.VMEM((rows, D_MODEL), BF16),
            pltpu.VMEM((STREAMS * ML_HEADS, ML_HEAD_DIM, 2 * ML_HEAD_DIM), F32),
            pltpu.VMEM((STREAMS * SUBLANES, LANES), F32),
            pltpu.VMEM((STREAMS, tm + CONV_HALO, C_WIDTH), F32),
            pltpu.VMEM((STREAMS, SUBLANES - 1, tm + CONV_SPAN, C_WIDTH), F32),
            pltpu.VMEM((N_EXPERTS, LANES), F32),
        ],
        compiler_params=pltpu.CompilerParams(
            dimension_semantics=("arbitrary",), vmem_limit_bytes=VMEM_LIMIT),
        name="mixer",
    )(x, w_in, w_out, *stacked_small, *shared)


def _pow2_pieces(n, largest, act):
    piece = largest
    while piece >= ROW_ALIGN:
        start = jnp.bitwise_and(n, -2 * piece)

        @pl.when(jnp.bitwise_and(n, piece) != 0)
        def _(piece=piece, start=start):
            act(start, piece)
        piece //= 2


def _for_each_run(n_tab, cnt_tab, base_tab, tile, make_copy, act):
    def body(e, off):
        n = n_tab[tile * N_EXPERTS + e]
        base = base_tab[e] + cnt_tab[tile * N_EXPERTS + e]
        _pow2_pieces(n, MOE_TM, lambda start, size: act(make_copy(
            pl.multiple_of(off + start, ROW_ALIGN), pl.multiple_of(base + start, ROW_ALIGN), size)))
        return off + n

    lax.fori_loop(0, N_EXPERTS, body, jnp.int32(0))


def _dispatch_kernel(n_tab, cnt_tab, base_tab, fill_tab, x_ref, metat_ref, xs_hbm, comp_ref, zero_ref, sem,
                     *, n_rows):
    i = pl.program_id(0)
    last = pl.num_programs(0) - 1
    slot = lax.rem(i, 2)

    def copies(tile, slot_, act):
        def make_copy(src_row, dst_row, rows):
            return pltpu.make_async_copy(comp_ref.at[slot_, pl.ds(src_row, rows)],
                                         xs_hbm.at[pl.ds(dst_row, rows)], sem.at[slot_])
        _for_each_run(n_tab, cnt_tab, base_tab, tile, make_copy, act)

    def zero_fill(act):
        def zero_copy(dst_row, rows):
            return pltpu.make_async_copy(zero_ref.at[pl.ds(0, rows)],
                                         xs_hbm.at[pl.ds(pl.multiple_of(dst_row, ROW_ALIGN), rows)], sem.at[2])

        def per_expert(e, carry):
            first = fill_tab[2 * e]
            _pow2_pieces(fill_tab[2 * e + 1], ROW_TILE // 2, lambda start, size: act(zero_copy(first + start, size)))
            return carry

        lax.fori_loop(0, N_EXPERTS, per_expert, jnp.int32(0))

        def per_half_tile(k, carry):
            act(zero_copy(fill_tab[2 * N_EXPERTS] + k * (ROW_TILE // 2), ROW_TILE // 2))
            return carry

        lax.fori_loop(0, (n_rows - fill_tab[2 * N_EXPERTS]) // (ROW_TILE // 2), per_half_tile, jnp.int32(0))

    @pl.when(i == 0)
    def _():
        zero_ref[...] = jnp.zeros_like(zero_ref)
        zero_fill(lambda cp: cp.start())

    @pl.when(i >= 2)
    def _():
        copies(i - 2, slot, lambda cp: cp.wait())

    mt = metat_ref[...]
    row_f = lax.broadcasted_iota(jnp.int32, (COMPACT_ROWS, MOE_TM), 0).astype(F32)
    hit1 = row_f == mt[0:1, :]
    hit2 = row_f == mt[1:2, :]
    onehot = jnp.where(hit1, 1.0, jnp.where(hit2, 1.0, 0.0)).astype(BF16)
    x_tile = x_ref[...].reshape(MOE_TM, D_MODEL).astype(BF16)
    comp_ref[slot, :, 0:D_MODEL] = _dot(onehot, x_tile).astype(BF16)
    gate = jnp.sum(jnp.where(hit1, mt[2:3, :], jnp.where(hit2, mt[3:4, :], 0.0)), axis=-1, keepdims=True)
    lane = lax.broadcasted_iota(jnp.int32, (COMPACT_ROWS, LANES), 1)
    packed, rest = jnp.zeros((COMPACT_ROWS, LANES), F32), gate
    for piece in range(GATE_PIECES):
        head = rest.astype(BF16).astype(F32)
        packed = jnp.where(lane == piece, head, packed)
        rest = rest - head
    comp_ref[slot, :, D_MODEL:PAYLOAD_W] = packed.astype(BF16)
    copies(i, slot, lambda cp: cp.start())

    @pl.when(i == last)
    def _():
        @pl.when(i >= 1)
        def _():
            copies(i - 1, 1 - slot, lambda cp: cp.wait())
        copies(i, slot, lambda cp: cp.wait())
        zero_fill(lambda cp: cp.wait())


def _token_step_spec(seq):
    tiles_per_seq = seq // MIX_TM
    return pl.BlockSpec((STREAMS, MIX_TM, D_MODEL), lambda i, *_: (i // tiles_per_seq, i % tiles_per_seq, 0))


_ROUTE_STEP_SPEC = pl.BlockSpec((None, SUBLANES, MOE_TM), lambda i, *_: (i, 0, 0))


def _dispatch_layer(n_tab, cnt_tab, base_tab, fill_tab, x, metat, n_rows):
    batch, seq, _ = x.shape
    return pl.pallas_call(
        functools.partial(_dispatch_kernel, n_rows=n_rows),
        out_shape=jax.ShapeDtypeStruct((n_rows, PAYLOAD_W), BF16),
        grid_spec=pltpu.PrefetchScalarGridSpec(
            num_scalar_prefetch=4,
            grid=(batch * seq // MOE_TM,),
            in_specs=[_token_step_spec(seq), _ROUTE_STEP_SPEC],
            out_specs=pl.BlockSpec(memory_space=pl.ANY),
            scratch_shapes=[
                pltpu.VMEM((2, COMPACT_ROWS, PAYLOAD_W), BF16),
                pltpu.VMEM((ROW_TILE // 2, PAYLOAD_W), BF16),
                pltpu.SemaphoreType.DMA((3,)),
            ],
        ),
        compiler_params=pltpu.CompilerParams(dimension_semantics=("arbitrary",), vmem_limit_bytes=VMEM_LIMIT),
        name="dispatch",
    )(n_tab, cnt_tab, base_tab, fill_tab, x, metat)


def _expert_kernel(tile_e, tile_valid, xs_ref, wg_ref, wu_ref, wd_ref, ys_ref, wgu_ref, wdb_ref):
    i = pl.program_id(0)
    valid = tile_valid[i]
    new_expert = jnp.logical_or(i == 0, tile_e[i] != tile_e[jnp.maximum(i - 1, 0)])

    @pl.when(jnp.logical_and(valid > 0, new_expert))
    def _():
        wgu_ref[:, 0:D_EXPERT] = wg_ref[...].astype(BF16)
        wgu_ref[:, D_EXPERT:2 * D_EXPERT] = wu_ref[...].astype(BF16)
        wdb_ref[...] = wd_ref[...].astype(BF16)

    @pl.when(valid > 0)
    def _():
        blk = ROW_TILE // EXPERT_ROW_BLOCKS
        rows = [slice(b * blk, (b + 1) * blk) for b in range(EXPERT_ROW_BLOCKS)]
        gu = [_dot(xs_ref[r, 0:D_MODEL], wgu_ref[...]) for r in rows]
        for r, gu_b in zip(rows, gu):
            g, u = gu_b[:, 0:D_EXPERT], gu_b[:, D_EXPERT:2 * D_EXPERT]
            hid = (g * jax.nn.sigmoid(g) * u).astype(BF16)
            gate = jnp.sum(xs_ref[r, D_MODEL:PAYLOAD_W].astype(F32), axis=-1, keepdims=True)
            ys_ref[r, :] = (gate * _dot(hid, wdb_ref[...])).astype(BF16)

    @pl.when(valid == 0)
    def _():
        ys_ref[...] = jnp.zeros_like(ys_ref)


def _expert_layer(layer, tile_e, tile_valid, xs, wg, wu, wd):
    n_steps = tile_e.shape[0]
    return pl.pallas_call(
        _expert_kernel,
        out_shape=jax.ShapeDtypeStruct((xs.shape[0], D_MODEL), BF16),
        grid_spec=pltpu.PrefetchScalarGridSpec(
            num_scalar_prefetch=2,
            grid=(n_steps,),
            in_specs=[
                pl.BlockSpec((ROW_TILE, PAYLOAD_W), lambda i, e, v: (i, 0)),
                pl.BlockSpec((None, None, D_MODEL, D_EXPERT), lambda i, e, v: (layer, e[i], 0, 0)),
                pl.BlockSpec((None, None, D_MODEL, D_EXPERT), lambda i, e, v: (layer, e[i], 0, 0)),
                pl.BlockSpec((None, None, D_EXPERT, D_MODEL), lambda i, e, v: (layer, e[i], 0, 0)),
            ],
            out_specs=pl.BlockSpec((ROW_TILE, D_MODEL), lambda i, e, v: (i, 0)),
            scratch_shapes=[
                pltpu.VMEM((D_MODEL, 2 * D_EXPERT), BF16),
                pltpu.VMEM((D_EXPERT, D_MODEL), BF16),
            ],
        ),
        compiler_params=pltpu.CompilerParams(dimension_semantics=("arbitrary",), vmem_limit_bytes=VMEM_LIMIT),
        name="experts",
    )(tile_e, tile_valid, xs, wg, wu, wd)


def _combine_kernel(n_tab, cnt_tab, base_tab, x_ref, metat_ref, ys_hbm, l2w_ref, l2b_ref, o_ref, yc_ref, sem):
    i = pl.program_id(0)
    n_steps = pl.num_programs(0)
    slot = lax.rem(i, 2)

    def copies(tile, slot_, act):
        def make_copy(buf_row, ys_row, rows):
            return pltpu.make_async_copy(ys_hbm.at[pl.ds(ys_row, rows)],
                                         yc_ref.at[slot_, pl.ds(buf_row, rows)], sem.at[slot_])
        _for_each_run(n_tab, cnt_tab, base_tab, tile, make_copy, act)

    @pl.when(i == 0)
    def _():
        yc_ref[...] = jnp.zeros_like(yc_ref)
        copies(0, 0, lambda cp: cp.start())

    @pl.when(i + 1 < n_steps)
    def _():
        copies(i + 1, 1 - slot, lambda cp: cp.start())

    copies(i, slot, lambda cp: cp.wait())

    t_r = lax.broadcasted_iota(jnp.int32, (MOE_TM, MOE_TM), 0)
    t_c = lax.broadcasted_iota(jnp.int32, (MOE_TM, MOE_TM), 1)
    as_col = lambda row: jnp.sum(jnp.where(t_r == t_c, row, 0.0), axis=-1, keepdims=True)
    col_f = lax.broadcasted_iota(jnp.int32, (MOE_TM, COMPACT_ROWS), 1).astype(F32)
    mt = metat_ref[...]
    onehot = jnp.where(col_f == as_col(mt[0:1, :]), 1.0,
                       jnp.where(col_f == as_col(mt[1:2, :]), 1.0, 0.0)).astype(BF16)
    y = _dot(onehot, yc_ref[slot])
    x = x_ref[...].reshape(MOE_TM, D_MODEL)
    o_ref[...] = _layer_norm(DEEPNORM_ALPHA * x + y, l2w_ref[...], l2b_ref[...], LN_EPS).reshape(o_ref.shape)


def _combine_layer(layer, n_tab, cnt_tab, base_tab, x, metat, ys, l2w, l2b):
    batch, seq, _ = x.shape
    ln_spec = pl.BlockSpec((None, 1, D_MODEL), lambda i, *_: (layer, 0, 0))
    return pl.pallas_call(
        _combine_kernel,
        out_shape=jax.ShapeDtypeStruct(x.shape, F32),
        grid_spec=pltpu.PrefetchScalarGridSpec(
            num_scalar_prefetch=3,
            grid=(batch * seq // MOE_TM,),
            in_specs=[
                _token_step_spec(seq),
                _ROUTE_STEP_SPEC,
                pl.BlockSpec(memory_space=pl.ANY),
                ln_spec,
                ln_spec,
            ],
            out_specs=_token_step_spec(seq),
            scratch_shapes=[pltpu.VMEM((2, COMPACT_ROWS, D_MODEL), BF16), pltpu.SemaphoreType.DMA((2,))],
        ),
        compiler_params=pltpu.CompilerParams(dimension_semantics=("arbitrary",), vmem_limit_bytes=VMEM_LIMIT),
        name="combine",
    )(n_tab, cnt_tab, base_tab, x, metat, ys, l2w, l2b)


def _expert_plan(counts, n_steps):
    tiles_e = (counts + (ROW_TILE - 1)) // ROW_TILE
    cum = jnp.cumsum(tiles_e)
    first_tile = cum - tiles_e
    total = cum[-1]
    step = jnp.arange(n_steps, dtype=jnp.int32)
    owner = (step[:, None] >= cum[None, :]).sum(axis=1).astype(jnp.int32)
    is_owner = owner[:, None] == jnp.arange(N_EXPERTS, dtype=jnp.int32)[None, :]
    local = step - jnp.where(is_owner, first_tile[None, :], 0).sum(axis=1)
    rows_left = jnp.where(is_owner, counts[None, :], 0).sum(axis=1) - local * ROW_TILE
    valid = jnp.where(step < total, jnp.clip(rows_left, 0, ROW_TILE), 0).astype(jnp.int32)
    tile_e = jnp.minimum(owner, N_EXPERTS - 1)
    base = (first_tile * ROW_TILE).astype(jnp.int32)
    fill = jnp.stack([base + counts, tiles_e * ROW_TILE - counts], axis=1).reshape(-1)
    fill_tab = jnp.concatenate([fill, (total * ROW_TILE)[None]]).astype(jnp.int32)
    return base, tile_e, valid, fill_tab


def _moe_layer(layer, x, metat, tile_n, tile_cnt, wg, wu, wd, l2w, l2b):
    n_tok = x.shape[0] * x.shape[1]
    n_tiles = n_tok // MOE_TM
    n_steps = (2 * n_tok + N_EXPERTS * (ROW_ALIGN - 1) * n_tiles) // ROW_TILE + N_EXPERTS
    n_tab = tile_n[:, :, 0].reshape(-1)
    cnt_tab = tile_cnt[:, :, 0].reshape(-1)
    counts = tile_cnt[-1, :, 0] + tile_n[-1, :, 0]
    base_tab, tile_e, tile_valid, fill_tab = _expert_plan(counts, n_steps)
    xs = _dispatch_layer(n_tab, cnt_tab, base_tab, fill_tab, x, metat, n_steps * ROW_TILE)
    ys = _expert_layer(layer, tile_e, tile_valid, xs, wg, wu, wd)
    return _combine_layer(layer, n_tab, cnt_tab, base_tab, x, metat, ys, l2w, l2b)


def kernel(x, w_in, b_in, mlstm_norm_w, gmlp_norm_w, gmlp_norm_b, gmlp_ws, gmlp_bs, conv_w, conv_b,
           conv_norm_w, conv_norm_b, w_out, ln1_w, ln1_b, router_w, router_b, w_gate, w_up, w_down,
           ln2_w, ln2_b):
    batch, seq, d = x.shape

    q_lo, k_lo, v_lo = 0, ML_WIDTH, 2 * ML_WIDTH
    gate_lo = 4 * ML_WIDTH
    rest_lo = gate_lo + 2 * ML_HEADS

    def row_part(t):
        return jnp.concatenate([t[..., q_lo:k_lo], t[..., v_lo:gate_lo], t[..., rest_lo:]], axis=-1)

    def col_part(t):
        return jnp.concatenate([t[..., k_lo:v_lo], t[..., gate_lo:rest_lo]], axis=-1)

    tril = jnp.tril(jnp.ones((CHUNK, CHUNK), gmlp_ws.dtype))
    rw_pad = jnp.pad(router_w.astype(F32), ((0, 0), (0, LANES - N_EXPERTS)))
    rw_head = rw_pad.astype(BF16)
    rw_split = jnp.concatenate([rw_head, (rw_pad - rw_head.astype(F32)).astype(BF16)], axis=1)
    rb_col = router_b.astype(F32).reshape(N_EXPERTS, 1)

    rows_of = lambda t: t.reshape(DEPTH, 1, -1)
    b_row = rows_of(row_part(b_in))
    b_col = jnp.pad(col_part(b_in), ((0, 0), (0, P_COL_PAD - P_COL))).reshape(DEPTH, P_COL_PAD, 1)
    ws_cat = jnp.transpose(gmlp_ws * tril, (0, 2, 1, 3)).reshape(DEPTH, CHUNK, G_HEADS * CHUNK).astype(BF16)
    bs_full = jnp.repeat(jnp.transpose(gmlp_bs, (0, 2, 1)), G_HEAD_DIM, axis=2)
    cw = jnp.pad(conv_w, ((0, 0), (0, 1), (0, 0)))
    mixer_small = (b_row, b_col, rows_of(mlstm_norm_w), rows_of(gmlp_norm_w), rows_of(gmlp_norm_b), ws_cat, bs_full,
                   cw, rows_of(conv_b), rows_of(conv_norm_w), rows_of(conv_norm_b), rows_of(ln1_w), rows_of(ln1_b))

    for l in range(DEPTH):
        x, metat, tile_n, tile_cnt = _mixer_layer(l, x, w_in, w_out, mixer_small, rw_split, rb_col)
        x = _moe_layer(l, x, metat, tile_n, tile_cnt, w_gate, w_up, w_down, rows_of(ln2_w), rows_of(ln2_b))
    return x
```

```python
import functools

import jax
import jax.numpy as jnp
from jax import lax
from jax.experimental import pallas as pl
from jax.experimental.pallas import tpu as pltpu

D_MODEL = 1024
DEPTH = 4
ML_WIDTH = 512
ML_HEADS = 4
ML_HEAD_DIM = 128
CHUNK = 128
G_WIDTH = 256
G_HEADS = 4
G_HEAD_DIM = 64
C_WIDTH = 256
CONV_WIDTH = 31
N_EXPERTS = 16
N_GROUPS = 4
EXPERTS_PER_GROUP = 4
D_EXPERT = 512
DEEPNORM_ALPHA = (2.0 * DEPTH) ** 0.25
LN_EPS = 1e-5
K_SCALE = ML_HEAD_DIM ** -0.5

Q_LO, V_LO, O_LO, GU_LO, GV_LO, CA_LO, CB_LO = 0, 512, 1024, 1536, 1792, 2048, 2304
P_ROW = 2560
P_COL = ML_WIDTH + 2 * ML_HEADS
P_COL_PAD = ML_WIDTH + 16
IN_Q, IN_K, IN_VO, IN_GATES, IN_REST = (0, 512), (512, 1024), (1024, 2048), (2048, 2056), (2056, 3080)
P_IN = 3080
W_PREP_ROWS = 256

LANES = 128
SUBLANES = 8
MIX_TM = 256
ROW_ALIGN = 16
ROW_TILE = 512
EXPERT_ROW_BLOCKS = 2
PAYLOAD_W = D_MODEL + LANES
CONV_HALO = 32
CONV_SPAN = CONV_HALO - SUBLANES
STREAMS = 2
MOE_TM = STREAMS * MIX_TM
COMPACT_ROWS = 2 * MOE_TM + 2 * LANES
assert COMPACT_ROWS >= 2 * MOE_TM + N_EXPERTS * (ROW_ALIGN - 1)
GATE_PIECES = 3
VMEM_LIMIT = 58 * 1024 * 1024

F32 = jnp.float32
BF16 = jnp.bfloat16
NEG_INF = float("-inf")


def _layer_norm(x, w, b, eps):
    mu = jnp.mean(x, axis=-1, keepdims=True)
    xc = x - mu
    var = jnp.mean(xc * xc, axis=-1, keepdims=True)
    return xc * lax.rsqrt(var + eps) * w + b


def _gelu_tanh(x):
    return 0.5 * x * (1.0 + jnp.tanh(0.7978845608028654 * (x + 0.044715 * (x * x * x))))


def _log_sigmoid(x):
    return jnp.minimum(x, 0.0) - jnp.log1p(jnp.exp(-jnp.abs(x)))


def _dot(a, b):
    return jnp.dot(a, b, preferred_element_type=F32)


def _router_logits(x_new, rw_ref):
    xh = x_new.astype(BF16)
    xl = (x_new - xh.astype(F32)).astype(BF16)
    head = _dot(xh, rw_ref[...])
    return head[:, 0:LANES] + head[:, LANES:2 * LANES] + _dot(xl, rw_ref[:, 0:LANES])


def _route_tile(logits, rb_ref, metat_ref, tile_n_ref, tile_cnt_ref, cnt_ref):
    tm = logits.shape[1]
    s_all = jax.nn.sigmoid(logits)
    sel_all = s_all + rb_ref[...]
    rows_of = lambda v: [v[k:k + 1, :] for k in range(N_EXPERTS)]
    s, sel = rows_of(s_all), rows_of(sel_all)

    best = None
    for g in range(N_GROUPS):
        r = sel[EXPERTS_PER_GROUP * g:EXPERTS_PER_GROUP * (g + 1)]
        hi01, lo01 = jnp.maximum(r[0], r[1]), jnp.minimum(r[0], r[1])
        hi23, lo23 = jnp.maximum(r[2], r[3]), jnp.minimum(r[2], r[3])
        score = jnp.maximum(hi01, hi23) + jnp.maximum(jnp.minimum(hi01, hi23), jnp.maximum(lo01, lo23))
        if best is None:
            best, gidx = score, jnp.zeros(score.shape, jnp.int32)
        else:
            better = score > best
            gidx = jnp.where(better, g, gidx)
            best = jnp.where(better, score, best)

    def of_group(rows):
        out = []
        for j in range(EXPERTS_PER_GROUP):
            v = rows[j]
            for g in range(1, N_GROUPS):
                v = jnp.where(gidx == g, rows[EXPERTS_PER_GROUP * g + j], v)
            out.append(v)
        return out

    def first_max(vals):
        best_v, best_j = vals[0], jnp.zeros(vals[0].shape, F32)
        for j in range(1, len(vals)):
            better = vals[j] > best_v
            best_j = jnp.where(better, float(j), best_j)
            best_v = jnp.where(better, vals[j], best_v)
        return best_j

    def take(vals, idx):
        v = vals[0]
        for j in range(1, len(vals)):
            v = jnp.where(idx == float(j), vals[j], v)
        return v

    cand, cand_s = of_group(sel), of_group(s)
    j1 = first_max(cand)
    j2 = first_max([jnp.where(j1 == float(j), NEG_INF, cand[j]) for j in range(EXPERTS_PER_GROUP)])
    g1, g2 = take(cand_s, j1), take(cand_s, j2)
    tot = g1 + g2
    first_of_group = gidx.astype(F32) * float(EXPERTS_PER_GROUP)
    e1, e2 = first_of_group + j1, first_of_group + j2

    e_f = lax.broadcasted_iota(jnp.int32, (N_EXPERTS, tm), 0).astype(F32)
    assign = jnp.where(e_f == e1, 1.0, jnp.where(e_f == e2, 1.0, 0.0)).astype(BF16)
    t_r = lax.broadcasted_iota(jnp.int32, (tm, tm), 0)
    t_c = lax.broadcasted_iota(jnp.int32, (tm, tm), 1)
    earlier = jnp.where(t_r < t_c, 1.0, 0.0).astype(BF16)
    rank = _dot(assign, earlier)
    n_b = _dot(assign, jnp.ones((tm, LANES), BF16))
    n_up_b = jnp.floor((n_b + (ROW_ALIGN - 1.0)) * (1.0 / ROW_ALIGN)) * ROW_ALIGN
    x_r = lax.broadcasted_iota(jnp.int32, (N_EXPERTS, N_EXPERTS), 0)
    x_c = lax.broadcasted_iota(jnp.int32, (N_EXPERTS, N_EXPERTS), 1)
    lower = jnp.where(x_c < x_r, 1.0, 0.0).astype(BF16)
    off_b = _dot(lower, n_up_b.astype(BF16))
    pos = rows_of(jnp.concatenate([off_b] * (tm // LANES), axis=1) + rank)
    metat_ref[...] = jnp.concatenate(
        [take(pos, e1), take(pos, e2), g1 / tot, g2 / tot, jnp.zeros((SUBLANES - 4, tm), F32)], axis=0)

    cnt = cnt_ref[...]
    tile_n_ref[...] = n_up_b.astype(jnp.int32)
    tile_cnt_ref[...] = cnt.astype(jnp.int32)
    cnt_ref[...] = cnt + n_up_b


def _project(x_tile, w_row_ref, b_row_ref, w_col_ref, b_col_ref, proj_ref, colp_ref):
    xb = x_tile.astype(BF16)
    proj_ref[...] = _dot(xb, w_row_ref[...]) + b_row_ref[...]
    colp_ref[...] = _dot(w_col_ref[...], x_tile.T.astype(BF16)) + b_col_ref[...]


def _mix_tile(x, proj_ref, colp_ref, mnw_ref, gnw_ref, gnb_ref, ws_ref, bs_ref, cw_ref, cb_ref, cnw_ref,
              cnb_ref, w_out_ref, l1w_ref, l1b_ref, hcat_ref, cstate_ref, mstate_ref, cbuf_ref, shift_ref, tm):
    n_chunks = tm // CHUNK
    gates = colp_ref[ML_WIDTH:P_COL, :]
    lane_in_chunk = lax.broadcasted_iota(jnp.int32, gates.shape, 1) % CHUNK
    logf_all = _log_sigmoid(gates)
    bcum = logf_all
    d = 1
    while d < CHUNK:
        bcum = bcum + jnp.where(lane_in_chunk >= d, pltpu.roll(bcum, d, 1), 0.0)
        d *= 2

    row_i = lax.broadcasted_iota(jnp.int32, (CHUNK, CHUNK), 0)
    col_i = lax.broadcasted_iota(jnp.int32, (CHUNK, CHUNK), 1)
    causal = col_i <= row_i
    diag = col_i == row_i
    ones_col = jnp.where(col_i == 0, 1.0, 0.0).astype(BF16)

    for sc in range(STREAMS * n_chunks):
        st, c = divmod(sc, n_chunks)
        r0 = st * tm + c * CHUNK
        rows = slice(r0, r0 + CHUNK)
        m_rows = slice(st * SUBLANES, st * SUBLANES + ML_HEADS)
        logi = gates[0:ML_HEADS, r0:r0 + CHUNK]
        logf = logf_all[ML_HEADS:2 * ML_HEADS, r0:r0 + CHUNK]
        b_row = bcum[ML_HEADS:2 * ML_HEADS, r0:r0 + CHUNK]
        g = jnp.sum(logf, axis=-1, keepdims=True)
        m_prev = mstate_ref[m_rows, 0:1]
        a_row = g - b_row + logi
        m_new = jnp.maximum(g + m_prev, jnp.max(a_row, axis=-1, keepdims=True))
        w_row = jnp.exp(a_row - m_new) * K_SCALE
        decay = jnp.exp(g + m_prev - m_new)
        mstate_ref[m_rows, :] = jnp.broadcast_to(m_new, (ML_HEADS, LANES))

        for h in range(ML_HEADS):
            hs = slice(h * ML_HEAD_DIM, (h + 1) * ML_HEAD_DIM)
            q = proj_ref[rows, Q_LO + h * ML_HEAD_DIM:Q_LO + (h + 1) * ML_HEAD_DIM].astype(BF16)
            v = proj_ref[rows, V_LO + h * ML_HEAD_DIM:V_LO + (h + 1) * ML_HEAD_DIM].astype(BF16)
            o_gate = proj_ref[rows, O_LO + h * ML_HEAD_DIM:O_LO + (h + 1) * ML_HEAD_DIM]
            kt = colp_ref[hs, r0:r0 + CHUNK]
            v_ext = jnp.concatenate([v, ones_col], axis=1)

            b_r = b_row[h:h + 1, :]
            b_c = jnp.sum(jnp.where(diag, b_r, 0.0), axis=-1, keepdims=True)
            log_d = jnp.where(causal, b_c - b_r + logi[h:h + 1, :], NEG_INF)
            mp = m_prev[h:h + 1, :]
            log_inter = b_c + mp
            m_row = jnp.maximum(log_inter, jnp.max(log_d, axis=-1, keepdims=True))
            c_ext = cstate_ref[st * ML_HEADS + h]
            from_q = _dot(q, jnp.concatenate([kt.astype(BF16), c_ext.astype(BF16)], axis=1))
            p = from_q[:, 0:CHUNK] * (jnp.exp(log_d - m_row) * K_SCALE)
            w_inter = jnp.exp(log_inter - m_row)
            ktw = (kt * w_row[h:h + 1, :]).astype(BF16)
            onto_v = _dot(jnp.concatenate([p.astype(BF16), ktw], axis=0), v_ext)
            tot = onto_v[0:CHUNK, :] + w_inter * from_q[:, CHUNK:]
            den = jnp.maximum(jnp.abs(tot[:, ML_HEAD_DIM:ML_HEAD_DIM + 1]), jnp.exp(-m_row))
            hh = tot[:, 0:ML_HEAD_DIM] / den
            cstate_ref[st * ML_HEADS + h] = decay[h:h + 1, :] * c_ext + onto_v[CHUNK:, :]

            mu = jnp.mean(hh, axis=-1, keepdims=True)
            hc = hh - mu
            var = jnp.mean(hc * hc, axis=-1, keepdims=True)
            hn = hc * lax.rsqrt(var + 1e-6) * mnw_ref[:, hs]
            hcat_ref[rows, hs] = (jax.nn.sigmoid(o_gate) * hn).astype(BF16)

        u = _gelu_tanh(proj_ref[rows, GU_LO:GU_LO + G_WIDTH])
        z = _layer_norm(_gelu_tanh(proj_ref[rows, GV_LO:GV_LO + G_WIDTH]), gnw_ref[...], gnb_ref[...], LN_EPS)
        lane_head = lax.broadcasted_iota(jnp.int32, (CHUNK, G_WIDTH), 1) // G_HEAD_DIM
        z_bd = jnp.concatenate(
            [jnp.where(lane_head == h, z, 0.0).astype(BF16) for h in range(G_HEADS)], axis=0)
        zs = _dot(ws_ref[...], z_bd) + bs_ref[...]
        hcat_ref[rows, ML_WIDTH:ML_WIDTH + G_WIDTH] = (u * zs).astype(BF16)

        ca = proj_ref[rows, CA_LO:CA_LO + C_WIDTH]
        cb = proj_ref[rows, CB_LO:CB_LO + C_WIDTH]
        cbuf_ref[st, CONV_HALO + c * CHUNK:CONV_HALO + (c + 1) * CHUNK, :] = ca * jax.nn.sigmoid(cb)

    first_tap = CONV_HALO - (CONV_WIDTH - 1)
    for st in range(STREAMS):
        for sh in range(1, SUBLANES):
            shift_ref[st, sh - 1] = cbuf_ref[st, sh:sh + tm + CONV_SPAN, :]
        for c in range(n_chunks):
            acc = jnp.zeros((CHUNK, C_WIDTH), F32) + cb_ref[...]
            for k in range(CONV_WIDTH):
                whole, sh = divmod(first_tap + k, SUBLANES)
                lo = c * CHUNK + whole * SUBLANES
                tap = cbuf_ref[st, lo:lo + CHUNK, :] if sh == 0 else shift_ref[st, sh - 1, lo:lo + CHUNK, :]
                acc = acc + cw_ref[k:k + 1, :] * tap
            cn = _layer_norm(acc, cnw_ref[...], cnb_ref[...], LN_EPS)
            r0 = st * tm + c * CHUNK
            hcat_ref[r0:r0 + CHUNK, ML_WIDTH + G_WIDTH:D_MODEL] = (cn * jax.nn.sigmoid(cn)).astype(BF16)
        cbuf_ref[st, 0:CONV_HALO, :] = cbuf_ref[st, tm:tm + CONV_HALO, :]

    y = _dot(hcat_ref[...], w_out_ref[...])
    x_new = _layer_norm(DEEPNORM_ALPHA * x + y, l1w_ref[...], l1b_ref[...], LN_EPS)
    return x_new


def _prepare_in_proj(w_t_ref, w_row_ref, w_col_ref):
    for (src_lo, src_hi), dst_lo in ((IN_Q, Q_LO), (IN_VO, V_LO), (IN_REST, GU_LO)):
        for off in range(0, src_hi - src_lo, W_PREP_ROWS):
            block = w_t_ref[src_lo + off:src_lo + off + W_PREP_ROWS, :]
            w_row_ref[:, dst_lo + off:dst_lo + off + W_PREP_ROWS] = block.T.astype(BF16)
    w_col_ref[0:ML_WIDTH, :] = w_t_ref[IN_K[0]:IN_K[1], :].astype(BF16)
    gates = w_t_ref[IN_GATES[0]:IN_GATES[1], :]
    w_col_ref[ML_WIDTH:P_COL_PAD, :] = jnp.concatenate(
        [gates, jnp.zeros((P_COL_PAD - P_COL, D_MODEL), F32)], axis=0).astype(BF16)


def _mixer_kernel(x_ref, w_in_t_ref, w_out_f32_ref, b_row_ref, b_col_ref, mnw_ref, gnw_ref, gnb_ref,
                  ws_ref, bs_ref, cw_ref, cb_ref, cnw_ref, cnb_ref, l1w_ref, l1b_ref, rw_ref, rb_ref,
                  o_ref, metat_ref, tile_n_ref, tile_cnt_ref,
                  w_row_ref, w_col_ref, w_out_ref, proj_ref, colp_ref, hcat_ref, cstate_ref, mstate_ref, cbuf_ref, shift_ref,
                  cnt_ref, *, tm, steps_per_seq):
    i = pl.program_id(0)

    @pl.when(i == 0)
    def _():
        cnt_ref[...] = jnp.zeros_like(cnt_ref)
        _prepare_in_proj(w_in_t_ref, w_row_ref, w_col_ref)
        for r0 in range(0, D_MODEL, W_PREP_ROWS):
            w_out_ref[r0:r0 + W_PREP_ROWS, :] = w_out_f32_ref[r0:r0 + W_PREP_ROWS, :].astype(BF16)

    @pl.when(i % steps_per_seq == 0)
    def _():
        cstate_ref[...] = jnp.zeros_like(cstate_ref)
        mstate_ref[...] = jnp.zeros_like(mstate_ref)
        cbuf_ref[:, 0:CONV_HALO, :] = jnp.zeros((STREAMS, CONV_HALO, C_WIDTH), F32)

    x = x_ref[...].reshape(STREAMS * tm, D_MODEL)
    _project(x, w_row_ref, b_row_ref, w_col_ref, b_col_ref, proj_ref, colp_ref)
    x_new = _mix_tile(x, proj_ref, colp_ref, mnw_ref, gnw_ref, gnb_ref, ws_ref, bs_ref, cw_ref, cb_ref, cnw_ref,
                      cnb_ref, w_out_ref, l1w_ref, l1b_ref, hcat_ref, cstate_ref, mstate_ref, cbuf_ref, shift_ref,
                      tm)
    o_ref[...] = x_new.reshape(STREAMS, tm, D_MODEL)
    _route_tile(_router_logits(x_new, rw_ref).T[0:N_EXPERTS, :], rb_ref, metat_ref, tile_n_ref, tile_cnt_ref, cnt_ref)


def _full(shape):
    nd = len(shape)
    return pl.BlockSpec(shape, lambda i, _nd=nd: (0,) * _nd, pipeline_mode=pl.Buffered(1))


def _layer_of(stacked, layer):
    nd = stacked.ndim - 1
    return pl.BlockSpec((None,) + stacked.shape[1:], lambda i, _nd=nd: (layer,) + (0,) * _nd,
                        pipeline_mode=pl.Buffered(1))


def _mixer_layer(layer, x, w_in_t, w_out, stacked_small, rw_split, rb_col):
    batch, seq, _ = x.shape
    tm = MIX_TM
    steps_per_seq = seq // tm
    n_steps = batch // STREAMS * steps_per_seq
    rows = STREAMS * tm
    kernel = functools.partial(_mixer_kernel, tm=tm, steps_per_seq=steps_per_seq)
    shared = (rw_split, rb_col)
    tile_spec = pl.BlockSpec((None, N_EXPERTS, LANES), lambda i: (i, 0, 0))
    step_spec = pl.BlockSpec((STREAMS, tm, D_MODEL), lambda i: (i // steps_per_seq, i % steps_per_seq, 0))
    return pl.pallas_call(
        kernel,
        out_shape=(
            jax.ShapeDtypeStruct(x.shape, F32),
            jax.ShapeDtypeStruct((n_steps, SUBLANES, rows), F32),
            jax.ShapeDtypeStruct((n_steps, N_EXPERTS, LANES), jnp.int32),
            jax.ShapeDtypeStruct((n_steps, N_EXPERTS, LANES), jnp.int32),
        ),
        grid=(n_steps,),
        in_specs=[step_spec] + [_layer_of(w, layer) for w in (w_in_t, w_out) + tuple(stacked_small)]
        + [_full(w.shape) for w in shared],
        out_specs=(step_spec, pl.BlockSpec((None, SUBLANES, rows), lambda i: (i, 0, 0)), tile_spec, tile_spec),
        scratch_shapes=[
            pltpu.VMEM((D_MODEL, P_ROW), BF16),
            pltpu.VMEM((P_COL_PAD, D_MODEL), BF16),
            pltpu.VMEM((D_MODEL, D_MODEL), BF16),
            pltpu.VMEM((rows, P_ROW), F32),
            pltpu.VMEM((P_COL_PAD, rows), F32),
            pltpu.VMEM((rows, D_MODEL), BF16),
            pltpu.VMEM((STREAMS * ML_HEADS, ML_HEAD_DIM, 2 * ML_HEAD_DIM), F32),
            pltpu.VMEM((STREAMS * SUBLANES, LANES), F32),
            pltpu.VMEM((STREAMS, tm + CONV_HALO, C_WIDTH), F32),
            pltpu.VMEM((STREAMS, SUBLANES - 1, tm + CONV_SPAN, C_WIDTH), F32),
            pltpu.VMEM((N_EXPERTS, LANES), F32),
        ],
        compiler_params=pltpu.CompilerParams(
            dimension_semantics=("arbitrary",), vmem_limit_bytes=VMEM_LIMIT),
        name="mixer",
    )(x, w_in_t, w_out, *stacked_small, *shared)


def _pow2_pieces(n, largest, act):
    piece = largest
    while piece >= ROW_ALIGN:
        start = jnp.bitwise_and(n, -2 * piece)

        @pl.when(jnp.bitwise_and(n, piece) != 0)
        def _(piece=piece, start=start):
            act(start, piece)
        piece //= 2


def _for_each_run(n_tab, cnt_tab, base_tab, tile, make_copy, act):
    def body(e, off):
        n = n_tab[tile * N_EXPERTS + e]
        base = base_tab[e] + cnt_tab[tile * N_EXPERTS + e]
        _pow2_pieces(n, MOE_TM, lambda start, size: act(make_copy(
            pl.multiple_of(off + start, ROW_ALIGN), pl.multiple_of(base + start, ROW_ALIGN), size)))
        return off + n

    lax.fori_loop(0, N_EXPERTS, body, jnp.int32(0))


def _dispatch_kernel(n_tab, cnt_tab, base_tab, fill_tab, x_ref, metat_ref, xs_hbm, comp_ref, zero_ref, sem,
                     *, n_rows):
    i = pl.program_id(0)
    last = pl.num_programs(0) - 1
    slot = lax.rem(i, 2)

    def copies(tile, slot_, act):
        def make_copy(src_row, dst_row, rows):
            return pltpu.make_async_copy(comp_ref.at[slot_, pl.ds(src_row, rows)],
                                         xs_hbm.at[pl.ds(dst_row, rows)], sem.at[slot_])
        _for_each_run(n_tab, cnt_tab, base_tab, tile, make_copy, act)

    def zero_fill(act):
        def zero_copy(dst_row, rows):
            return pltpu.make_async_copy(zero_ref.at[pl.ds(0, rows)],
                                         xs_hbm.at[pl.ds(pl.multiple_of(dst_row, ROW_ALIGN), rows)], sem.at[2])

        def per_expert(e, carry):
            first = fill_tab[2 * e]
            _pow2_pieces(fill_tab[2 * e + 1], ROW_TILE // 2, lambda start, size: act(zero_copy(first + start, size)))
            return carry

        lax.fori_loop(0, N_EXPERTS, per_expert, jnp.int32(0))

        def per_half_tile(k, carry):
            act(zero_copy(fill_tab[2 * N_EXPERTS] + k * (ROW_TILE // 2), ROW_TILE // 2))
            return carry

        lax.fori_loop(0, (n_rows - fill_tab[2 * N_EXPERTS]) // (ROW_TILE // 2), per_half_tile, jnp.int32(0))

    @pl.when(i == 0)
    def _():
        zero_ref[...] = jnp.zeros_like(zero_ref)
        zero_fill(lambda cp: cp.start())

    @pl.when(i >= 2)
    def _():
        copies(i - 2, slot, lambda cp: cp.wait())

    mt = metat_ref[...]
    row_f = lax.broadcasted_iota(jnp.int32, (COMPACT_ROWS, MOE_TM), 0).astype(F32)
    hit1 = row_f == mt[0:1, :]
    hit2 = row_f == mt[1:2, :]
    onehot = jnp.where(hit1, 1.0, jnp.where(hit2, 1.0, 0.0)).astype(BF16)
    x_tile = x_ref[...].reshape(MOE_TM, D_MODEL).astype(BF16)
    comp_ref[slot, :, 0:D_MODEL] = _dot(onehot, x_tile).astype(BF16)
    gate = jnp.sum(jnp.where(hit1, mt[2:3, :], jnp.where(hit2, mt[3:4, :], 0.0)), axis=-1, keepdims=True)
    lane = lax.broadcasted_iota(jnp.int32, (COMPACT_ROWS, LANES), 1)
    packed, rest = jnp.zeros((COMPACT_ROWS, LANES), F32), gate
    for piece in range(GATE_PIECES):
        head = rest.astype(BF16).astype(F32)
        packed = jnp.where(lane == piece, head, packed)
        rest = rest - head
    comp_ref[slot, :, D_MODEL:PAYLOAD_W] = packed.astype(BF16)
    copies(i, slot, lambda cp: cp.start())

    @pl.when(i == last)
    def _():
        @pl.when(i >= 1)
        def _():
            copies(i - 1, 1 - slot, lambda cp: cp.wait())
        copies(i, slot, lambda cp: cp.wait())
        zero_fill(lambda cp: cp.wait())


def _token_step_spec(seq):
    tiles_per_seq = seq // MIX_TM
    return pl.BlockSpec((STREAMS, MIX_TM, D_MODEL), lambda i, *_: (i // tiles_per_seq, i % tiles_per_seq, 0))


_ROUTE_STEP_SPEC = pl.BlockSpec((None, SUBLANES, MOE_TM), lambda i, *_: (i, 0, 0))


def _dispatch_layer(n_tab, cnt_tab, base_tab, fill_tab, x, metat, n_rows):
    batch, seq, _ = x.shape
    return pl.pallas_call(
        functools.partial(_dispatch_kernel, n_rows=n_rows),
        out_shape=jax.ShapeDtypeStruct((n_rows, PAYLOAD_W), BF16),
        grid_spec=pltpu.PrefetchScalarGridSpec(
            num_scalar_prefetch=4,
            grid=(batch * seq // MOE_TM,),
            in_specs=[_token_step_spec(seq), _ROUTE_STEP_SPEC],
            out_specs=pl.BlockSpec(memory_space=pl.ANY),
            scratch_shapes=[
                pltpu.VMEM((2, COMPACT_ROWS, PAYLOAD_W), BF16),
                pltpu.VMEM((ROW_TILE // 2, PAYLOAD_W), BF16),
                pltpu.SemaphoreType.DMA((3,)),
            ],
        ),
        compiler_params=pltpu.CompilerParams(dimension_semantics=("arbitrary",), vmem_limit_bytes=VMEM_LIMIT),
        name="dispatch",
    )(n_tab, cnt_tab, base_tab, fill_tab, x, metat)


def _expert_kernel(tile_e, tile_valid, xs_ref, wg_ref, wu_ref, wd_ref, ys_ref, wgu_ref, wdb_ref):
    i = pl.program_id(0)
    valid = tile_valid[i]
    new_expert = jnp.logical_or(i == 0, tile_e[i] != tile_e[jnp.maximum(i - 1, 0)])

    @pl.when(jnp.logical_and(valid > 0, new_expert))
    def _():
        wgu_ref[:, 0:D_EXPERT] = wg_ref[...].astype(BF16)
        wgu_ref[:, D_EXPERT:2 * D_EXPERT] = wu_ref[...].astype(BF16)
        wdb_ref[...] = wd_ref[...].astype(BF16)

    @pl.when(valid > 0)
    def _():
        blk = ROW_TILE // EXPERT_ROW_BLOCKS
        rows = [slice(b * blk, (b + 1) * blk) for b in range(EXPERT_ROW_BLOCKS)]
        gu = [_dot(xs_ref[r, 0:D_MODEL], wgu_ref[...]) for r in rows]
        for r, gu_b in zip(rows, gu):
            g, u = gu_b[:, 0:D_EXPERT], gu_b[:, D_EXPERT:2 * D_EXPERT]
            hid = (g * jax.nn.sigmoid(g) * u).astype(BF16)
            gate = jnp.sum(xs_ref[r, D_MODEL:PAYLOAD_W].astype(F32), axis=-1, keepdims=True)
            ys_ref[r, :] = (gate * _dot(hid, wdb_ref[...])).astype(BF16)

    @pl.when(valid == 0)
    def _():
        ys_ref[...] = jnp.zeros_like(ys_ref)


def _expert_layer(layer, tile_e, tile_valid, xs, wg, wu, wd):
    n_steps = tile_e.shape[0]
    return pl.pallas_call(
        _expert_kernel,
        out_shape=jax.ShapeDtypeStruct((xs.shape[0], D_MODEL), BF16),
        grid_spec=pltpu.PrefetchScalarGridSpec(
            num_scalar_prefetch=2,
            grid=(n_steps,),
            in_specs=[
                pl.BlockSpec((ROW_TILE, PAYLOAD_W), lambda i, e, v: (i, 0)),
                pl.BlockSpec((None, None, D_MODEL, D_EXPERT), lambda i, e, v: (layer, e[i], 0, 0)),
                pl.BlockSpec((None, None, D_MODEL, D_EXPERT), lambda i, e, v: (layer, e[i], 0, 0)),
                pl.BlockSpec((None, None, D_EXPERT, D_MODEL), lambda i, e, v: (layer, e[i], 0, 0)),
            ],
            out_specs=pl.BlockSpec((ROW_TILE, D_MODEL), lambda i, e, v: (i, 0)),
            scratch_shapes=[
                pltpu.VMEM((D_MODEL, 2 * D_EXPERT), BF16),
                pltpu.VMEM((D_EXPERT, D_MODEL), BF16),
            ],
        ),
        compiler_params=pltpu.CompilerParams(dimension_semantics=("arbitrary",), vmem_limit_bytes=VMEM_LIMIT),
        name="experts",
    )(tile_e, tile_valid, xs, wg, wu, wd)


def _combine_kernel(n_tab, cnt_tab, base_tab, x_ref, metat_ref, ys_hbm, l2w_ref, l2b_ref, o_ref, yc_ref, sem):
    i = pl.program_id(0)
    n_steps = pl.num_programs(0)
    slot = lax.rem(i, 2)

    def copies(tile, slot_, act):
        def make_copy(buf_row, ys_row, rows):
            return pltpu.make_async_copy(ys_hbm.at[pl.ds(ys_row, rows)],
                                         yc_ref.at[slot_, pl.ds(buf_row, rows)], sem.at[slot_])
        _for_each_run(n_tab, cnt_tab, base_tab, tile, make_copy, act)

    @pl.when(i == 0)
    def _():
        yc_ref[...] = jnp.zeros_like(yc_ref)
        copies(0, 0, lambda cp: cp.start())

    @pl.when(i + 1 < n_steps)
    def _():
        copies(i + 1, 1 - slot, lambda cp: cp.start())

    copies(i, slot, lambda cp: cp.wait())

    t_r = lax.broadcasted_iota(jnp.int32, (MOE_TM, MOE_TM), 0)
    t_c = lax.broadcasted_iota(jnp.int32, (MOE_TM, MOE_TM), 1)
    as_col = lambda row: jnp.sum(jnp.where(t_r == t_c, row, 0.0), axis=-1, keepdims=True)
    col_f = lax.broadcasted_iota(jnp.int32, (MOE_TM, COMPACT_ROWS), 1).astype(F32)
    mt = metat_ref[...]
    onehot = jnp.where(col_f == as_col(mt[0:1, :]), 1.0,
                       jnp.where(col_f == as_col(mt[1:2, :]), 1.0, 0.0)).astype(BF16)
    y = _dot(onehot, yc_ref[slot])
    x = x_ref[...].reshape(MOE_TM, D_MODEL)
    o_ref[...] = _layer_norm(DEEPNORM_ALPHA * x + y, l2w_ref[...], l2b_ref[...], LN_EPS).reshape(o_ref.shape)


def _combine_layer(layer, n_tab, cnt_tab, base_tab, x, metat, ys, l2w, l2b):
    batch, seq, _ = x.shape
    ln_spec = pl.BlockSpec((None, 1, D_MODEL), lambda i, *_: (layer, 0, 0))
    return pl.pallas_call(
        _combine_kernel,
        out_shape=jax.ShapeDtypeStruct(x.shape, F32),
        grid_spec=pltpu.PrefetchScalarGridSpec(
            num_scalar_prefetch=3,
            grid=(batch * seq // MOE_TM,),
            in_specs=[
                _token_step_spec(seq),
                _ROUTE_STEP_SPEC,
                pl.BlockSpec(memory_space=pl.ANY),
                ln_spec,
                ln_spec,
            ],
            out_specs=_token_step_spec(seq),
            scratch_shapes=[pltpu.VMEM((2, COMPACT_ROWS, D_MODEL), BF16), pltpu.SemaphoreType.DMA((2,))],
        ),
        compiler_params=pltpu.CompilerParams(dimension_semantics=("arbitrary",), vmem_limit_bytes=VMEM_LIMIT),
        name="combine",
    )(n_tab, cnt_tab, base_tab, x, metat, ys, l2w, l2b)


def _expert_plan(counts, n_steps):
    tiles_e = (counts + (ROW_TILE - 1)) // ROW_TILE
    cum = jnp.cumsum(tiles_e)
    first_tile = cum - tiles_e
    total = cum[-1]
    step = jnp.arange(n_steps, dtype=jnp.int32)
    owner = (step[:, None] >= cum[None, :]).sum(axis=1).astype(jnp.int32)
    is_owner = owner[:, None] == jnp.arange(N_EXPERTS, dtype=jnp.int32)[None, :]
    local = step - jnp.where(is_owner, first_tile[None, :], 0).sum(axis=1)
    rows_left = jnp.where(is_owner, counts[None, :], 0).sum(axis=1) - local * ROW_TILE
    valid = jnp.where(step < total, jnp.clip(rows_left, 0, ROW_TILE), 0).astype(jnp.int32)
    tile_e = jnp.minimum(owner, N_EXPERTS - 1)
    base = (first_tile * ROW_TILE).astype(jnp.int32)
    fill = jnp.stack([base + counts, tiles_e * ROW_TILE - counts], axis=1).reshape(-1)
    fill_tab = jnp.concatenate([fill, (total * ROW_TILE)[None]]).astype(jnp.int32)
    return base, tile_e, valid, fill_tab


def _moe_layer(layer, x, metat, tile_n, tile_cnt, wg, wu, wd, l2w, l2b):
    n_tok = x.shape[0] * x.shape[1]
    n_tiles = n_tok // MOE_TM
    n_steps = (2 * n_tok + N_EXPERTS * (ROW_ALIGN - 1) * n_tiles) // ROW_TILE + N_EXPERTS
    n_tab = tile_n[:, :, 0].reshape(-1)
    cnt_tab = tile_cnt[:, :, 0].reshape(-1)
    counts = tile_cnt[-1, :, 0] + tile_n[-1, :, 0]
    base_tab, tile_e, tile_valid, fill_tab = _expert_plan(counts, n_steps)
    xs = _dispatch_layer(n_tab, cnt_tab, base_tab, fill_tab, x, metat, n_steps * ROW_TILE)
    ys = _expert_layer(layer, tile_e, tile_valid, xs, wg, wu, wd)
    return _combine_layer(layer, n_tab, cnt_tab, base_tab, x, metat, ys, l2w, l2b)


def kernel(x, w_in, b_in, mlstm_norm_w, gmlp_norm_w, gmlp_norm_b, gmlp_ws, gmlp_bs, conv_w, conv_b,
           conv_norm_w, conv_norm_b, w_out, ln1_w, ln1_b, router_w, router_b, w_gate, w_up, w_down,
           ln2_w, ln2_b):
    batch, seq, d = x.shape

    q_lo, k_lo, v_lo = 0, ML_WIDTH, 2 * ML_WIDTH
    gate_lo = 4 * ML_WIDTH
    rest_lo = gate_lo + 2 * ML_HEADS

    def row_part(t):
        return jnp.concatenate([t[..., q_lo:k_lo], t[..., v_lo:gate_lo], t[..., rest_lo:]], axis=-1)

    def col_part(t):
        return jnp.concatenate([t[..., k_lo:v_lo], t[..., gate_lo:rest_lo]], axis=-1)

    tril = jnp.tril(jnp.ones((CHUNK, CHUNK), gmlp_ws.dtype))
    rw_pad = jnp.pad(router_w.astype(F32), ((0, 0), (0, LANES - N_EXPERTS)))
    rw_head = rw_pad.astype(BF16)
    rw_split = jnp.concatenate([rw_head, (rw_pad - rw_head.astype(F32)).astype(BF16)], axis=1)
    rb_col = router_b.astype(F32).reshape(N_EXPERTS, 1)

    rows_of = lambda t: t.reshape(DEPTH, 1, -1)
    b_row = rows_of(row_part(b_in))
    b_col = jnp.pad(col_part(b_in), ((0, 0), (0, P_COL_PAD - P_COL))).reshape(DEPTH, P_COL_PAD, 1)
    ws_cat = jnp.transpose(gmlp_ws * tril, (0, 2, 1, 3)).reshape(DEPTH, CHUNK, G_HEADS * CHUNK).astype(BF16)
    bs_full = jnp.repeat(jnp.transpose(gmlp_bs, (0, 2, 1)), G_HEAD_DIM, axis=2)
    cw = jnp.pad(conv_w, ((0, 0), (0, 1), (0, 0)))
    mixer_small = (b_row, b_col, rows_of(mlstm_norm_w), rows_of(gmlp_norm_w), rows_of(gmlp_norm_b), ws_cat, bs_full,
                   cw, rows_of(conv_b), rows_of(conv_norm_w), rows_of(conv_norm_b), rows_of(ln1_w), rows_of(ln1_b))

    w_in_t = jnp.swapaxes(w_in, 1, 2)

    for l in range(DEPTH):
        x, metat, tile_n, tile_cnt = _mixer_layer(l, x, w_in_t, w_out, mixer_small, rw_split, rb_col)
        x = _moe_layer(l, x, metat, tile_n, tile_cnt, w_gate, w_up, w_down, rows_of(ln2_w), rows_of(ln2_b))
    return x
```

```python
import functools

import jax
import jax.numpy as jnp
from jax import lax
from jax.experimental import pallas as pl
from jax.experimental.pallas import tpu as pltpu

D_MODEL = 1024
DEPTH = 4
ML_WIDTH = 512
ML_HEADS = 4
ML_HEAD_DIM = 128
CHUNK = 128
G_WIDTH = 256
G_HEADS = 4
G_HEAD_DIM = 64
C_WIDTH = 256
CONV_WIDTH = 31
N_EXPERTS = 16
N_GROUPS = 4
EXPERTS_PER_GROUP = 4
D_EXPERT = 512
DEEPNORM_ALPHA = (2.0 * DEPTH) ** 0.25
LN_EPS = 1e-5
K_SCALE = ML_HEAD_DIM ** -0.5

Q_LO, V_LO, O_LO, GU_LO, GV_LO, CA_LO, CB_LO = 0, 512, 1024, 1536, 1792, 2048, 2304
P_ROW = 2560
P_COL = ML_WIDTH + 2 * ML_HEADS
P_COL_PAD = ML_WIDTH + 16
IN_Q, IN_K, IN_VO, IN_GATES, IN_REST = (0, 512), (512, 1024), (1024, 2048), (2048, 2056), (2056, 3080)
P_IN = 3080
W_PREP_ROWS = 256

LANES = 128
SUBLANES = 8
MIX_TM = 256
ROW_ALIGN = 16
ROW_TILE = 512
EXPERT_ROW_BLOCKS = 2
PAYLOAD_W = D_MODEL + LANES
CONV_HALO = 32
CONV_SPAN = CONV_HALO - SUBLANES
STREAMS = 2
MOE_TM = STREAMS * MIX_TM
COMPACT_ROWS = 2 * MOE_TM + 2 * LANES
assert COMPACT_ROWS >= 2 * MOE_TM + N_EXPERTS * (ROW_ALIGN - 1)
GATE_PIECES = 3
COMMON_RUN = 2 * (2 * MOE_TM // N_EXPERTS)
VMEM_LIMIT = 58 * 1024 * 1024

F32 = jnp.float32
BF16 = jnp.bfloat16
NEG_INF = float("-inf")


def _layer_norm(x, w, b, eps):
    mu = jnp.mean(x, axis=-1, keepdims=True)
    xc = x - mu
    var = jnp.mean(xc * xc, axis=-1, keepdims=True)
    return xc * lax.rsqrt(var + eps) * w + b


def _gelu_tanh(x):
    return 0.5 * x * (1.0 + jnp.tanh(0.7978845608028654 * (x + 0.044715 * (x * x * x))))


def _log_sigmoid(x):
    return jnp.minimum(x, 0.0) - jnp.log1p(jnp.exp(-jnp.abs(x)))


def _dot(a, b):
    return jnp.dot(a, b, preferred_element_type=F32)


def _router_logits(x_new, rw_ref):
    xh = x_new.astype(BF16)
    xl = (x_new - xh.astype(F32)).astype(BF16)
    head = _dot(xh, rw_ref[...])
    return head[:, 0:LANES] + head[:, LANES:2 * LANES] + _dot(xl, rw_ref[:, 0:LANES])


def _route_tile(logits, rb_ref, metat_ref, tile_n_ref, tile_cnt_ref, cnt_ref):
    tm = logits.shape[1]
    s_all = jax.nn.sigmoid(logits)
    sel_all = s_all + rb_ref[...]
    rows_of = lambda v: [v[k:k + 1, :] for k in range(N_EXPERTS)]
    s, sel = rows_of(s_all), rows_of(sel_all)

    best = None
    for g in range(N_GROUPS):
        r = sel[EXPERTS_PER_GROUP * g:EXPERTS_PER_GROUP * (g + 1)]
        hi01, lo01 = jnp.maximum(r[0], r[1]), jnp.minimum(r[0], r[1])
        hi23, lo23 = jnp.maximum(r[2], r[3]), jnp.minimum(r[2], r[3])
        score = jnp.maximum(hi01, hi23) + jnp.maximum(jnp.minimum(hi01, hi23), jnp.maximum(lo01, lo23))
        if best is None:
            best, gidx = score, jnp.zeros(score.shape, jnp.int32)
        else:
            better = score > best
            gidx = jnp.where(better, g, gidx)
            best = jnp.where(better, score, best)

    def of_group(rows):
        out = []
        for j in range(EXPERTS_PER_GROUP):
            v = rows[j]
            for g in range(1, N_GROUPS):
                v = jnp.where(gidx == g, rows[EXPERTS_PER_GROUP * g + j], v)
            out.append(v)
        return out

    def first_max(vals):
        best_v, best_j = vals[0], jnp.zeros(vals[0].shape, F32)
        for j in range(1, len(vals)):
            better = vals[j] > best_v
            best_j = jnp.where(better, float(j), best_j)
            best_v = jnp.where(better, vals[j], best_v)
        return best_j

    def take(vals, idx):
        v = vals[0]
        for j in range(1, len(vals)):
            v = jnp.where(idx == float(j), vals[j], v)
        return v

    cand, cand_s = of_group(sel), of_group(s)
    j1 = first_max(cand)
    j2 = first_max([jnp.where(j1 == float(j), NEG_INF, cand[j]) for j in range(EXPERTS_PER_GROUP)])
    g1, g2 = take(cand_s, j1), take(cand_s, j2)
    tot = g1 + g2
    first_of_group = gidx.astype(F32) * float(EXPERTS_PER_GROUP)
    e1, e2 = first_of_group + j1, first_of_group + j2

    e_f = lax.broadcasted_iota(jnp.int32, (N_EXPERTS, tm), 0).astype(F32)
    assign = jnp.where(e_f == e1, 1.0, jnp.where(e_f == e2, 1.0, 0.0)).astype(BF16)
    t_r = lax.broadcasted_iota(jnp.int32, (tm, tm), 0)
    t_c = lax.broadcasted_iota(jnp.int32, (tm, tm), 1)
    earlier = jnp.where(t_r < t_c, 1.0, 0.0).astype(BF16)
    rank = _dot(assign, earlier)
    n_b = _dot(assign, jnp.ones((tm, LANES), BF16))
    n_up_b = jnp.floor((n_b + (ROW_ALIGN - 1.0)) * (1.0 / ROW_ALIGN)) * ROW_ALIGN
    x_r = lax.broadcasted_iota(jnp.int32, (N_EXPERTS, N_EXPERTS), 0)
    x_c = lax.broadcasted_iota(jnp.int32, (N_EXPERTS, N_EXPERTS), 1)
    lower = jnp.where(x_c < x_r, 1.0, 0.0).astype(BF16)
    off_b = _dot(lower, n_up_b.astype(BF16))
    pos = rows_of(jnp.concatenate([off_b] * (tm // LANES), axis=1) + rank)
    metat_ref[...] = jnp.concatenate(
        [take(pos, e1), take(pos, e2), g1 / tot, g2 / tot, jnp.zeros((SUBLANES - 4, tm), F32)], axis=0)

    cnt = cnt_ref[...]
    tile_n_ref[...] = n_up_b.astype(jnp.int32)
    tile_cnt_ref[...] = cnt.astype(jnp.int32)
    cnt_ref[...] = cnt + n_up_b


def _project(x_tile, w_row_ref, b_row_ref, w_col_ref, b_col_ref, proj_ref, colp_ref):
    xb = x_tile.astype(BF16)
    proj_ref[...] = _dot(xb, w_row_ref[...]) + b_row_ref[...]
    colp_ref[...] = _dot(w_col_ref[...], x_tile.T.astype(BF16)) + b_col_ref[...]


def _mix_tile(x, proj_ref, colp_ref, mnw_ref, gnw_ref, gnb_ref, ws_ref, bs_ref, cw_ref, cb_ref, cnw_ref,
              cnb_ref, w_out_ref, l1w_ref, l1b_ref, hcat_ref, cstate_ref, mstate_ref, cbuf_ref, shift_ref, tm):
    n_chunks = tm // CHUNK
    gates = colp_ref[ML_WIDTH:P_COL, :]
    lane_in_chunk = lax.broadcasted_iota(jnp.int32, gates.shape, 1) % CHUNK
    logf_all = _log_sigmoid(gates)
    bcum = logf_all
    d = 1
    while d < CHUNK:
        bcum = bcum + jnp.where(lane_in_chunk >= d, pltpu.roll(bcum, d, 1), 0.0)
        d *= 2

    row_i = lax.broadcasted_iota(jnp.int32, (CHUNK, CHUNK), 0)
    col_i = lax.broadcasted_iota(jnp.int32, (CHUNK, CHUNK), 1)
    causal = col_i <= row_i
    diag = col_i == row_i
    ones_col = jnp.where(col_i == 0, 1.0, 0.0).astype(BF16)

    for sc in range(STREAMS * n_chunks):
        st, c = divmod(sc, n_chunks)
        r0 = st * tm + c * CHUNK
        rows = slice(r0, r0 + CHUNK)
        m_rows = slice(st * SUBLANES, st * SUBLANES + ML_HEADS)
        logi = gates[0:ML_HEADS, r0:r0 + CHUNK]
        logf = logf_all[ML_HEADS:2 * ML_HEADS, r0:r0 + CHUNK]
        b_row = bcum[ML_HEADS:2 * ML_HEADS, r0:r0 + CHUNK]
        g = jnp.sum(logf, axis=-1, keepdims=True)
        m_prev = mstate_ref[m_rows, 0:1]
        a_row = g - b_row + logi
        m_new = jnp.maximum(g + m_prev, jnp.max(a_row, axis=-1, keepdims=True))
        w_row = jnp.exp(a_row - m_new) * K_SCALE
        decay = jnp.exp(g + m_prev - m_new)
        mstate_ref[m_rows, :] = jnp.broadcast_to(m_new, (ML_HEADS, LANES))

        for h in range(ML_HEADS):
            hs = slice(h * ML_HEAD_DIM, (h + 1) * ML_HEAD_DIM)
            q = proj_ref[rows, Q_LO + h * ML_HEAD_DIM:Q_LO + (h + 1) * ML_HEAD_DIM].astype(BF16)
            v = proj_ref[rows, V_LO + h * ML_HEAD_DIM:V_LO + (h + 1) * ML_HEAD_DIM].astype(BF16)
            o_gate = proj_ref[rows, O_LO + h * ML_HEAD_DIM:O_LO + (h + 1) * ML_HEAD_DIM]
            kt = colp_ref[hs, r0:r0 + CHUNK]
            v_ext = jnp.concatenate([v, ones_col], axis=1)

            b_r = b_row[h:h + 1, :]
            b_c = jnp.sum(jnp.where(diag, b_r, 0.0), axis=-1, keepdims=True)
            log_d = jnp.where(causal, b_c - b_r + logi[h:h + 1, :], NEG_INF)
            mp = m_prev[h:h + 1, :]
            log_inter = b_c + mp
            m_row = jnp.maximum(log_inter, jnp.max(log_d, axis=-1, keepdims=True))
            c_ext = cstate_ref[st * ML_HEADS + h]
            from_q = _dot(q, jnp.concatenate([kt.astype(BF16), c_ext.astype(BF16)], axis=1))
            p = from_q[:, 0:CHUNK] * (jnp.exp(log_d - m_row) * K_SCALE)
            w_inter = jnp.exp(log_inter - m_row)
            ktw = (kt * w_row[h:h + 1, :]).astype(BF16)
            onto_v = _dot(jnp.concatenate([p.astype(BF16), ktw], axis=0), v_ext)
            tot = onto_v[0:CHUNK, :] + w_inter * from_q[:, CHUNK:]
            den = jnp.maximum(jnp.abs(tot[:, ML_HEAD_DIM:ML_HEAD_DIM + 1]), jnp.exp(-m_row))
            hh = tot[:, 0:ML_HEAD_DIM] / den
            cstate_ref[st * ML_HEADS + h] = decay[h:h + 1, :] * c_ext + onto_v[CHUNK:, :]

            mu = jnp.mean(hh, axis=-1, keepdims=True)
            hc = hh - mu
            var = jnp.mean(hc * hc, axis=-1, keepdims=True)
            hn = hc * lax.rsqrt(var + 1e-6) * mnw_ref[:, hs]
            hcat_ref[rows, hs] = (jax.nn.sigmoid(o_gate) * hn).astype(BF16)

        u = _gelu_tanh(proj_ref[rows, GU_LO:GU_LO + G_WIDTH])
        z = _layer_norm(_gelu_tanh(proj_ref[rows, GV_LO:GV_LO + G_WIDTH]), gnw_ref[...], gnb_ref[...], LN_EPS)
        lane_head = lax.broadcasted_iota(jnp.int32, (CHUNK, G_WIDTH), 1) // G_HEAD_DIM
        z_bd = jnp.concatenate(
            [jnp.where(lane_head == h, z, 0.0).astype(BF16) for h in range(G_HEADS)], axis=0)
        zs = _dot(ws_ref[...], z_bd) + bs_ref[...]
        hcat_ref[rows, ML_WIDTH:ML_WIDTH + G_WIDTH] = (u * zs).astype(BF16)

        ca = proj_ref[rows, CA_LO:CA_LO + C_WIDTH]
        cb = proj_ref[rows, CB_LO:CB_LO + C_WIDTH]
        cbuf_ref[st, CONV_HALO + c * CHUNK:CONV_HALO + (c + 1) * CHUNK, :] = ca * jax.nn.sigmoid(cb)

    first_tap = CONV_HALO - (CONV_WIDTH - 1)
    for st in range(STREAMS):
        for sh in range(1, SUBLANES):
            shift_ref[st, sh - 1] = cbuf_ref[st, sh:sh + tm + CONV_SPAN, :]
        for c in range(n_chunks):
            acc = jnp.zeros((CHUNK, C_WIDTH), F32) + cb_ref[...]
            for k in range(CONV_WIDTH):
                whole, sh = divmod(first_tap + k, SUBLANES)
                lo = c * CHUNK + whole * SUBLANES
                tap = cbuf_ref[st, lo:lo + CHUNK, :] if sh == 0 else shift_ref[st, sh - 1, lo:lo + CHUNK, :]
                acc = acc + cw_ref[k:k + 1, :] * tap
            cn = _layer_norm(acc, cnw_ref[...], cnb_ref[...], LN_EPS)
            r0 = st * tm + c * CHUNK
            hcat_ref[r0:r0 + CHUNK, ML_WIDTH + G_WIDTH:D_MODEL] = (cn * jax.nn.sigmoid(cn)).astype(BF16)
        cbuf_ref[st, 0:CONV_HALO, :] = cbuf_ref[st, tm:tm + CONV_HALO, :]

    y = _dot(hcat_ref[...], w_out_ref[...])
    x_new = _layer_norm(DEEPNORM_ALPHA * x + y, l1w_ref[...], l1b_ref[...], LN_EPS)
    return x_new


def _prepare_in_proj(w_t_ref, w_row_ref, w_col_ref):
    for (src_lo, src_hi), dst_lo in ((IN_Q, Q_LO), (IN_VO, V_LO), (IN_REST, GU_LO)):
        for off in range(0, src_hi - src_lo, W_PREP_ROWS):
            block = w_t_ref[src_lo + off:src_lo + off + W_PREP_ROWS, :]
            w_row_ref[:, dst_lo + off:dst_lo + off + W_PREP_ROWS] = block.T.astype(BF16)
    w_col_ref[0:ML_WIDTH, :] = w_t_ref[IN_K[0]:IN_K[1], :].astype(BF16)
    gates = w_t_ref[IN_GATES[0]:IN_GATES[1], :]
    w_col_ref[ML_WIDTH:P_COL_PAD, :] = jnp.concatenate(
        [gates, jnp.zeros((P_COL_PAD - P_COL, D_MODEL), F32)], axis=0).astype(BF16)


def _mixer_kernel(x_ref, w_in_t_ref, w_out_f32_ref, b_row_ref, b_col_ref, mnw_ref, gnw_ref, gnb_ref,
                  ws_ref, bs_ref, cw_ref, cb_ref, cnw_ref, cnb_ref, l1w_ref, l1b_ref, rw_ref, rb_ref,
                  o_ref, metat_ref, tile_n_ref, tile_cnt_ref,
                  w_row_ref, w_col_ref, w_out_ref, proj_ref, colp_ref, hcat_ref, cstate_ref, mstate_ref, cbuf_ref, shift_ref,
                  cnt_ref, *, tm, steps_per_seq):
    i = pl.program_id(0)

    @pl.when(i == 0)
    def _():
        cnt_ref[...] = jnp.zeros_like(cnt_ref)
        _prepare_in_proj(w_in_t_ref, w_row_ref, w_col_ref)
        for r0 in range(0, D_MODEL, W_PREP_ROWS):
            w_out_ref[r0:r0 + W_PREP_ROWS, :] = w_out_f32_ref[r0:r0 + W_PREP_ROWS, :].astype(BF16)

    @pl.when(i % steps_per_seq == 0)
    def _():
        cstate_ref[...] = jnp.zeros_like(cstate_ref)
        mstate_ref[...] = jnp.zeros_like(mstate_ref)
        cbuf_ref[:, 0:CONV_HALO, :] = jnp.zeros((STREAMS, CONV_HALO, C_WIDTH), F32)

    x = x_ref[...].reshape(STREAMS * tm, D_MODEL)
    _project(x, w_row_ref, b_row_ref, w_col_ref, b_col_ref, proj_ref, colp_ref)
    x_new = _mix_tile(x, proj_ref, colp_ref, mnw_ref, gnw_ref, gnb_ref, ws_ref, bs_ref, cw_ref, cb_ref, cnw_ref,
                      cnb_ref, w_out_ref, l1w_ref, l1b_ref, hcat_ref, cstate_ref, mstate_ref, cbuf_ref, shift_ref,
                      tm)
    o_ref[...] = x_new.reshape(STREAMS, tm, D_MODEL)
    _route_tile(_router_logits(x_new, rw_ref).T[0:N_EXPERTS, :], rb_ref, metat_ref, tile_n_ref, tile_cnt_ref, cnt_ref)


def _full(shape):
    nd = len(shape)
    return pl.BlockSpec(shape, lambda i, _nd=nd: (0,) * _nd, pipeline_mode=pl.Buffered(1))


def _layer_of(stacked, layer):
    nd = stacked.ndim - 1
    return pl.BlockSpec((None,) + stacked.shape[1:], lambda i, _nd=nd: (layer,) + (0,) * _nd,
                        pipeline_mode=pl.Buffered(1))


def _mixer_layer(layer, x, w_in_t, w_out, stacked_small, rw_split, rb_col):
    batch, seq, _ = x.shape
    tm = MIX_TM
    steps_per_seq = seq // tm
    n_steps = batch // STREAMS * steps_per_seq
    rows = STREAMS * tm
    kernel = functools.partial(_mixer_kernel, tm=tm, steps_per_seq=steps_per_seq)
    shared = (rw_split, rb_col)
    tile_spec = pl.BlockSpec((None, N_EXPERTS, LANES), lambda i: (i, 0, 0))
    step_spec = pl.BlockSpec((STREAMS, tm, D_MODEL), lambda i: (i // steps_per_seq, i % steps_per_seq, 0))
    return pl.pallas_call(
        kernel,
        out_shape=(
            jax.ShapeDtypeStruct(x.shape, F32),
            jax.ShapeDtypeStruct((n_steps, SUBLANES, rows), F32),
            jax.ShapeDtypeStruct((n_steps, N_EXPERTS, LANES), jnp.int32),
            jax.ShapeDtypeStruct((n_steps, N_EXPERTS, LANES), jnp.int32),
        ),
        grid=(n_steps,),
        in_specs=[step_spec] + [_layer_of(w, layer) for w in (w_in_t, w_out) + tuple(stacked_small)]
        + [_full(w.shape) for w in shared],
        out_specs=(step_spec, pl.BlockSpec((None, SUBLANES, rows), lambda i: (i, 0, 0)), tile_spec, tile_spec),
        scratch_shapes=[
            pltpu.VMEM((D_MODEL, P_ROW), BF16),
            pltpu.VMEM((P_COL_PAD, D_MODEL), BF16),
            pltpu.VMEM((D_MODEL, D_MODEL), BF16),
            pltpu.VMEM((rows, P_ROW), F32),
            pltpu.VMEM((P_COL_PAD, rows), F32),
            pltpu.VMEM((rows, D_MODEL), BF16),
            pltpu.VMEM((STREAMS * ML_HEADS, ML_HEAD_DIM, 2 * ML_HEAD_DIM), F32),
            pltpu.VMEM((STREAMS * SUBLANES, LANES), F32),
            pltpu.VMEM((STREAMS, tm + CONV_HALO, C_WIDTH), F32),
            pltpu.VMEM((STREAMS, SUBLANES - 1, tm + CONV_SPAN, C_WIDTH), F32),
            pltpu.VMEM((N_EXPERTS, LANES), F32),
        ],
        compiler_params=pltpu.CompilerParams(
            dimension_semantics=("arbitrary",), vmem_limit_bytes=VMEM_LIMIT),
        name="mixer",
    )(x, w_in_t, w_out, *stacked_small, *shared)


def _pow2_pieces(n, largest, act):
    def arms(smallest, biggest):
        piece = biggest
        while piece >= smallest:
            start = jnp.bitwise_and(n, -2 * piece)

            @pl.when(jnp.bitwise_and(n, piece) != 0)
            def _(piece=piece, start=start):
                act(start, piece)
            piece //= 2

    if largest >= COMMON_RUN:
        @pl.when(n >= COMMON_RUN)
        def _():
            arms(COMMON_RUN, largest)
    arms(ROW_ALIGN, min(largest, COMMON_RUN // 2))


def _for_each_run(n_tab, cnt_tab, base_tab, tile, make_copy, act):
    def body(e, off):
        n = n_tab[tile * N_EXPERTS + e]
        base = base_tab[e] + cnt_tab[tile * N_EXPERTS + e]
        _pow2_pieces(n, MOE_TM, lambda start, size: act(make_copy(
            pl.multiple_of(off + start, ROW_ALIGN), pl.multiple_of(base + start, ROW_ALIGN), size)))
        return off + n

    lax.fori_loop(0, N_EXPERTS, body, jnp.int32(0))


def _dispatch_kernel(n_tab, cnt_tab, base_tab, fill_tab, x_ref, metat_ref, xs_hbm, comp_ref, zero_ref, sem,
                     *, n_rows):
    i = pl.program_id(0)
    last = pl.num_programs(0) - 1
    slot = lax.rem(i, 2)

    def copies(tile, slot_, act):
        def make_copy(src_row, dst_row, rows):
            return pltpu.make_async_copy(comp_ref.at[slot_, pl.ds(src_row, rows)],
                                         xs_hbm.at[pl.ds(dst_row, rows)], sem.at[slot_])
        _for_each_run(n_tab, cnt_tab, base_tab, tile, make_copy, act)

    def zero_fill(act):
        def zero_copy(dst_row, rows):
            return pltpu.make_async_copy(zero_ref.at[pl.ds(0, rows)],
                                         xs_hbm.at[pl.ds(pl.multiple_of(dst_row, ROW_ALIGN), rows)], sem.at[2])

        def per_expert(e, carry):
            first = fill_tab[2 * e]
            _pow2_pieces(fill_tab[2 * e + 1], ROW_TILE // 2, lambda start, size: act(zero_copy(first + start, size)))
            return carry

        lax.fori_loop(0, N_EXPERTS, per_expert, jnp.int32(0))

        def per_half_tile(k, carry):
            act(zero_copy(fill_tab[2 * N_EXPERTS] + k * (ROW_TILE // 2), ROW_TILE // 2))
            return carry

        lax.fori_loop(0, (n_rows - fill_tab[2 * N_EXPERTS]) // (ROW_TILE // 2), per_half_tile, jnp.int32(0))

    @pl.when(i == 0)
    def _():
        zero_ref[...] = jnp.zeros_like(zero_ref)
        zero_fill(lambda cp: cp.start())

    @pl.when(i >= 2)
    def _():
        copies(i - 2, slot, lambda cp: cp.wait())

    mt = metat_ref[...]
    row_f = lax.broadcasted_iota(jnp.int32, (COMPACT_ROWS, MOE_TM), 0).astype(F32)
    hit1 = row_f == mt[0:1, :]
    hit2 = row_f == mt[1:2, :]
    onehot = jnp.where(hit1, 1.0, jnp.where(hit2, 1.0, 0.0)).astype(BF16)
    x_tile = x_ref[...].reshape(MOE_TM, D_MODEL).astype(BF16)
    comp_ref[slot, :, 0:D_MODEL] = _dot(onehot, x_tile).astype(BF16)
    gate = jnp.sum(jnp.where(hit1, mt[2:3, :], jnp.where(hit2, mt[3:4, :], 0.0)), axis=-1, keepdims=True)
    lane = lax.broadcasted_iota(jnp.int32, (COMPACT_ROWS, LANES), 1)
    packed, rest = jnp.zeros((COMPACT_ROWS, LANES), F32), gate
    for piece in range(GATE_PIECES):
        head = rest.astype(BF16).astype(F32)
        packed = jnp.where(lane == piece, head, packed)
        rest = rest - head
    comp_ref[slot, :, D_MODEL:PAYLOAD_W] = packed.astype(BF16)
    copies(i, slot, lambda cp: cp.start())

    @pl.when(i == last)
    def _():
        @pl.when(i >= 1)
        def _():
            copies(i - 1, 1 - slot, lambda cp: cp.wait())
        copies(i, slot, lambda cp: cp.wait())
        zero_fill(lambda cp: cp.wait())


def _token_step_spec(seq):
    tiles_per_seq = seq // MIX_TM
    return pl.BlockSpec((STREAMS, MIX_TM, D_MODEL), lambda i, *_: (i // tiles_per_seq, i % tiles_per_seq, 0))


_ROUTE_STEP_SPEC = pl.BlockSpec((None, SUBLANES, MOE_TM), lambda i, *_: (i, 0, 0))


def _dispatch_layer(n_tab, cnt_tab, base_tab, fill_tab, x, metat, n_rows):
    batch, seq, _ = x.shape
    return pl.pallas_call(
        functools.partial(_dispatch_kernel, n_rows=n_rows),
        out_shape=jax.ShapeDtypeStruct((n_rows, PAYLOAD_W), BF16),
        grid_spec=pltpu.PrefetchScalarGridSpec(
            num_scalar_prefetch=4,
            grid=(batch * seq // MOE_TM,),
            in_specs=[_token_step_spec(seq), _ROUTE_STEP_SPEC],
            out_specs=pl.BlockSpec(memory_space=pl.ANY),
            scratch_shapes=[
                pltpu.VMEM((2, COMPACT_ROWS, PAYLOAD_W), BF16),
                pltpu.VMEM((ROW_TILE // 2, PAYLOAD_W), BF16),
                pltpu.SemaphoreType.DMA((3,)),
            ],
        ),
        compiler_params=pltpu.CompilerParams(dimension_semantics=("arbitrary",), vmem_limit_bytes=VMEM_LIMIT),
        name="dispatch",
    )(n_tab, cnt_tab, base_tab, fill_tab, x, metat)


def _expert_kernel(tile_e, tile_valid, xs_ref, wg_ref, wu_ref, wd_ref, ys_ref, wgu_ref, wdb_ref):
    i = pl.program_id(0)
    valid = tile_valid[i]
    new_expert = jnp.logical_or(i == 0, tile_e[i] != tile_e[jnp.maximum(i - 1, 0)])

    @pl.when(jnp.logical_and(valid > 0, new_expert))
    def _():
        wgu_ref[:, 0:D_EXPERT] = wg_ref[...].astype(BF16)
        wgu_ref[:, D_EXPERT:2 * D_EXPERT] = wu_ref[...].astype(BF16)
        wdb_ref[...] = wd_ref[...].astype(BF16)

    @pl.when(valid > 0)
    def _():
        blk = ROW_TILE // EXPERT_ROW_BLOCKS
        rows = [slice(b * blk, (b + 1) * blk) for b in range(EXPERT_ROW_BLOCKS)]
        gu = [_dot(xs_ref[r, 0:D_MODEL], wgu_ref[...]) for r in rows]
        for r, gu_b in zip(rows, gu):
            g, u = gu_b[:, 0:D_EXPERT], gu_b[:, D_EXPERT:2 * D_EXPERT]
            hid = (g * jax.nn.sigmoid(g) * u).astype(BF16)
            gate = jnp.sum(xs_ref[r, D_MODEL:PAYLOAD_W].astype(F32), axis=-1, keepdims=True)
            ys_ref[r, :] = (gate * _dot(hid, wdb_ref[...])).astype(BF16)

    @pl.when(valid == 0)
    def _():
        ys_ref[...] = jnp.zeros_like(ys_ref)


def _expert_layer(layer, tile_e, tile_valid, xs, wg, wu, wd):
    n_steps = tile_e.shape[0]
    return pl.pallas_call(
        _expert_kernel,
        out_shape=jax.ShapeDtypeStruct((xs.shape[0], D_MODEL), BF16),
        grid_spec=pltpu.PrefetchScalarGridSpec(
            num_scalar_prefetch=2,
            grid=(n_steps,),
            in_specs=[
                pl.BlockSpec((ROW_TILE, PAYLOAD_W), lambda i, e, v: (i, 0)),
                pl.BlockSpec((None, None, D_MODEL, D_EXPERT), lambda i, e, v: (layer, e[i], 0, 0)),
                pl.BlockSpec((None, None, D_MODEL, D_EXPERT), lambda i, e, v: (layer, e[i], 0, 0)),
                pl.BlockSpec((None, None, D_EXPERT, D_MODEL), lambda i, e, v: (layer, e[i], 0, 0)),
            ],
            out_specs=pl.BlockSpec((ROW_TILE, D_MODEL), lambda i, e, v: (i, 0)),
            scratch_shapes=[
                pltpu.VMEM((D_MODEL, 2 * D_EXPERT), BF16),
                pltpu.VMEM((D_EXPERT, D_MODEL), BF16),
            ],
        ),
        compiler_params=pltpu.CompilerParams(dimension_semantics=("arbitrary",), vmem_limit_bytes=VMEM_LIMIT),
        name="experts",
    )(tile_e, tile_valid, xs, wg, wu, wd)


def _combine_kernel(n_tab, cnt_tab, base_tab, x_ref, metat_ref, ys_hbm, l2w_ref, l2b_ref, o_ref, yc_ref, sem):
    i = pl.program_id(0)
    n_steps = pl.num_programs(0)
    slot = lax.rem(i, 2)

    def copies(tile, slot_, act):
        def make_copy(buf_row, ys_row, rows):
            return pltpu.make_async_copy(ys_hbm.at[pl.ds(ys_row, rows)],
                                         yc_ref.at[slot_, pl.ds(buf_row, rows)], sem.at[slot_])
        _for_each_run(n_tab, cnt_tab, base_tab, tile, make_copy, act)

    @pl.when(i == 0)
    def _():
        yc_ref[...] = jnp.zeros_like(yc_ref)
        copies(0, 0, lambda cp: cp.start())

    @pl.when(i + 1 < n_steps)
    def _():
        copies(i + 1, 1 - slot, lambda cp: cp.start())

    copies(i, slot, lambda cp: cp.wait())

    t_r = lax.broadcasted_iota(jnp.int32, (MOE_TM, MOE_TM), 0)
    t_c = lax.broadcasted_iota(jnp.int32, (MOE_TM, MOE_TM), 1)
    as_col = lambda row: jnp.sum(jnp.where(t_r == t_c, row, 0.0), axis=-1, keepdims=True)
    col_f = lax.broadcasted_iota(jnp.int32, (MOE_TM, COMPACT_ROWS), 1).astype(F32)
    mt = metat_ref[...]
    onehot = jnp.where(col_f == as_col(mt[0:1, :]), 1.0,
                       jnp.where(col_f == as_col(mt[1:2, :]), 1.0, 0.0)).astype(BF16)
    y = _dot(onehot, yc_ref[slot])
    x = x_ref[...].reshape(MOE_TM, D_MODEL)
    o_ref[...] = _layer_norm(DEEPNORM_ALPHA * x + y, l2w_ref[...], l2b_ref[...], LN_EPS).reshape(o_ref.shape)


def _combine_layer(layer, n_tab, cnt_tab, base_tab, x, metat, ys, l2w, l2b):
    batch, seq, _ = x.shape
    ln_spec = pl.BlockSpec((None, 1, D_MODEL), lambda i, *_: (layer, 0, 0))
    return pl.pallas_call(
        _combine_kernel,
        out_shape=jax.ShapeDtypeStruct(x.shape, F32),
        grid_spec=pltpu.PrefetchScalarGridSpec(
            num_scalar_prefetch=3,
            grid=(batch * seq // MOE_TM,),
            in_specs=[
                _token_step_spec(seq),
                _ROUTE_STEP_SPEC,
                pl.BlockSpec(memory_space=pl.ANY),
                ln_spec,
                ln_spec,
            ],
            out_specs=_token_step_spec(seq),
            scratch_shapes=[pltpu.VMEM((2, COMPACT_ROWS, D_MODEL), BF16), pltpu.SemaphoreType.DMA((2,))],
        ),
        compiler_params=pltpu.CompilerParams(dimension_semantics=("arbitrary",), vmem_limit_bytes=VMEM_LIMIT),
        name="combine",
    )(n_tab, cnt_tab, base_tab, x, metat, ys, l2w, l2b)


def _expert_plan(counts, n_steps):
    tiles_e = (counts + (ROW_TILE - 1)) // ROW_TILE
    cum = jnp.cumsum(tiles_e)
    first_tile = cum - tiles_e
    total = cum[-1]
    step = jnp.arange(n_steps, dtype=jnp.int32)
    owner = (step[:, None] >= cum[None, :]).sum(axis=1).astype(jnp.int32)
    is_owner = owner[:, None] == jnp.arange(N_EXPERTS, dtype=jnp.int32)[None, :]
    local = step - jnp.where(is_owner, first_tile[None, :], 0).sum(axis=1)
    rows_left = jnp.where(is_owner, counts[None, :], 0).sum(axis=1) - local * ROW_TILE
    valid = jnp.where(step < total, jnp.clip(rows_left, 0, ROW_TILE), 0).astype(jnp.int32)
    tile_e = jnp.minimum(owner, N_EXPERTS - 1)
    base = (first_tile * ROW_TILE).astype(jnp.int32)
    fill = jnp.stack([base + counts, tiles_e * ROW_TILE - counts], axis=1).reshape(-1)
    fill_tab = jnp.concatenate([fill, (total * ROW_TILE)[None]]).astype(jnp.int32)
    return base, tile_e, valid, fill_tab


def _moe_layer(layer, x, metat, tile_n, tile_cnt, wg, wu, wd, l2w, l2b):
    n_tok = x.shape[0] * x.shape[1]
    n_tiles = n_tok // MOE_TM
    n_steps = (2 * n_tok + N_EXPERTS * (ROW_ALIGN - 1) * n_tiles) // ROW_TILE + N_EXPERTS
    n_tab = tile_n[:, :, 0].reshape(-1)
    cnt_tab = tile_cnt[:, :, 0].reshape(-1)
    counts = tile_cnt[-1, :, 0] + tile_n[-1, :, 0]
    base_tab, tile_e, tile_valid, fill_tab = _expert_plan(counts, n_steps)
    xs = _dispatch_layer(n_tab, cnt_tab, base_tab, fill_tab, x, metat, n_steps * ROW_TILE)
    ys = _expert_layer(layer, tile_e, tile_valid, xs, wg, wu, wd)
    return _combine_layer(layer, n_tab, cnt_tab, base_tab, x, metat, ys, l2w, l2b)


def kernel(x, w_in, b_in, mlstm_norm_w, gmlp_norm_w, gmlp_norm_b, gmlp_ws, gmlp_bs, conv_w, conv_b,
           conv_norm_w, conv_norm_b, w_out, ln1_w, ln1_b, router_w, router_b, w_gate, w_up, w_down,
           ln2_w, ln2_b):
    batch, seq, d = x.shape

    q_lo, k_lo, v_lo = 0, ML_WIDTH, 2 * ML_WIDTH
    gate_lo = 4 * ML_WIDTH
    rest_lo = gate_lo + 2 * ML_HEADS

    def row_part(t):
        return jnp.concatenate([t[..., q_lo:k_lo], t[..., v_lo:gate_lo], t[..., rest_lo:]], axis=-1)

    def col_part(t):
        return jnp.concatenate([t[..., k_lo:v_lo], t[..., gate_lo:rest_lo]], axis=-1)

    tril = jnp.tril(jnp.ones((CHUNK, CHUNK), gmlp_ws.dtype))
    rw_pad = jnp.pad(router_w.astype(F32), ((0, 0), (0, LANES - N_EXPERTS)))
    rw_head = rw_pad.astype(BF16)
    rw_split = jnp.concatenate([rw_head, (rw_pad - rw_head.astype(F32)).astype(BF16)], axis=1)
    rb_col = router_b.astype(F32).reshape(N_EXPERTS, 1)

    rows_of = lambda t: t.reshape(DEPTH, 1, -1)
    b_row = rows_of(row_part(b_in))
    b_col = jnp.pad(col_part(b_in), ((0, 0), (0, P_COL_PAD - P_COL))).reshape(DEPTH, P_COL_PAD, 1)
    ws_cat = jnp.transpose(gmlp_ws * tril, (0, 2, 1, 3)).reshape(DEPTH, CHUNK, G_HEADS * CHUNK).astype(BF16)
    bs_full = jnp.repeat(jnp.transpose(gmlp_bs, (0, 2, 1)), G_HEAD_DIM, axis=2)
    cw = jnp.pad(conv_w, ((0, 0), (0, 1), (0, 0)))
    mixer_small = (b_row, b_col, rows_of(mlstm_norm_w), rows_of(gmlp_norm_w), rows_of(gmlp_norm_b), ws_cat, bs_full,
                   cw, rows_of(conv_b), rows_of(conv_norm_w), rows_of(conv_norm_b), rows_of(ln1_w), rows_of(ln1_b))

    w_in_t = jnp.swapaxes(w_in, 1, 2)

    for l in range(DEPTH):
        x, metat, tile_n, tile_cnt = _mixer_layer(l, x, w_in_t, w_out, mixer_small, rw_split, rb_col)
        x = _moe_layer(l, x, metat, tile_n, tile_cnt, w_gate, w_up, w_down, rows_of(ln2_w), rows_of(ln2_b))
    return x
```

```python
import functools

import jax
import jax.numpy as jnp
from jax import lax
from jax.experimental import pallas as pl
from jax.experimental.pallas import tpu as pltpu

D_MODEL = 1024
DEPTH = 4
ML_WIDTH = 512
ML_HEADS = 4
ML_HEAD_DIM = 128
CHUNK = 128
G_WIDTH = 256
G_HEADS = 4
G_HEAD_DIM = 64
C_WIDTH = 256
CONV_WIDTH = 31
N_EXPERTS = 16
N_GROUPS = 4
EXPERTS_PER_GROUP = 4
D_EXPERT = 512
DEEPNORM_ALPHA = (2.0 * DEPTH) ** 0.25
LN_EPS = 1e-5
K_SCALE = ML_HEAD_DIM ** -0.5

Q_LO, V_LO, O_LO, GU_LO, GV_LO, CA_LO, CB_LO = 0, 512, 1024, 1536, 1792, 2048, 2304
P_ROW = 2560
P_COL = ML_WIDTH + 2 * ML_HEADS
P_COL_PAD = ML_WIDTH + 16
IN_Q, IN_K, IN_VO, IN_GATES, IN_REST = (0, 512), (512, 1024), (1024, 2048), (2048, 2056), (2056, 3080)
P_IN = 3080
W_PREP_ROWS = 256

LANES = 128
SUBLANES = 8
MIX_TM = 256
ROW_ALIGN = 16
ROW_TILE = 512
EXPERT_ROW_BLOCKS = 2
PAYLOAD_W = D_MODEL + LANES
CONV_HALO = 32
CONV_SPAN = CONV_HALO - SUBLANES
STREAMS = 2
MOE_TM = STREAMS * MIX_TM
COMPACT_ROWS = 2 * MOE_TM + 2 * LANES
assert COMPACT_ROWS >= 2 * MOE_TM + N_EXPERTS * (ROW_ALIGN - 1)
GATE_PIECES = 3
LONG_PIECE = 2 * (2 * MOE_TM // N_EXPERTS)
VMEM_LIMIT = 58 * 1024 * 1024

F32 = jnp.float32
BF16 = jnp.bfloat16
NEG_INF = float("-inf")


def _layer_norm(x, w, b, eps):
    mu = jnp.mean(x, axis=-1, keepdims=True)
    xc = x - mu
    var = jnp.mean(xc * xc, axis=-1, keepdims=True)
    return xc * lax.rsqrt(var + eps) * w + b


def _gelu_tanh(x):
    return 0.5 * x * (1.0 + jnp.tanh(0.7978845608028654 * (x + 0.044715 * (x * x * x))))


def _log_sigmoid(x):
    return jnp.minimum(x, 0.0) - jnp.log1p(jnp.exp(-jnp.abs(x)))


def _dot(a, b):
    return jnp.dot(a, b, preferred_element_type=F32)


def _router_logits(x_new, rw_ref):
    xh = x_new.astype(BF16)
    xl = (x_new - xh.astype(F32)).astype(BF16)
    head = _dot(xh, rw_ref[...])
    return head[:, 0:LANES] + head[:, LANES:2 * LANES] + _dot(xl, rw_ref[:, 0:LANES])


def _route_tile(logits, rb_ref, metat_ref, tile_n_ref, tile_cnt_ref, cnt_ref):
    tm = logits.shape[1]
    s_all = jax.nn.sigmoid(logits)
    sel_all = s_all + rb_ref[...]
    rows_of = lambda v: [v[k:k + 1, :] for k in range(N_EXPERTS)]
    s, sel = rows_of(s_all), rows_of(sel_all)

    best = None
    for g in range(N_GROUPS):
        r = sel[EXPERTS_PER_GROUP * g:EXPERTS_PER_GROUP * (g + 1)]
        hi01, lo01 = jnp.maximum(r[0], r[1]), jnp.minimum(r[0], r[1])
        hi23, lo23 = jnp.maximum(r[2], r[3]), jnp.minimum(r[2], r[3])
        score = jnp.maximum(hi01, hi23) + jnp.maximum(jnp.minimum(hi01, hi23), jnp.maximum(lo01, lo23))
        if best is None:
            best, gidx = score, jnp.zeros(score.shape, jnp.int32)
        else:
            better = score > best
            gidx = jnp.where(better, g, gidx)
            best = jnp.where(better, score, best)

    def of_group(rows):
        out = []
        for j in range(EXPERTS_PER_GROUP):
            v = rows[j]
            for g in range(1, N_GROUPS):
                v = jnp.where(gidx == g, rows[EXPERTS_PER_GROUP * g + j], v)
            out.append(v)
        return out

    def first_max(vals):
        best_v, best_j = vals[0], jnp.zeros(vals[0].shape, F32)
        for j in range(1, len(vals)):
            better = vals[j] > best_v
            best_j = jnp.where(better, float(j), best_j)
            best_v = jnp.where(better, vals[j], best_v)
        return best_j

    def take(vals, idx):
        v = vals[0]
        for j in range(1, len(vals)):
            v = jnp.where(idx == float(j), vals[j], v)
        return v

    cand, cand_s = of_group(sel), of_group(s)
    j1 = first_max(cand)
    j2 = first_max([jnp.where(j1 == float(j), NEG_INF, cand[j]) for j in range(EXPERTS_PER_GROUP)])
    g1, g2 = take(cand_s, j1), take(cand_s, j2)
    tot = g1 + g2
    first_of_group = gidx.astype(F32) * float(EXPERTS_PER_GROUP)
    e1, e2 = first_of_group + j1, first_of_group + j2

    e_f = lax.broadcasted_iota(jnp.int32, (N_EXPERTS, tm), 0).astype(F32)
    assign = jnp.where(e_f == e1, 1.0, jnp.where(e_f == e2, 1.0, 0.0)).astype(BF16)
    t_r = lax.broadcasted_iota(jnp.int32, (tm, tm), 0)
    t_c = lax.broadcasted_iota(jnp.int32, (tm, tm), 1)
    earlier = jnp.where(t_r < t_c, 1.0, 0.0).astype(BF16)
    rank = _dot(assign, earlier)
    n_b = _dot(assign, jnp.ones((tm, LANES), BF16))
    n_up_b = jnp.floor((n_b + (ROW_ALIGN - 1.0)) * (1.0 / ROW_ALIGN)) * ROW_ALIGN
    x_r = lax.broadcasted_iota(jnp.int32, (N_EXPERTS, N_EXPERTS), 0)
    x_c = lax.broadcasted_iota(jnp.int32, (N_EXPERTS, N_EXPERTS), 1)
    lower = jnp.where(x_c < x_r, 1.0, 0.0).astype(BF16)
    off_b = _dot(lower, n_up_b.astype(BF16))
    pos = rows_of(jnp.concatenate([off_b] * (tm // LANES), axis=1) + rank)
    metat_ref[...] = jnp.concatenate(
        [take(pos, e1), take(pos, e2), g1 / tot, g2 / tot, jnp.zeros((SUBLANES - 4, tm), F32)], axis=0)

    cnt = cnt_ref[...]
    tile_n_ref[...] = n_up_b.astype(jnp.int32)
    tile_cnt_ref[...] = cnt.astype(jnp.int32)
    cnt_ref[...] = cnt + n_up_b


def _project(x_tile, w_row_ref, b_row_ref, w_col_ref, b_col_ref, proj_ref, colp_ref):
    xb = x_tile.astype(BF16)
    proj_ref[...] = _dot(xb, w_row_ref[...]) + b_row_ref[...]
    colp_ref[...] = _dot(w_col_ref[...], x_tile.T.astype(BF16)) + b_col_ref[...]


def _mix_tile(x, proj_ref, colp_ref, mnw_ref, gnw_ref, gnb_ref, ws_ref, bs_ref, cw_ref, cb_ref, cnw_ref,
              cnb_ref, w_out_ref, l1w_ref, l1b_ref, hcat_ref, cstate_ref, mstate_ref, cbuf_ref, shift_ref, tm):
    n_chunks = tm // CHUNK
    gates = colp_ref[ML_WIDTH:P_COL, :]
    lane_in_chunk = lax.broadcasted_iota(jnp.int32, gates.shape, 1) % CHUNK
    logf_all = _log_sigmoid(gates)
    bcum = logf_all
    d = 1
    while d < CHUNK:
        bcum = bcum + jnp.where(lane_in_chunk >= d, pltpu.roll(bcum, d, 1), 0.0)
        d *= 2

    row_i = lax.broadcasted_iota(jnp.int32, (CHUNK, CHUNK), 0)
    col_i = lax.broadcasted_iota(jnp.int32, (CHUNK, CHUNK), 1)
    causal = col_i <= row_i
    diag = col_i == row_i
    ones_col = jnp.where(col_i == 0, 1.0, 0.0).astype(BF16)

    for sc in range(STREAMS * n_chunks):
        st, c = divmod(sc, n_chunks)
        r0 = st * tm + c * CHUNK
        rows = slice(r0, r0 + CHUNK)
        m_rows = slice(st * SUBLANES, st * SUBLANES + ML_HEADS)
        logi = gates[0:ML_HEADS, r0:r0 + CHUNK]
        logf = logf_all[ML_HEADS:2 * ML_HEADS, r0:r0 + CHUNK]
        b_row = bcum[ML_HEADS:2 * ML_HEADS, r0:r0 + CHUNK]
        g = jnp.sum(logf, axis=-1, keepdims=True)
        m_prev = mstate_ref[m_rows, 0:1]
        a_row = g - b_row + logi
        m_new = jnp.maximum(g + m_prev, jnp.max(a_row, axis=-1, keepdims=True))
        w_row = jnp.exp(a_row - m_new) * K_SCALE
        decay = jnp.exp(g + m_prev - m_new)
        mstate_ref[m_rows, :] = jnp.broadcast_to(m_new, (ML_HEADS, LANES))

        for h in range(ML_HEADS):
            hs = slice(h * ML_HEAD_DIM, (h + 1) * ML_HEAD_DIM)
            q = proj_ref[rows, Q_LO + h * ML_HEAD_DIM:Q_LO + (h + 1) * ML_HEAD_DIM].astype(BF16)
            v = proj_ref[rows, V_LO + h * ML_HEAD_DIM:V_LO + (h + 1) * ML_HEAD_DIM].astype(BF16)
            o_gate = proj_ref[rows, O_LO + h * ML_HEAD_DIM:O_LO + (h + 1) * ML_HEAD_DIM]
            kt = colp_ref[hs, r0:r0 + CHUNK]
            v_ext = jnp.concatenate([v, ones_col], axis=1)

            b_r = b_row[h:h + 1, :]
            b_c = jnp.sum(jnp.where(diag, b_r, 0.0), axis=-1, keepdims=True)
            log_d = jnp.where(causal, b_c - b_r + logi[h:h + 1, :], NEG_INF)
            mp = m_prev[h:h + 1, :]
            log_inter = b_c + mp
            m_row = jnp.maximum(log_inter, jnp.max(log_d, axis=-1, keepdims=True))
            c_ext = cstate_ref[st * ML_HEADS + h]
            from_q = _dot(q, jnp.concatenate([kt.astype(BF16), c_ext.astype(BF16)], axis=1))
            p = from_q[:, 0:CHUNK] * (jnp.exp(log_d - m_row) * K_SCALE)
            w_inter = jnp.exp(log_inter - m_row)
            ktw = (kt * w_row[h:h + 1, :]).astype(BF16)
            onto_v = _dot(jnp.concatenate([p.astype(BF16), ktw], axis=0), v_ext)
            tot = onto_v[0:CHUNK, :] + w_inter * from_q[:, CHUNK:]
            den = jnp.maximum(jnp.abs(tot[:, ML_HEAD_DIM:ML_HEAD_DIM + 1]), jnp.exp(-m_row))
            hh = tot[:, 0:ML_HEAD_DIM] / den
            cstate_ref[st * ML_HEADS + h] = decay[h:h + 1, :] * c_ext + onto_v[CHUNK:, :]

            mu = jnp.mean(hh, axis=-1, keepdims=True)
            hc = hh - mu
            var = jnp.mean(hc * hc, axis=-1, keepdims=True)
            hn = hc * lax.rsqrt(var + 1e-6) * mnw_ref[:, hs]
            hcat_ref[rows, hs] = (jax.nn.sigmoid(o_gate) * hn).astype(BF16)

        u = _gelu_tanh(proj_ref[rows, GU_LO:GU_LO + G_WIDTH])
        z = _layer_norm(_gelu_tanh(proj_ref[rows, GV_LO:GV_LO + G_WIDTH]), gnw_ref[...], gnb_ref[...], LN_EPS)
        lane_head = lax.broadcasted_iota(jnp.int32, (CHUNK, G_WIDTH), 1) // G_HEAD_DIM
        z_bd = jnp.concatenate(
            [jnp.where(lane_head == h, z, 0.0).astype(BF16) for h in range(G_HEADS)], axis=0)
        zs = _dot(ws_ref[...], z_bd) + bs_ref[...]
        hcat_ref[rows, ML_WIDTH:ML_WIDTH + G_WIDTH] = (u * zs).astype(BF16)

        ca = proj_ref[rows, CA_LO:CA_LO + C_WIDTH]
        cb = proj_ref[rows, CB_LO:CB_LO + C_WIDTH]
        cbuf_ref[st, CONV_HALO + c * CHUNK:CONV_HALO + (c + 1) * CHUNK, :] = ca * jax.nn.sigmoid(cb)

    first_tap = CONV_HALO - (CONV_WIDTH - 1)
    for st in range(STREAMS):
        for sh in range(1, SUBLANES):
            shift_ref[st, sh - 1] = cbuf_ref[st, sh:sh + tm + CONV_SPAN, :]
        for c in range(n_chunks):
            acc = jnp.zeros((CHUNK, C_WIDTH), F32) + cb_ref[...]
            for k in range(CONV_WIDTH):
                whole, sh = divmod(first_tap + k, SUBLANES)
                lo = c * CHUNK + whole * SUBLANES
                tap = cbuf_ref[st, lo:lo + CHUNK, :] if sh == 0 else shift_ref[st, sh - 1, lo:lo + CHUNK, :]
                acc = acc + cw_ref[k:k + 1, :] * tap
            cn = _layer_norm(acc, cnw_ref[...], cnb_ref[...], LN_EPS)
            r0 = st * tm + c * CHUNK
            hcat_ref[r0:r0 + CHUNK, ML_WIDTH + G_WIDTH:D_MODEL] = (cn * jax.nn.sigmoid(cn)).astype(BF16)
        cbuf_ref[st, 0:CONV_HALO, :] = cbuf_ref[st, tm:tm + CONV_HALO, :]

    y = _dot(hcat_ref[...], w_out_ref[...])
    x_new = _layer_norm(DEEPNORM_ALPHA * x + y, l1w_ref[...], l1b_ref[...], LN_EPS)
    return x_new


def _prepare_in_proj(w_t_ref, w_row_ref, w_col_ref):
    for (src_lo, src_hi), dst_lo in ((IN_Q, Q_LO), (IN_VO, V_LO), (IN_REST, GU_LO)):
        for off in range(0, src_hi - src_lo, W_PREP_ROWS):
            block = w_t_ref[src_lo + off:src_lo + off + W_PREP_ROWS, :]
            w_row_ref[:, dst_lo + off:dst_lo + off + W_PREP_ROWS] = block.T.astype(BF16)
    w_col_ref[0:ML_WIDTH, :] = w_t_ref[IN_K[0]:IN_K[1], :].astype(BF16)
    gates = w_t_ref[IN_GATES[0]:IN_GATES[1], :]
    w_col_ref[ML_WIDTH:P_COL_PAD, :] = jnp.concatenate(
        [gates, jnp.zeros((P_COL_PAD - P_COL, D_MODEL), F32)], axis=0).astype(BF16)


def _mixer_kernel(x_ref, w_in_t_ref, w_out_f32_ref, b_row_ref, b_col_ref, mnw_ref, gnw_ref, gnb_ref,
                  ws_ref, bs_ref, cw_ref, cb_ref, cnw_ref, cnb_ref, l1w_ref, l1b_ref, rw_ref, rb_ref,
                  o_ref, metat_ref, tile_n_ref, tile_cnt_ref,
                  w_row_ref, w_col_ref, w_out_ref, proj_ref, colp_ref, hcat_ref, cstate_ref, mstate_ref, cbuf_ref, shift_ref,
                  cnt_ref, *, tm, steps_per_seq):
    i = pl.program_id(0)

    @pl.when(i == 0)
    def _():
        cnt_ref[...] = jnp.zeros_like(cnt_ref)
        _prepare_in_proj(w_in_t_ref, w_row_ref, w_col_ref)
        for r0 in range(0, D_MODEL, W_PREP_ROWS):
            w_out_ref[r0:r0 + W_PREP_ROWS, :] = w_out_f32_ref[r0:r0 + W_PREP_ROWS, :].astype(BF16)

    @pl.when(i % steps_per_seq == 0)
    def _():
        cstate_ref[...] = jnp.zeros_like(cstate_ref)
        mstate_ref[...] = jnp.zeros_like(mstate_ref)
        cbuf_ref[:, 0:CONV_HALO, :] = jnp.zeros((STREAMS, CONV_HALO, C_WIDTH), F32)

    x = x_ref[...].reshape(STREAMS * tm, D_MODEL)
    _project(x, w_row_ref, b_row_ref, w_col_ref, b_col_ref, proj_ref, colp_ref)
    x_new = _mix_tile(x, proj_ref, colp_ref, mnw_ref, gnw_ref, gnb_ref, ws_ref, bs_ref, cw_ref, cb_ref, cnw_ref,
                      cnb_ref, w_out_ref, l1w_ref, l1b_ref, hcat_ref, cstate_ref, mstate_ref, cbuf_ref, shift_ref,
                      tm)
    o_ref[...] = x_new.reshape(STREAMS, tm, D_MODEL)
    _route_tile(_router_logits(x_new, rw_ref).T[0:N_EXPERTS, :], rb_ref, metat_ref, tile_n_ref, tile_cnt_ref, cnt_ref)


def _full(shape):
    nd = len(shape)
    return pl.BlockSpec(shape, lambda i, _nd=nd: (0,) * _nd, pipeline_mode=pl.Buffered(1))


def _layer_of(stacked, layer):
    nd = stacked.ndim - 1
    return pl.BlockSpec((None,) + stacked.shape[1:], lambda i, _nd=nd: (layer,) + (0,) * _nd,
                        pipeline_mode=pl.Buffered(1))


def _mixer_layer(layer, x, w_in_t, w_out, stacked_small, rw_split, rb_col):
    batch, seq, _ = x.shape
    tm = MIX_TM
    steps_per_seq = seq // tm
    n_steps = batch // STREAMS * steps_per_seq
    rows = STREAMS * tm
    kernel = functools.partial(_mixer_kernel, tm=tm, steps_per_seq=steps_per_seq)
    shared = (rw_split, rb_col)
    tile_spec = pl.BlockSpec((None, N_EXPERTS, LANES), lambda i: (i, 0, 0))
    step_spec = pl.BlockSpec((STREAMS, tm, D_MODEL), lambda i: (i // steps_per_seq, i % steps_per_seq, 0))
    return pl.pallas_call(
        kernel,
        out_shape=(
            jax.ShapeDtypeStruct(x.shape, F32),
            jax.ShapeDtypeStruct((n_steps, SUBLANES, rows), F32),
            jax.ShapeDtypeStruct((n_steps, N_EXPERTS, LANES), jnp.int32),
            jax.ShapeDtypeStruct((n_steps, N_EXPERTS, LANES), jnp.int32),
        ),
        grid=(n_steps,),
        in_specs=[step_spec] + [_layer_of(w, layer) for w in (w_in_t, w_out) + tuple(stacked_small)]
        + [_full(w.shape) for w in shared],
        out_specs=(step_spec, pl.BlockSpec((None, SUBLANES, rows), lambda i: (i, 0, 0)), tile_spec, tile_spec),
        scratch_shapes=[
            pltpu.VMEM((D_MODEL, P_ROW), BF16),
            pltpu.VMEM((P_COL_PAD, D_MODEL), BF16),
            pltpu.VMEM((D_MODEL, D_MODEL), BF16),
            pltpu.VMEM((rows, P_ROW), F32),
            pltpu.VMEM((P_COL_PAD, rows), F32),
            pltpu.VMEM((rows, D_MODEL), BF16),
            pltpu.VMEM((STREAMS * ML_HEADS, ML_HEAD_DIM, 2 * ML_HEAD_DIM), F32),
            pltpu.VMEM((STREAMS * SUBLANES, LANES), F32),
            pltpu.VMEM((STREAMS, tm + CONV_HALO, C_WIDTH), F32),
            pltpu.VMEM((STREAMS, SUBLANES - 1, tm + CONV_SPAN, C_WIDTH), F32),
            pltpu.VMEM((N_EXPERTS, LANES), F32),
        ],
        compiler_params=pltpu.CompilerParams(
            dimension_semantics=("arbitrary",), vmem_limit_bytes=VMEM_LIMIT),
        name="mixer",
    )(x, w_in_t, w_out, *stacked_small, *shared)


def _pow2_pieces(n, largest, act):
    piece = largest
    if largest >= LONG_PIECE:
        def long_piece(j, carry):
            act(j * LONG_PIECE, LONG_PIECE)
            return carry

        lax.fori_loop(0, lax.shift_right_logical(n, LONG_PIECE.bit_length() - 1), long_piece, jnp.int32(0))
        piece = LONG_PIECE // 2
    while piece >= ROW_ALIGN:
        start = jnp.bitwise_and(n, -2 * piece)

        @pl.when(jnp.bitwise_and(n, piece) != 0)
        def _(piece=piece, start=start):
            act(start, piece)
        piece //= 2


def _for_each_run(n_tab, cnt_tab, base_tab, tile, make_copy, act):
    def body(e, off):
        n = n_tab[tile * N_EXPERTS + e]
        base = base_tab[e] + cnt_tab[tile * N_EXPERTS + e]
        _pow2_pieces(n, MOE_TM, lambda start, size: act(make_copy(
            pl.multiple_of(off + start, ROW_ALIGN), pl.multiple_of(base + start, ROW_ALIGN), size)))
        return off + n

    lax.fori_loop(0, N_EXPERTS, body, jnp.int32(0))


def _dispatch_kernel(n_tab, cnt_tab, base_tab, fill_tab, x_ref, metat_ref, xs_hbm, comp_ref, zero_ref, sem,
                     *, n_rows):
    i = pl.program_id(0)
    last = pl.num_programs(0) - 1
    slot = lax.rem(i, 2)

    def copies(tile, slot_, act):
        def make_copy(src_row, dst_row, rows):
            return pltpu.make_async_copy(comp_ref.at[slot_, pl.ds(src_row, rows)],
                                         xs_hbm.at[pl.ds(dst_row, rows)], sem.at[slot_])
        _for_each_run(n_tab, cnt_tab, base_tab, tile, make_copy, act)

    def zero_fill(act):
        def zero_copy(dst_row, rows):
            return pltpu.make_async_copy(zero_ref.at[pl.ds(0, rows)],
                                         xs_hbm.at[pl.ds(pl.multiple_of(dst_row, ROW_ALIGN), rows)], sem.at[2])

        def per_expert(e, carry):
            first = fill_tab[2 * e]
            _pow2_pieces(fill_tab[2 * e + 1], ROW_TILE // 2, lambda start, size: act(zero_copy(first + start, size)))
            return carry

        lax.fori_loop(0, N_EXPERTS, per_expert, jnp.int32(0))

        def per_half_tile(k, carry):
            act(zero_copy(fill_tab[2 * N_EXPERTS] + k * (ROW_TILE // 2), ROW_TILE // 2))
            return carry

        lax.fori_loop(0, (n_rows - fill_tab[2 * N_EXPERTS]) // (ROW_TILE // 2), per_half_tile, jnp.int32(0))

    @pl.when(i == 0)
    def _():
        zero_ref[...] = jnp.zeros_like(zero_ref)
        zero_fill(lambda cp: cp.start())

    @pl.when(i >= 2)
    def _():
        copies(i - 2, slot, lambda cp: cp.wait())

    mt = metat_ref[...]
    row_f = lax.broadcasted_iota(jnp.int32, (COMPACT_ROWS, MOE_TM), 0).astype(F32)
    hit1 = row_f == mt[0:1, :]
    hit2 = row_f == mt[1:2, :]
    onehot = jnp.where(hit1, 1.0, jnp.where(hit2, 1.0, 0.0)).astype(BF16)
    x_tile = x_ref[...].reshape(MOE_TM, D_MODEL).astype(BF16)
    comp_ref[slot, :, 0:D_MODEL] = _dot(onehot, x_tile).astype(BF16)
    gate = jnp.sum(jnp.where(hit1, mt[2:3, :], jnp.where(hit2, mt[3:4, :], 0.0)), axis=-1, keepdims=True)
    lane = lax.broadcasted_iota(jnp.int32, (COMPACT_ROWS, LANES), 1)
    packed, rest = jnp.zeros((COMPACT_ROWS, LANES), F32), gate
    for piece in range(GATE_PIECES):
        head = rest.astype(BF16).astype(F32)
        packed = jnp.where(lane == piece, head, packed)
        rest = rest - head
    comp_ref[slot, :, D_MODEL:PAYLOAD_W] = packed.astype(BF16)
    copies(i, slot, lambda cp: cp.start())

    @pl.when(i == last)
    def _():
        @pl.when(i >= 1)
        def _():
            copies(i - 1, 1 - slot, lambda cp: cp.wait())
        copies(i, slot, lambda cp: cp.wait())
        zero_fill(lambda cp: cp.wait())


def _token_step_spec(seq):
    tiles_per_seq = seq // MIX_TM
    return pl.BlockSpec((STREAMS, MIX_TM, D_MODEL), lambda i, *_: (i // tiles_per_seq, i % tiles_per_seq, 0))


_ROUTE_STEP_SPEC = pl.BlockSpec((None, SUBLANES, MOE_TM), lambda i, *_: (i, 0, 0))


def _dispatch_layer(n_tab, cnt_tab, base_tab, fill_tab, x, metat, n_rows):
    batch, seq, _ = x.shape
    return pl.pallas_call(
        functools.partial(_dispatch_kernel, n_rows=n_rows),
        out_shape=jax.ShapeDtypeStruct((n_rows, PAYLOAD_W), BF16),
        grid_spec=pltpu.PrefetchScalarGridSpec(
            num_scalar_prefetch=4,
            grid=(batch * seq // MOE_TM,),
            in_specs=[_token_step_spec(seq), _ROUTE_STEP_SPEC],
            out_specs=pl.BlockSpec(memory_space=pl.ANY),
            scratch_shapes=[
                pltpu.VMEM((2, COMPACT_ROWS, PAYLOAD_W), BF16),
                pltpu.VMEM((ROW_TILE // 2, PAYLOAD_W), BF16),
                pltpu.SemaphoreType.DMA((3,)),
            ],
        ),
        compiler_params=pltpu.CompilerParams(dimension_semantics=("arbitrary",), vmem_limit_bytes=VMEM_LIMIT),
        name="dispatch",
    )(n_tab, cnt_tab, base_tab, fill_tab, x, metat)


def _expert_kernel(tile_e, tile_valid, xs_ref, wg_ref, wu_ref, wd_ref, ys_ref, wgu_ref, wdb_ref):
    i = pl.program_id(0)
    valid = tile_valid[i]
    new_expert = jnp.logical_or(i == 0, tile_e[i] != tile_e[jnp.maximum(i - 1, 0)])

    @pl.when(jnp.logical_and(valid > 0, new_expert))
    def _():
        wgu_ref[:, 0:D_EXPERT] = wg_ref[...].astype(BF16)
        wgu_ref[:, D_EXPERT:2 * D_EXPERT] = wu_ref[...].astype(BF16)
        wdb_ref[...] = wd_ref[...].astype(BF16)

    @pl.when(valid > 0)
    def _():
        blk = ROW_TILE // EXPERT_ROW_BLOCKS
        rows = [slice(b * blk, (b + 1) * blk) for b in range(EXPERT_ROW_BLOCKS)]
        gu = [_dot(xs_ref[r, 0:D_MODEL], wgu_ref[...]) for r in rows]
        for r, gu_b in zip(rows, gu):
            g, u = gu_b[:, 0:D_EXPERT], gu_b[:, D_EXPERT:2 * D_EXPERT]
            hid = (g * jax.nn.sigmoid(g) * u).astype(BF16)
            gate = jnp.sum(xs_ref[r, D_MODEL:PAYLOAD_W].astype(F32), axis=-1, keepdims=True)
            ys_ref[r, :] = (gate * _dot(hid, wdb_ref[...])).astype(BF16)

    @pl.when(valid == 0)
    def _():
        ys_ref[...] = jnp.zeros_like(ys_ref)


def _expert_layer(layer, tile_e, tile_valid, xs, wg, wu, wd):
    n_steps = tile_e.shape[0]
    return pl.pallas_call(
        _expert_kernel,
        out_shape=jax.ShapeDtypeStruct((xs.shape[0], D_MODEL), BF16),
        grid_spec=pltpu.PrefetchScalarGridSpec(
            num_scalar_prefetch=2,
            grid=(n_steps,),
            in_specs=[
                pl.BlockSpec((ROW_TILE, PAYLOAD_W), lambda i, e, v: (i, 0)),
                pl.BlockSpec((None, None, D_MODEL, D_EXPERT), lambda i, e, v: (layer, e[i], 0, 0)),
                pl.BlockSpec((None, None, D_MODEL, D_EXPERT), lambda i, e, v: (layer, e[i], 0, 0)),
                pl.BlockSpec((None, None, D_EXPERT, D_MODEL), lambda i, e, v: (layer, e[i], 0, 0)),
            ],
            out_specs=pl.BlockSpec((ROW_TILE, D_MODEL), lambda i, e, v: (i, 0)),
            scratch_shapes=[
                pltpu.VMEM((D_MODEL, 2 * D_EXPERT), BF16),
                pltpu.VMEM((D_EXPERT, D_MODEL), BF16),
            ],
        ),
        compiler_params=pltpu.CompilerParams(dimension_semantics=("arbitrary",), vmem_limit_bytes=VMEM_LIMIT),
        name="experts",
    )(tile_e, tile_valid, xs, wg, wu, wd)


def _combine_kernel(n_tab, cnt_tab, base_tab, x_ref, metat_ref, ys_hbm, l2w_ref, l2b_ref, o_ref, yc_ref, sem):
    i = pl.program_id(0)
    n_steps = pl.num_programs(0)
    slot = lax.rem(i, 2)

    def copies(tile, slot_, act):
        def make_copy(buf_row, ys_row, rows):
            return pltpu.make_async_copy(ys_hbm.at[pl.ds(ys_row, rows)],
                                         yc_ref.at[slot_, pl.ds(buf_row, rows)], sem.at[slot_])
        _for_each_run(n_tab, cnt_tab, base_tab, tile, make_copy, act)

    @pl.when(i == 0)
    def _():
        yc_ref[...] = jnp.zeros_like(yc_ref)
        copies(0, 0, lambda cp: cp.start())

    @pl.when(i + 1 < n_steps)
    def _():
        copies(i + 1, 1 - slot, lambda cp: cp.start())

    copies(i, slot, lambda cp: cp.wait())

    t_r = lax.broadcasted_iota(jnp.int32, (MOE_TM, MOE_TM), 0)
    t_c = lax.broadcasted_iota(jnp.int32, (MOE_TM, MOE_TM), 1)
    as_col = lambda row: jnp.sum(jnp.where(t_r == t_c, row, 0.0), axis=-1, keepdims=True)
    col_f = lax.broadcasted_iota(jnp.int32, (MOE_TM, COMPACT_ROWS), 1).astype(F32)
    mt = metat_ref[...]
    onehot = jnp.where(col_f == as_col(mt[0:1, :]), 1.0,
                       jnp.where(col_f == as_col(mt[1:2, :]), 1.0, 0.0)).astype(BF16)
    y = _dot(onehot, yc_ref[slot])
    x = x_ref[...].reshape(MOE_TM, D_MODEL)
    o_ref[...] = _layer_norm(DEEPNORM_ALPHA * x + y, l2w_ref[...], l2b_ref[...], LN_EPS).reshape(o_ref.shape)


def _combine_layer(layer, n_tab, cnt_tab, base_tab, x, metat, ys, l2w, l2b):
    batch, seq, _ = x.shape
    ln_spec = pl.BlockSpec((None, 1, D_MODEL), lambda i, *_: (layer, 0, 0))
    return pl.pallas_call(
        _combine_kernel,
        out_shape=jax.ShapeDtypeStruct(x.shape, F32),
        grid_spec=pltpu.PrefetchScalarGridSpec(
            num_scalar_prefetch=3,
            grid=(batch * seq // MOE_TM,),
            in_specs=[
                _token_step_spec(seq),
                _ROUTE_STEP_SPEC,
                pl.BlockSpec(memory_space=pl.ANY),
                ln_spec,
                ln_spec,
            ],
            out_specs=_token_step_spec(seq),
            scratch_shapes=[pltpu.VMEM((2, COMPACT_ROWS, D_MODEL), BF16), pltpu.SemaphoreType.DMA((2,))],
        ),
        compiler_params=pltpu.CompilerParams(dimension_semantics=("arbitrary",), vmem_limit_bytes=VMEM_LIMIT),
        name="combine",
    )(n_tab, cnt_tab, base_tab, x, metat, ys, l2w, l2b)


def _expert_plan(counts, n_steps):
    tiles_e = (counts + (ROW_TILE - 1)) // ROW_TILE
    cum = jnp.cumsum(tiles_e)
    first_tile = cum - tiles_e
    total = cum[-1]
    step = jnp.arange(n_steps, dtype=jnp.int32)
    owner = (step[:, None] >= cum[None, :]).sum(axis=1).astype(jnp.int32)
    is_owner = owner[:, None] == jnp.arange(N_EXPERTS, dtype=jnp.int32)[None, :]
    local = step - jnp.where(is_owner, first_tile[None, :], 0).sum(axis=1)
    rows_left = jnp.where(is_owner, counts[None, :], 0).sum(axis=1) - local * ROW_TILE
    valid = jnp.where(step < total, jnp.clip(rows_left, 0, ROW_TILE), 0).astype(jnp.int32)
    tile_e = jnp.minimum(owner, N_EXPERTS - 1)
    base = (first_tile * ROW_TILE).astype(jnp.int32)
    fill = jnp.stack([base + counts, tiles_e * ROW_TILE - counts], axis=1).reshape(-1)
    fill_tab = jnp.concatenate([fill, (total * ROW_TILE)[None]]).astype(jnp.int32)
    return base, tile_e, valid, fill_tab


def _moe_layer(layer, x, metat, tile_n, tile_cnt, wg, wu, wd, l2w, l2b):
    n_tok = x.shape[0] * x.shape[1]
    n_tiles = n_tok // MOE_TM
    n_steps = (2 * n_tok + N_EXPERTS * (ROW_ALIGN - 1) * n_tiles) // ROW_TILE + N_EXPERTS
    n_tab = tile_n[:, :, 0].reshape(-1)
    cnt_tab = tile_cnt[:, :, 0].reshape(-1)
    counts = tile_cnt[-1, :, 0] + tile_n[-1, :, 0]
    base_tab, tile_e, tile_valid, fill_tab = _expert_plan(counts, n_steps)
    xs = _dispatch_layer(n_tab, cnt_tab, base_tab, fill_tab, x, metat, n_steps * ROW_TILE)
    ys = _expert_layer(layer, tile_e, tile_valid, xs, wg, wu, wd)
    return _combine_layer(layer, n_tab, cnt_tab, base_tab, x, metat, ys, l2w, l2b)


def kernel(x, w_in, b_in, mlstm_norm_w, gmlp_norm_w, gmlp_norm_b, gmlp_ws, gmlp_bs, conv_w, conv_b,
           conv_norm_w, conv_norm_b, w_out, ln1_w, ln1_b, router_w, router_b, w_gate, w_up, w_down,
           ln2_w, ln2_b):
    batch, seq, d = x.shape

    q_lo, k_lo, v_lo = 0, ML_WIDTH, 2 * ML_WIDTH
    gate_lo = 4 * ML_WIDTH
    rest_lo = gate_lo + 2 * ML_HEADS

    def row_part(t):
        return jnp.concatenate([t[..., q_lo:k_lo], t[..., v_lo:gate_lo], t[..., rest_lo:]], axis=-1)

    def col_part(t):
        return jnp.concatenate([t[..., k_lo:v_lo], t[..., gate_lo:rest_lo]], axis=-1)

    tril = jnp.tril(jnp.ones((CHUNK, CHUNK), gmlp_ws.dtype))
    rw_pad = jnp.pad(router_w.astype(F32), ((0, 0), (0, LANES - N_EXPERTS)))
    rw_head = rw_pad.astype(BF16)
    rw_split = jnp.concatenate([rw_head, (rw_pad - rw_head.astype(F32)).astype(BF16)], axis=1)
    rb_col = router_b.astype(F32).reshape(N_EXPERTS, 1)

    rows_of = lambda t: t.reshape(DEPTH, 1, -1)
    b_row = rows_of(row_part(b_in))
    b_col = jnp.pad(col_part(b_in), ((0, 0), (0, P_COL_PAD - P_COL))).reshape(DEPTH, P_COL_PAD, 1)
    ws_cat = jnp.transpose(gmlp_ws * tril, (0, 2, 1, 3)).reshape(DEPTH, CHUNK, G_HEADS * CHUNK).astype(BF16)
    bs_full = jnp.repeat(jnp.transpose(gmlp_bs, (0, 2, 1)), G_HEAD_DIM, axis=2)
    cw = jnp.pad(conv_w, ((0, 0), (0, 1), (0, 0)))
    mixer_small = (b_row, b_col, rows_of(mlstm_norm_w), rows_of(gmlp_norm_w), rows_of(gmlp_norm_b), ws_cat, bs_full,
                   cw, rows_of(conv_b), rows_of(conv_norm_w), rows_of(conv_norm_b), rows_of(ln1_w), rows_of(ln1_b))

    w_in_t = jnp.swapaxes(w_in, 1, 2)

    for l in range(DEPTH):
        x, metat, tile_n, tile_cnt = _mixer_layer(l, x, w_in_t, w_out, mixer_small, rw_split, rb_col)
        x = _moe_layer(l, x, metat, tile_n, tile_cnt, w_gate, w_up, w_down, rows_of(ln2_w), rows_of(ln2_b))
    return x
```

```python
import functools

import jax
import jax.numpy as jnp
from jax import lax
from jax.experimental import pallas as pl
from jax.experimental.pallas import tpu as pltpu

D_MODEL = 1024
DEPTH = 4
ML_WIDTH = 512
ML_HEADS = 4
ML_HEAD_DIM = 128
CHUNK = 128
G_WIDTH = 256
G_HEADS = 4
G_HEAD_DIM = 64
C_WIDTH = 256
CONV_WIDTH = 31
N_EXPERTS = 16
N_GROUPS = 4
EXPERTS_PER_GROUP = 4
D_EXPERT = 512
DEEPNORM_ALPHA = (2.0 * DEPTH) ** 0.25
LN_EPS = 1e-5
K_SCALE = ML_HEAD_DIM ** -0.5

Q_LO, V_LO, O_LO, GU_LO, GV_LO, CA_LO, CB_LO = 0, 512, 1024, 1536, 1792, 2048, 2304
P_ROW = 2560
P_COL = ML_WIDTH + 2 * ML_HEADS
P_COL_PAD = ML_WIDTH + 16
IN_Q, IN_K, IN_VO, IN_GATES, IN_REST = (0, 512), (512, 1024), (1024, 2048), (2048, 2056), (2056, 3080)
P_IN = 3080
W_PREP_ROWS = 256

LANES = 128
SUBLANES = 8
MIX_TM = 256
ROW_ALIGN = 16
ROW_TILE = 512
EXPERT_ROW_BLOCKS = 2
PAYLOAD_W = D_MODEL + LANES
CONV_HALO = 32
CONV_SPAN = CONV_HALO - SUBLANES
STREAMS = 2
MOE_TM = STREAMS * MIX_TM
COMPACT_ROWS = 2 * MOE_TM + 2 * LANES
assert COMPACT_ROWS >= 2 * MOE_TM + N_EXPERTS * (ROW_ALIGN - 1)
GATE_PIECES = 3
LONG_PIECE = 2 * (2 * MOE_TM // N_EXPERTS)
VMEM_LIMIT = 58 * 1024 * 1024

F32 = jnp.float32
BF16 = jnp.bfloat16
NEG_INF = float("-inf")


def _layer_norm(x, w, b, eps):
    mu = jnp.mean(x, axis=-1, keepdims=True)
    xc = x - mu
    var = jnp.mean(xc * xc, axis=-1, keepdims=True)
    return xc * lax.rsqrt(var + eps) * w + b


def _gelu_tanh(x):
    return 0.5 * x * (1.0 + jnp.tanh(0.7978845608028654 * (x + 0.044715 * (x * x * x))))


def _log_sigmoid(x):
    return jnp.minimum(x, 0.0) - jnp.log1p(jnp.exp(-jnp.abs(x)))


def _dot(a, b):
    return jnp.dot(a, b, preferred_element_type=F32)


def _router_logits(x_new, rw_ref):
    xh = x_new.astype(BF16)
    xl = (x_new - xh.astype(F32)).astype(BF16)
    head = _dot(xh, rw_ref[...])
    return head[:, 0:LANES] + head[:, LANES:2 * LANES] + _dot(xl, rw_ref[:, 0:LANES])


def _route_tile(logits, rb_ref, metat_ref, tile_n_ref, tile_cnt_ref, cnt_ref):
    tm = logits.shape[1]
    s_all = jax.nn.sigmoid(logits)
    sel_all = s_all + rb_ref[...]
    rows_of = lambda v: [v[k:k + 1, :] for k in range(N_EXPERTS)]
    s, sel = rows_of(s_all), rows_of(sel_all)

    best = None
    for g in range(N_GROUPS):
        r = sel[EXPERTS_PER_GROUP * g:EXPERTS_PER_GROUP * (g + 1)]
        hi01, lo01 = jnp.maximum(r[0], r[1]), jnp.minimum(r[0], r[1])
        hi23, lo23 = jnp.maximum(r[2], r[3]), jnp.minimum(r[2], r[3])
        score = jnp.maximum(hi01, hi23) + jnp.maximum(jnp.minimum(hi01, hi23), jnp.maximum(lo01, lo23))
        if best is None:
            best, gidx = score, jnp.zeros(score.shape, jnp.int32)
        else:
            better = score > best
            gidx = jnp.where(better, g, gidx)
            best = jnp.where(better, score, best)

    def of_group(rows):
        out = []
        for j in range(EXPERTS_PER_GROUP):
            v = rows[j]
            for g in range(1, N_GROUPS):
                v = jnp.where(gidx == g, rows[EXPERTS_PER_GROUP * g + j], v)
            out.append(v)
        return out

    def first_max(vals):
        best_v, best_j = vals[0], jnp.zeros(vals[0].shape, F32)
        for j in range(1, len(vals)):
            better = vals[j] > best_v
            best_j = jnp.where(better, float(j), best_j)
            best_v = jnp.where(better, vals[j], best_v)
        return best_j

    def take(vals, idx):
        v = vals[0]
        for j in range(1, len(vals)):
            v = jnp.where(idx == float(j), vals[j], v)
        return v

    cand, cand_s = of_group(sel), of_group(s)
    j1 = first_max(cand)
    j2 = first_max([jnp.where(j1 == float(j), NEG_INF, cand[j]) for j in range(EXPERTS_PER_GROUP)])
    g1, g2 = take(cand_s, j1), take(cand_s, j2)
    tot = g1 + g2
    first_of_group = gidx.astype(F32) * float(EXPERTS_PER_GROUP)
    e1, e2 = first_of_group + j1, first_of_group + j2

    e_f = lax.broadcasted_iota(jnp.int32, (N_EXPERTS, tm), 0).astype(F32)
    assign = jnp.where(e_f == e1, 1.0, jnp.where(e_f == e2, 1.0, 0.0)).astype(BF16)
    t_r = lax.broadcasted_iota(jnp.int32, (tm, tm), 0)
    t_c = lax.broadcasted_iota(jnp.int32, (tm, tm), 1)
    earlier = jnp.where(t_r < t_c, 1.0, 0.0).astype(BF16)
    rank = _dot(assign, earlier)
    n_b = _dot(assign, jnp.ones((tm, LANES), BF16))
    n_up_b = jnp.floor((n_b + (ROW_ALIGN - 1.0)) * (1.0 / ROW_ALIGN)) * ROW_ALIGN
    x_r = lax.broadcasted_iota(jnp.int32, (N_EXPERTS, N_EXPERTS), 0)
    x_c = lax.broadcasted_iota(jnp.int32, (N_EXPERTS, N_EXPERTS), 1)
    lower = jnp.where(x_c < x_r, 1.0, 0.0).astype(BF16)
    off_b = _dot(lower, n_up_b.astype(BF16))
    pos = rows_of(jnp.concatenate([off_b] * (tm // LANES), axis=1) + rank)
    metat_ref[...] = jnp.concatenate(
        [take(pos, e1), take(pos, e2), g1 / tot, g2 / tot, jnp.zeros((SUBLANES - 4, tm), F32)], axis=0)

    cnt = cnt_ref[...]
    tile_n_ref[...] = n_up_b.astype(jnp.int32)
    tile_cnt_ref[...] = cnt.astype(jnp.int32)
    cnt_ref[...] = cnt + n_up_b


def _project(x_tile, w_row_ref, b_row_ref, w_col_ref, b_col_ref, proj_ref, colp_ref):
    xb = x_tile.astype(BF16)
    proj_ref[...] = _dot(xb, w_row_ref[...]) + b_row_ref[...]
    colp_ref[...] = _dot(w_col_ref[...], x_tile.T.astype(BF16)) + b_col_ref[...]


def _mix_tile(x, proj_ref, colp_ref, mnw_ref, gnw_ref, gnb_ref, ws_ref, bs_ref, cw_ref, cb_ref, cnw_ref,
              cnb_ref, w_out_ref, l1w_ref, l1b_ref, hcat_ref, cstate_ref, mstate_ref, cbuf_ref, shift_ref, tm):
    n_chunks = tm // CHUNK
    gates = colp_ref[ML_WIDTH:P_COL, :]
    lane_in_chunk = lax.broadcasted_iota(jnp.int32, gates.shape, 1) % CHUNK
    logf_all = _log_sigmoid(gates)
    bcum = logf_all
    d = 1
    while d < CHUNK:
        bcum = bcum + jnp.where(lane_in_chunk >= d, pltpu.roll(bcum, d, 1), 0.0)
        d *= 2

    row_i = lax.broadcasted_iota(jnp.int32, (CHUNK, CHUNK), 0)
    col_i = lax.broadcasted_iota(jnp.int32, (CHUNK, CHUNK), 1)
    causal = col_i <= row_i
    diag = col_i == row_i
    ones_col = jnp.where(col_i == 0, 1.0, 0.0).astype(BF16)

    for sc in range(STREAMS * n_chunks):
        st, c = divmod(sc, n_chunks)
        r0 = st * tm + c * CHUNK
        rows = slice(r0, r0 + CHUNK)
        m_rows = slice(st * SUBLANES, st * SUBLANES + ML_HEADS)
        logi = gates[0:ML_HEADS, r0:r0 + CHUNK]
        logf = logf_all[ML_HEADS:2 * ML_HEADS, r0:r0 + CHUNK]
        b_row = bcum[ML_HEADS:2 * ML_HEADS, r0:r0 + CHUNK]
        g = jnp.sum(logf, axis=-1, keepdims=True)
        m_prev = mstate_ref[m_rows, 0:1]
        a_row = g - b_row + logi
        m_new = jnp.maximum(g + m_prev, jnp.max(a_row, axis=-1, keepdims=True))
        w_row = jnp.exp(a_row - m_new) * K_SCALE
        decay = jnp.exp(g + m_prev - m_new)
        mstate_ref[m_rows, :] = jnp.broadcast_to(m_new, (ML_HEADS, LANES))

        for h in range(ML_HEADS):
            hs = slice(h * ML_HEAD_DIM, (h + 1) * ML_HEAD_DIM)
            q = proj_ref[rows, Q_LO + h * ML_HEAD_DIM:Q_LO + (h + 1) * ML_HEAD_DIM].astype(BF16)
            v = proj_ref[rows, V_LO + h * ML_HEAD_DIM:V_LO + (h + 1) * ML_HEAD_DIM].astype(BF16)
            o_gate = proj_ref[rows, O_LO + h * ML_HEAD_DIM:O_LO + (h + 1) * ML_HEAD_DIM]
            kt = colp_ref[hs, r0:r0 + CHUNK]
            v_ext = jnp.concatenate([v, ones_col], axis=1)

            b_r = b_row[h:h + 1, :]
            b_c = jnp.sum(jnp.where(diag, b_r, 0.0), axis=-1, keepdims=True)
            log_d = jnp.where(causal, b_c - b_r + logi[h:h + 1, :], NEG_INF)
            mp = m_prev[h:h + 1, :]
            log_inter = b_c + mp
            m_row = jnp.maximum(log_inter, jnp.max(log_d, axis=-1, keepdims=True))
            c_ext = cstate_ref[st * ML_HEADS + h]
            from_q = _dot(q, jnp.concatenate([kt.astype(BF16), c_ext.astype(BF16)], axis=1))
            p = from_q[:, 0:CHUNK] * (jnp.exp(log_d - m_row) * K_SCALE)
            w_inter = jnp.exp(log_inter - m_row)
            ktw = (kt * w_row[h:h + 1, :]).astype(BF16)
            onto_v = _dot(jnp.concatenate([p.astype(BF16), ktw], axis=0), v_ext)
            tot = onto_v[0:CHUNK, :] + w_inter * from_q[:, CHUNK:]
            den = jnp.maximum(jnp.abs(tot[:, ML_HEAD_DIM:ML_HEAD_DIM + 1]), jnp.exp(-m_row))
            hh = tot[:, 0:ML_HEAD_DIM] / den
            cstate_ref[st * ML_HEADS + h] = decay[h:h + 1, :] * c_ext + onto_v[CHUNK:, :]

            mu = jnp.mean(hh, axis=-1, keepdims=True)
            hc = hh - mu
            var = jnp.mean(hc * hc, axis=-1, keepdims=True)
            hn = hc * lax.rsqrt(var + 1e-6) * mnw_ref[:, hs]
            hcat_ref[rows, hs] = (jax.nn.sigmoid(o_gate) * hn).astype(BF16)

        u = _gelu_tanh(proj_ref[rows, GU_LO:GU_LO + G_WIDTH])
        z = _layer_norm(_gelu_tanh(proj_ref[rows, GV_LO:GV_LO + G_WIDTH]), gnw_ref[...], gnb_ref[...], LN_EPS)
        lane_head = lax.broadcasted_iota(jnp.int32, (CHUNK, G_WIDTH), 1) // G_HEAD_DIM
        z_bd = jnp.concatenate(
            [jnp.where(lane_head == h, z, 0.0).astype(BF16) for h in range(G_HEADS)], axis=0)
        zs = _dot(ws_ref[...], z_bd) + bs_ref[...]
        hcat_ref[rows, ML_WIDTH:ML_WIDTH + G_WIDTH] = (u * zs).astype(BF16)

        ca = proj_ref[rows, CA_LO:CA_LO + C_WIDTH]
        cb = proj_ref[rows, CB_LO:CB_LO + C_WIDTH]
        cbuf_ref[st, CONV_HALO + c * CHUNK:CONV_HALO + (c + 1) * CHUNK, :] = ca * jax.nn.sigmoid(cb)

    first_tap = CONV_HALO - (CONV_WIDTH - 1)
    for st in range(STREAMS):
        for sh in range(1, SUBLANES):
            shift_ref[st, sh - 1] = cbuf_ref[st, sh:sh + tm + CONV_SPAN, :]
        for c in range(n_chunks):
            acc = jnp.zeros((CHUNK, C_WIDTH), F32) + cb_ref[...]
            for k in range(CONV_WIDTH):
                whole, sh = divmod(first_tap + k, SUBLANES)
                lo = c * CHUNK + whole * SUBLANES
                tap = cbuf_ref[st, lo:lo + CHUNK, :] if sh == 0 else shift_ref[st, sh - 1, lo:lo + CHUNK, :]
                acc = acc + cw_ref[k:k + 1, :] * tap
            cn = _layer_norm(acc, cnw_ref[...], cnb_ref[...], LN_EPS)
            r0 = st * tm + c * CHUNK
            hcat_ref[r0:r0 + CHUNK, ML_WIDTH + G_WIDTH:D_MODEL] = (cn * jax.nn.sigmoid(cn)).astype(BF16)
        cbuf_ref[st, 0:CONV_HALO, :] = cbuf_ref[st, tm:tm + CONV_HALO, :]

    y = _dot(hcat_ref[...], w_out_ref[...])
    x_new = _layer_norm(DEEPNORM_ALPHA * x + y, l1w_ref[...], l1b_ref[...], LN_EPS)
    return x_new


def _prepare_in_proj(w_t_ref, w_row_ref, w_col_ref):
    for (src_lo, src_hi), dst_lo in ((IN_Q, Q_LO), (IN_VO, V_LO), (IN_REST, GU_LO)):
        for off in range(0, src_hi - src_lo, W_PREP_ROWS):
            block = w_t_ref[src_lo + off:src_lo + off + W_PREP_ROWS, :]
            w_row_ref[:, dst_lo + off:dst_lo + off + W_PREP_ROWS] = block.T.astype(BF16)
    w_col_ref[0:ML_WIDTH, :] = w_t_ref[IN_K[0]:IN_K[1], :].astype(BF16)
    gates = w_t_ref[IN_GATES[0]:IN_GATES[1], :]
    w_col_ref[ML_WIDTH:P_COL_PAD, :] = jnp.concatenate(
        [gates, jnp.zeros((P_COL_PAD - P_COL, D_MODEL), F32)], axis=0).astype(BF16)


def _mixer_kernel(x_ref, w_in_t_ref, w_out_f32_ref, b_row_ref, b_col_ref, mnw_ref, gnw_ref, gnb_ref,
                  ws_ref, bs_ref, cw_ref, cb_ref, cnw_ref, cnb_ref, l1w_ref, l1b_ref, rw_ref, rb_ref,
                  o_ref, metat_ref, tile_n_ref, tile_cnt_ref,
                  w_row_ref, w_col_ref, w_out_ref, proj_ref, colp_ref, hcat_ref, cstate_ref, mstate_ref, cbuf_ref, shift_ref,
                  cnt_ref, *, tm, steps_per_seq):
    i = pl.program_id(0)

    @pl.when(i == 0)
    def _():
        cnt_ref[...] = jnp.zeros_like(cnt_ref)
        _prepare_in_proj(w_in_t_ref, w_row_ref, w_col_ref)
        for r0 in range(0, D_MODEL, W_PREP_ROWS):
            w_out_ref[r0:r0 + W_PREP_ROWS, :] = w_out_f32_ref[r0:r0 + W_PREP_ROWS, :].astype(BF16)

    @pl.when(i % steps_per_seq == 0)
    def _():
        cstate_ref[...] = jnp.zeros_like(cstate_ref)
        mstate_ref[...] = jnp.zeros_like(mstate_ref)
        cbuf_ref[:, 0:CONV_HALO, :] = jnp.zeros((STREAMS, CONV_HALO, C_WIDTH), F32)

    x = x_ref[...].reshape(STREAMS * tm, D_MODEL)
    _project(x, w_row_ref, b_row_ref, w_col_ref, b_col_ref, proj_ref, colp_ref)
    x_new = _mix_tile(x, proj_ref, colp_ref, mnw_ref, gnw_ref, gnb_ref, ws_ref, bs_ref, cw_ref, cb_ref, cnw_ref,
                      cnb_ref, w_out_ref, l1w_ref, l1b_ref, hcat_ref, cstate_ref, mstate_ref, cbuf_ref, shift_ref,
                      tm)
    o_ref[...] = x_new.reshape(STREAMS, tm, D_MODEL)
    _route_tile(_router_logits(x_new, rw_ref).T[0:N_EXPERTS, :], rb_ref, metat_ref, tile_n_ref, tile_cnt_ref, cnt_ref)


def _full(shape):
    nd = len(shape)
    return pl.BlockSpec(shape, lambda i, _nd=nd: (0,) * _nd, pipeline_mode=pl.Buffered(1))


def _layer_of(stacked, layer):
    nd = stacked.ndim - 1
    return pl.BlockSpec((None,) + stacked.shape[1:], lambda i, _nd=nd: (layer,) + (0,) * _nd,
                        pipeline_mode=pl.Buffered(1))


def _mixer_layer(layer, x, w_in_t, w_out, stacked_small, rw_split, rb_col):
    batch, seq, _ = x.shape
    tm = MIX_TM
    steps_per_seq = seq // tm
    n_steps = batch // STREAMS * steps_per_seq
    rows = STREAMS * tm
    kernel = functools.partial(_mixer_kernel, tm=tm, steps_per_seq=steps_per_seq)
    shared = (rw_split, rb_col)
    tile_spec = pl.BlockSpec((None, N_EXPERTS, LANES), lambda i: (i, 0, 0))
    step_spec = pl.BlockSpec((STREAMS, tm, D_MODEL), lambda i: (i // steps_per_seq, i % steps_per_seq, 0))
    return pl.pallas_call(
        kernel,
        out_shape=(
            jax.ShapeDtypeStruct(x.shape, F32),
            jax.ShapeDtypeStruct((n_steps, SUBLANES, rows), F32),
            jax.ShapeDtypeStruct((n_steps, N_EXPERTS, LANES), jnp.int32),
            jax.ShapeDtypeStruct((n_steps, N_EXPERTS, LANES), jnp.int32),
        ),
        grid=(n_steps,),
        in_specs=[step_spec] + [_layer_of(w, layer) for w in (w_in_t, w_out) + tuple(stacked_small)]
        + [_full(w.shape) for w in shared],
        out_specs=(step_spec, pl.BlockSpec((None, SUBLANES, rows), lambda i: (i, 0, 0)), tile_spec, tile_spec),
        scratch_shapes=[
            pltpu.VMEM((D_MODEL, P_ROW), BF16),
            pltpu.VMEM((P_COL_PAD, D_MODEL), BF16),
            pltpu.VMEM((D_MODEL, D_MODEL), BF16),
            pltpu.VMEM((rows, P_ROW), F32),
            pltpu.VMEM((P_COL_PAD, rows), F32),
            pltpu.VMEM((rows, D_MODEL), BF16),
            pltpu.VMEM((STREAMS * ML_HEADS, ML_HEAD_DIM, 2 * ML_HEAD_DIM), F32),
            pltpu.VMEM((STREAMS * SUBLANES, LANES), F32),
            pltpu.VMEM((STREAMS, tm + CONV_HALO, C_WIDTH), F32),
            pltpu.VMEM((STREAMS, SUBLANES - 1, tm + CONV_SPAN, C_WIDTH), F32),
            pltpu.VMEM((N_EXPERTS, LANES), F32),
        ],
        compiler_params=pltpu.CompilerParams(
            dimension_semantics=("arbitrary",), vmem_limit_bytes=VMEM_LIMIT),
        name="mixer",
    )(x, w_in_t, w_out, *stacked_small, *shared)


def _pow2_pieces(n, largest, act):
    piece = largest
    if largest >= LONG_PIECE:
        def long_piece(j, carry):
            act(j * LONG_PIECE, LONG_PIECE)
            return carry

        lax.fori_loop(0, lax.shift_right_logical(n, LONG_PIECE.bit_length() - 1), long_piece, jnp.int32(0))
        piece = LONG_PIECE // 2
    while piece >= ROW_ALIGN:
        start = jnp.bitwise_and(n, -2 * piece)

        @pl.when(jnp.bitwise_and(n, piece) != 0)
        def _(piece=piece, start=start):
            act(start, piece)
        piece //= 2


def _for_each_run(n_tab, cnt_tab, base_tab, tile, make_copy, act):
    def body(e, off):
        n = n_tab[tile * N_EXPERTS + e]
        base = base_tab[e] + cnt_tab[tile * N_EXPERTS + e]
        _pow2_pieces(n, MOE_TM, lambda start, size: act(make_copy(
            pl.multiple_of(off + start, ROW_ALIGN), pl.multiple_of(base + start, ROW_ALIGN), size)))
        return off + n

    lax.fori_loop(0, N_EXPERTS, body, jnp.int32(0))


def _dispatch_kernel(n_tab, cnt_tab, base_tab, fill_tab, x_ref, metat_ref, xs_hbm, comp_ref, zero_ref, sem,
                     *, n_rows):
    i = pl.program_id(0)
    last = pl.num_programs(0) - 1
    slot = lax.rem(i, 2)

    def copies(tile, slot_, act):
        def make_copy(src_row, dst_row, rows):
            return pltpu.make_async_copy(comp_ref.at[slot_, pl.ds(src_row, rows)],
                                         xs_hbm.at[pl.ds(dst_row, rows)], sem.at[slot_])
        _for_each_run(n_tab, cnt_tab, base_tab, tile, make_copy, act)

    def drain(tile, slot_):
        total = n_tab[pl.num_programs(0) * N_EXPERTS + tile]
        _pow2_pieces(total, MOE_TM, lambda start, size: pltpu.make_async_copy(
            comp_ref.at[slot_, pl.ds(0, size)], xs_hbm.at[pl.ds(0, size)], sem.at[slot_]).wait())

    def zero_fill(act):
        def zero_copy(dst_row, rows):
            return pltpu.make_async_copy(zero_ref.at[pl.ds(0, rows)],
                                         xs_hbm.at[pl.ds(pl.multiple_of(dst_row, ROW_ALIGN), rows)], sem.at[2])

        def per_expert(e, carry):
            first = fill_tab[2 * e]
            _pow2_pieces(fill_tab[2 * e + 1], ROW_TILE // 2, lambda start, size: act(zero_copy(first + start, size)))
            return carry

        lax.fori_loop(0, N_EXPERTS, per_expert, jnp.int32(0))

        def per_half_tile(k, carry):
            act(zero_copy(fill_tab[2 * N_EXPERTS] + k * (ROW_TILE // 2), ROW_TILE // 2))
            return carry

        lax.fori_loop(0, (n_rows - fill_tab[2 * N_EXPERTS]) // (ROW_TILE // 2), per_half_tile, jnp.int32(0))

    @pl.when(i == 0)
    def _():
        zero_ref[...] = jnp.zeros_like(zero_ref)
        zero_fill(lambda cp: cp.start())

    @pl.when(i >= 2)
    def _():
        drain(i - 2, slot)

    mt = metat_ref[...]
    row_f = lax.broadcasted_iota(jnp.int32, (COMPACT_ROWS, MOE_TM), 0).astype(F32)
    hit1 = row_f == mt[0:1, :]
    hit2 = row_f == mt[1:2, :]
    onehot = jnp.where(hit1, 1.0, jnp.where(hit2, 1.0, 0.0)).astype(BF16)
    x_tile = x_ref[...].reshape(MOE_TM, D_MODEL).astype(BF16)
    comp_ref[slot, :, 0:D_MODEL] = _dot(onehot, x_tile).astype(BF16)
    gate = jnp.sum(jnp.where(hit1, mt[2:3, :], jnp.where(hit2, mt[3:4, :], 0.0)), axis=-1, keepdims=True)
    lane = lax.broadcasted_iota(jnp.int32, (COMPACT_ROWS, LANES), 1)
    packed, rest = jnp.zeros((COMPACT_ROWS, LANES), F32), gate
    for piece in range(GATE_PIECES):
        head = rest.astype(BF16).astype(F32)
        packed = jnp.where(lane == piece, head, packed)
        rest = rest - head
    comp_ref[slot, :, D_MODEL:PAYLOAD_W] = packed.astype(BF16)
    copies(i, slot, lambda cp: cp.start())

    @pl.when(i == last)
    def _():
        @pl.when(i >= 1)
        def _():
            drain(i - 1, 1 - slot)
        drain(i, slot)
        zero_fill(lambda cp: cp.wait())


def _token_step_spec(seq):
    tiles_per_seq = seq // MIX_TM
    return pl.BlockSpec((STREAMS, MIX_TM, D_MODEL), lambda i, *_: (i // tiles_per_seq, i % tiles_per_seq, 0))


_ROUTE_STEP_SPEC = pl.BlockSpec((None, SUBLANES, MOE_TM), lambda i, *_: (i, 0, 0))


def _dispatch_layer(n_tab, cnt_tab, base_tab, fill_tab, x, metat, n_rows):
    batch, seq, _ = x.shape
    return pl.pallas_call(
        functools.partial(_dispatch_kernel, n_rows=n_rows),
        out_shape=jax.ShapeDtypeStruct((n_rows, PAYLOAD_W), BF16),
        grid_spec=pltpu.PrefetchScalarGridSpec(
            num_scalar_prefetch=4,
            grid=(batch * seq // MOE_TM,),
            in_specs=[_token_step_spec(seq), _ROUTE_STEP_SPEC],
            out_specs=pl.BlockSpec(memory_space=pl.ANY),
            scratch_shapes=[
                pltpu.VMEM((2, COMPACT_ROWS, PAYLOAD_W), BF16),
                pltpu.VMEM((ROW_TILE // 2, PAYLOAD_W), BF16),
                pltpu.SemaphoreType.DMA((3,)),
            ],
        ),
        compiler_params=pltpu.CompilerParams(dimension_semantics=("arbitrary",), vmem_limit_bytes=VMEM_LIMIT),
        name="dispatch",
    )(n_tab, cnt_tab, base_tab, fill_tab, x, metat)


def _expert_kernel(tile_e, tile_valid, xs_ref, wg_ref, wu_ref, wd_ref, ys_ref, wgu_ref, wdb_ref):
    i = pl.program_id(0)
    valid = tile_valid[i]
    new_expert = jnp.logical_or(i == 0, tile_e[i] != tile_e[jnp.maximum(i - 1, 0)])

    @pl.when(jnp.logical_and(valid > 0, new_expert))
    def _():
        wgu_ref[:, 0:D_EXPERT] = wg_ref[...].astype(BF16)
        wgu_ref[:, D_EXPERT:2 * D_EXPERT] = wu_ref[...].astype(BF16)
        wdb_ref[...] = wd_ref[...].astype(BF16)

    @pl.when(valid > 0)
    def _():
        blk = ROW_TILE // EXPERT_ROW_BLOCKS
        rows = [slice(b * blk, (b + 1) * blk) for b in range(EXPERT_ROW_BLOCKS)]
        gu = [_dot(xs_ref[r, 0:D_MODEL], wgu_ref[...]) for r in rows]
        for r, gu_b in zip(rows, gu):
            g, u = gu_b[:, 0:D_EXPERT], gu_b[:, D_EXPERT:2 * D_EXPERT]
            hid = (g * jax.nn.sigmoid(g) * u).astype(BF16)
            gate = jnp.sum(xs_ref[r, D_MODEL:PAYLOAD_W].astype(F32), axis=-1, keepdims=True)
            ys_ref[r, :] = (gate * _dot(hid, wdb_ref[...])).astype(BF16)

    @pl.when(valid == 0)
    def _():
        ys_ref[...] = jnp.zeros_like(ys_ref)


def _expert_layer(layer, tile_e, tile_valid, xs, wg, wu, wd):
    n_steps = tile_e.shape[0]
    return pl.pallas_call(
        _expert_kernel,
        out_shape=jax.ShapeDtypeStruct((xs.shape[0], D_MODEL), BF16),
        grid_spec=pltpu.PrefetchScalarGridSpec(
            num_scalar_prefetch=2,
            grid=(n_steps,),
            in_specs=[
                pl.BlockSpec((ROW_TILE, PAYLOAD_W), lambda i, e, v: (i, 0)),
                pl.BlockSpec((None, None, D_MODEL, D_EXPERT), lambda i, e, v: (layer, e[i], 0, 0)),
                pl.BlockSpec((None, None, D_MODEL, D_EXPERT), lambda i, e, v: (layer, e[i], 0, 0)),
                pl.BlockSpec((None, None, D_EXPERT, D_MODEL), lambda i, e, v: (layer, e[i], 0, 0)),
            ],
            out_specs=pl.BlockSpec((ROW_TILE, D_MODEL), lambda i, e, v: (i, 0)),
            scratch_shapes=[
                pltpu.VMEM((D_MODEL, 2 * D_EXPERT), BF16),
                pltpu.VMEM((D_EXPERT, D_MODEL), BF16),
            ],
        ),
        compiler_params=pltpu.CompilerParams(dimension_semantics=("arbitrary",), vmem_limit_bytes=VMEM_LIMIT),
        name="experts",
    )(tile_e, tile_valid, xs, wg, wu, wd)


def _combine_kernel(n_tab, cnt_tab, base_tab, x_ref, metat_ref, ys_hbm, l2w_ref, l2b_ref, o_ref, yc_ref, sem):
    i = pl.program_id(0)
    n_steps = pl.num_programs(0)
    slot = lax.rem(i, 2)

    def copies(tile, slot_, act):
        def make_copy(buf_row, ys_row, rows):
            return pltpu.make_async_copy(ys_hbm.at[pl.ds(ys_row, rows)],
                                         yc_ref.at[slot_, pl.ds(buf_row, rows)], sem.at[slot_])
        _for_each_run(n_tab, cnt_tab, base_tab, tile, make_copy, act)

    def drain(tile, slot_):
        total = n_tab[n_steps * N_EXPERTS + tile]
        _pow2_pieces(total, MOE_TM, lambda start, size: pltpu.make_async_copy(
            ys_hbm.at[pl.ds(0, size)], yc_ref.at[slot_, pl.ds(0, size)], sem.at[slot_]).wait())

    @pl.when(i == 0)
    def _():
        yc_ref[...] = jnp.zeros_like(yc_ref)
        copies(0, 0, lambda cp: cp.start())

    @pl.when(i + 1 < n_steps)
    def _():
        copies(i + 1, 1 - slot, lambda cp: cp.start())

    drain(i, slot)

    t_r = lax.broadcasted_iota(jnp.int32, (MOE_TM, MOE_TM), 0)
    t_c = lax.broadcasted_iota(jnp.int32, (MOE_TM, MOE_TM), 1)
    as_col = lambda row: jnp.sum(jnp.where(t_r == t_c, row, 0.0), axis=-1, keepdims=True)
    col_f = lax.broadcasted_iota(jnp.int32, (MOE_TM, COMPACT_ROWS), 1).astype(F32)
    mt = metat_ref[...]
    onehot = jnp.where(col_f == as_col(mt[0:1, :]), 1.0,
                       jnp.where(col_f == as_col(mt[1:2, :]), 1.0, 0.0)).astype(BF16)
    y = _dot(onehot, yc_ref[slot])
    x = x_ref[...].reshape(MOE_TM, D_MODEL)
    o_ref[...] = _layer_norm(DEEPNORM_ALPHA * x + y, l2w_ref[...], l2b_ref[...], LN_EPS).reshape(o_ref.shape)


def _combine_layer(layer, n_tab, cnt_tab, base_tab, x, metat, ys, l2w, l2b):
    batch, seq, _ = x.shape
    ln_spec = pl.BlockSpec((None, 1, D_MODEL), lambda i, *_: (layer, 0, 0))
    return pl.pallas_call(
        _combine_kernel,
        out_shape=jax.ShapeDtypeStruct(x.shape, F32),
        grid_spec=pltpu.PrefetchScalarGridSpec(
            num_scalar_prefetch=3,
            grid=(batch * seq // MOE_TM,),
            in_specs=[
                _token_step_spec(seq),
                _ROUTE_STEP_SPEC,
                pl.BlockSpec(memory_space=pl.ANY),
                ln_spec,
                ln_spec,
            ],
            out_specs=_token_step_spec(seq),
            scratch_shapes=[pltpu.VMEM((2, COMPACT_ROWS, D_MODEL), BF16), pltpu.SemaphoreType.DMA((2,))],
        ),
        compiler_params=pltpu.CompilerParams(dimension_semantics=("arbitrary",), vmem_limit_bytes=VMEM_LIMIT),
        name="combine",
    )(n_tab, cnt_tab, base_tab, x, metat, ys, l2w, l2b)


def _expert_plan(counts, n_steps):
    tiles_e = (counts + (ROW_TILE - 1)) // ROW_TILE
    cum = jnp.cumsum(tiles_e)
    first_tile = cum - tiles_e
    total = cum[-1]
    step = jnp.arange(n_steps, dtype=jnp.int32)
    owner = (step[:, None] >= cum[None, :]).sum(axis=1).astype(jnp.int32)
    is_owner = owner[:, None] == jnp.arange(N_EXPERTS, dtype=jnp.int32)[None, :]
    local = step - jnp.where(is_owner, first_tile[None, :], 0).sum(axis=1)
    rows_left = jnp.where(is_owner, counts[None, :], 0).sum(axis=1) - local * ROW_TILE
    valid = jnp.where(step < total, jnp.clip(rows_left, 0, ROW_TILE), 0).astype(jnp.int32)
    tile_e = jnp.minimum(owner, N_EXPERTS - 1)
    base = (first_tile * ROW_TILE).astype(jnp.int32)
    fill = jnp.stack([base + counts, tiles_e * ROW_TILE - counts], axis=1).reshape(-1)
    fill_tab = jnp.concatenate([fill, (total * ROW_TILE)[None]]).astype(jnp.int32)
    return base, tile_e, valid, fill_tab


def _moe_layer(layer, x, metat, tile_n, tile_cnt, wg, wu, wd, l2w, l2b):
    n_tok = x.shape[0] * x.shape[1]
    n_tiles = n_tok // MOE_TM
    n_steps = (2 * n_tok + N_EXPERTS * (ROW_ALIGN - 1) * n_tiles) // ROW_TILE + N_EXPERTS
    per_tile = tile_n[:, :, 0]
    n_tab = jnp.concatenate([per_tile.reshape(-1), per_tile.sum(axis=1)])
    cnt_tab = tile_cnt[:, :, 0].reshape(-1)
    counts = tile_cnt[-1, :, 0] + tile_n[-1, :, 0]
    base_tab, tile_e, tile_valid, fill_tab = _expert_plan(counts, n_steps)
    xs = _dispatch_layer(n_tab, cnt_tab, base_tab, fill_tab, x, metat, n_steps * ROW_TILE)
    ys = _expert_layer(layer, tile_e, tile_valid, xs, wg, wu, wd)
    return _combine_layer(layer, n_tab, cnt_tab, base_tab, x, metat, ys, l2w, l2b)


def kernel(x, w_in, b_in, mlstm_norm_w, gmlp_norm_w, gmlp_norm_b, gmlp_ws, gmlp_bs, conv_w, conv_b,
           conv_norm_w, conv_norm_b, w_out, ln1_w, ln1_b, router_w, router_b, w_gate, w_up, w_down,
           ln2_w, ln2_b):
    batch, seq, d = x.shape

    q_lo, k_lo, v_lo = 0, ML_WIDTH, 2 * ML_WIDTH
    gate_lo = 4 * ML_WIDTH
    rest_lo = gate_lo + 2 * ML_HEADS

    def row_part(t):
        return jnp.concatenate([t[..., q_lo:k_lo], t[..., v_lo:gate_lo], t[..., rest_lo:]], axis=-1)

    def col_part(t):
        return jnp.concatenate([t[..., k_lo:v_lo], t[..., gate_lo:rest_lo]], axis=-1)

    tril = jnp.tril(jnp.ones((CHUNK, CHUNK), gmlp_ws.dtype))
    rw_pad = jnp.pad(router_w.astype(F32), ((0, 0), (0, LANES - N_EXPERTS)))
    rw_head = rw_pad.astype(BF16)
    rw_split = jnp.concatenate([rw_head, (rw_pad - rw_head.astype(F32)).astype(BF16)], axis=1)
    rb_col = router_b.astype(F32).reshape(N_EXPERTS, 1)

    rows_of = lambda t: t.reshape(DEPTH, 1, -1)
    b_row = rows_of(row_part(b_in))
    b_col = jnp.pad(col_part(b_in), ((0, 0), (0, P_COL_PAD - P_COL))).reshape(DEPTH, P_COL_PAD, 1)
    ws_cat = jnp.transpose(gmlp_ws * tril, (0, 2, 1, 3)).reshape(DEPTH, CHUNK, G_HEADS * CHUNK).astype(BF16)
    bs_full = jnp.repeat(jnp.transpose(gmlp_bs, (0, 2, 1)), G_HEAD_DIM, axis=2)
    cw = jnp.pad(conv_w, ((0, 0), (0, 1), (0, 0)))
    mixer_small = (b_row, b_col, rows_of(mlstm_norm_w), rows_of(gmlp_norm_w), rows_of(gmlp_norm_b), ws_cat, bs_full,
                   cw, rows_of(conv_b), rows_of(conv_norm_w), rows_of(conv_norm_b), rows_of(ln1_w), rows_of(ln1_b))

    w_in_t = jnp.swapaxes(w_in, 1, 2)

    for l in range(DEPTH):
        x, metat, tile_n, tile_cnt = _mixer_layer(l, x, w_in_t, w_out, mixer_small, rw_split, rb_col)
        x = _moe_layer(l, x, metat, tile_n, tile_cnt, w_gate, w_up, w_down, rows_of(ln2_w), rows_of(ln2_b))
    return x
```

```python
import functools

import jax
import jax.numpy as jnp
from jax import lax
from jax.experimental import pallas as pl
from jax.experimental.pallas import tpu as pltpu

D_MODEL = 1024
DEPTH = 4
ML_WIDTH = 512
ML_HEADS = 4
ML_HEAD_DIM = 128
CHUNK = 128
G_WIDTH = 256
G_HEADS = 4
G_HEAD_DIM = 64
C_WIDTH = 256
CONV_WIDTH = 31
N_EXPERTS = 16
N_GROUPS = 4
EXPERTS_PER_GROUP = 4
D_EXPERT = 512
DEEPNORM_ALPHA = (2.0 * DEPTH) ** 0.25
LN_EPS = 1e-5
K_SCALE = ML_HEAD_DIM ** -0.5

Q_LO, V_LO, O_LO, GU_LO, GV_LO, CA_LO, CB_LO = 0, 512, 1024, 1536, 1792, 2048, 2304
P_ROW = 2560
P_COL = ML_WIDTH + 2 * ML_HEADS
P_COL_PAD = ML_WIDTH + 16
IN_Q, IN_K, IN_VO, IN_GATES, IN_REST = (0, 512), (512, 1024), (1024, 2048), (2048, 2056), (2056, 3080)
P_IN = 3080
W_PREP_ROWS = 256

LANES = 128
SUBLANES = 8
MIX_TM = 256
ROW_ALIGN = 16
ROW_TILE = 512
EXPERT_ROW_BLOCKS = 2
PAYLOAD_W = D_MODEL + LANES
CONV_HALO = 32
CONV_SPAN = CONV_HALO - SUBLANES
STREAMS = 2
MOE_TM = STREAMS * MIX_TM
COMPACT_ROWS = 2 * MOE_TM + 2 * LANES
assert COMPACT_ROWS >= 2 * MOE_TM + N_EXPERTS * (ROW_ALIGN - 1)
GATE_PIECES = 3
LONG_PIECE = 2 * (2 * MOE_TM // N_EXPERTS)
VMEM_LIMIT = 58 * 1024 * 1024

F32 = jnp.float32
BF16 = jnp.bfloat16
NEG_INF = float("-inf")


def _layer_norm(x, w, b, eps):
    mu = jnp.mean(x, axis=-1, keepdims=True)
    xc = x - mu
    var = jnp.mean(xc * xc, axis=-1, keepdims=True)
    return xc * lax.rsqrt(var + eps) * w + b


def _gelu_tanh(x):
    return 0.5 * x * (1.0 + jnp.tanh(0.7978845608028654 * (x + 0.044715 * (x * x * x))))


def _log_sigmoid(x):
    return jnp.minimum(x, 0.0) - jnp.log1p(jnp.exp(-jnp.abs(x)))


def _dot(a, b):
    return jnp.dot(a, b, preferred_element_type=F32)


def _router_logits(x_new, rw_ref):
    xh = x_new.astype(BF16)
    xl = (x_new - xh.astype(F32)).astype(BF16)
    head = _dot(xh, rw_ref[...])
    return head[:, 0:LANES] + head[:, LANES:2 * LANES] + _dot(xl, rw_ref[:, 0:LANES])


def _route_tile(logits, rb_ref, metat_ref, tile_n_ref, tile_cnt_ref, cnt_ref):
    tm = logits.shape[1]
    s_all = jax.nn.sigmoid(logits)
    sel_all = s_all + rb_ref[...]
    rows_of = lambda v: [v[k:k + 1, :] for k in range(N_EXPERTS)]
    s, sel = rows_of(s_all), rows_of(sel_all)

    best = None
    for g in range(N_GROUPS):
        r = sel[EXPERTS_PER_GROUP * g:EXPERTS_PER_GROUP * (g + 1)]
        hi01, lo01 = jnp.maximum(r[0], r[1]), jnp.minimum(r[0], r[1])
        hi23, lo23 = jnp.maximum(r[2], r[3]), jnp.minimum(r[2], r[3])
        score = jnp.maximum(hi01, hi23) + jnp.maximum(jnp.minimum(hi01, hi23), jnp.maximum(lo01, lo23))
        if best is None:
            best, gidx = score, jnp.zeros(score.shape, jnp.int32)
        else:
            better = score > best
            gidx = jnp.where(better, g, gidx)
            best = jnp.where(better, score, best)

    def of_group(rows):
        out = []
        for j in range(EXPERTS_PER_GROUP):
            v = rows[j]
            for g in range(1, N_GROUPS):
                v = jnp.where(gidx == g, rows[EXPERTS_PER_GROUP * g + j], v)
            out.append(v)
        return out

    def first_max(vals):
        best_v, best_j = vals[0], jnp.zeros(vals[0].shape, F32)
        for j in range(1, len(vals)):
            better = vals[j] > best_v
            best_j = jnp.where(better, float(j), best_j)
            best_v = jnp.where(better, vals[j], best_v)
        return best_j

    def take(vals, idx):
        v = vals[0]
        for j in range(1, len(vals)):
            v = jnp.where(idx == float(j), vals[j], v)
        return v

    cand, cand_s = of_group(sel), of_group(s)
    j1 = first_max(cand)
    j2 = first_max([jnp.where(j1 == float(j), NEG_INF, cand[j]) for j in range(EXPERTS_PER_GROUP)])
    g1, g2 = take(cand_s, j1), take(cand_s, j2)
    tot = g1 + g2
    first_of_group = gidx.astype(F32) * float(EXPERTS_PER_GROUP)
    e1, e2 = first_of_group + j1, first_of_group + j2

    e_f = lax.broadcasted_iota(jnp.int32, (N_EXPERTS, tm), 0).astype(F32)
    assign = jnp.where(e_f == e1, 1.0, jnp.where(e_f == e2, 1.0, 0.0)).astype(BF16)
    t_r = lax.broadcasted_iota(jnp.int32, (tm, tm), 0)
    t_c = lax.broadcasted_iota(jnp.int32, (tm, tm), 1)
    earlier = jnp.where(t_r < t_c, 1.0, 0.0).astype(BF16)
    rank = _dot(assign, earlier)
    n_b = _dot(assign, jnp.ones((tm, LANES), BF16))
    n_up_b = jnp.floor((n_b + (ROW_ALIGN - 1.0)) * (1.0 / ROW_ALIGN)) * ROW_ALIGN
    x_r = lax.broadcasted_iota(jnp.int32, (N_EXPERTS, N_EXPERTS), 0)
    x_c = lax.broadcasted_iota(jnp.int32, (N_EXPERTS, N_EXPERTS), 1)
    lower = jnp.where(x_c < x_r, 1.0, 0.0).astype(BF16)
    off_b = _dot(lower, n_up_b.astype(BF16))
    pos = rows_of(jnp.concatenate([off_b] * (tm // LANES), axis=1) + rank)
    metat_ref[...] = jnp.concatenate(
        [take(pos, e1), take(pos, e2), g1 / tot, g2 / tot, jnp.zeros((SUBLANES - 4, tm), F32)], axis=0)

    cnt = cnt_ref[...]
    tile_n_ref[...] = n_up_b.astype(jnp.int32)
    tile_cnt_ref[...] = cnt.astype(jnp.int32)
    cnt_ref[...] = cnt + n_up_b


def _project(x_tile, w_row_ref, b_row_ref, w_col_ref, b_col_ref, proj_ref, colp_ref):
    xb = x_tile.astype(BF16)
    proj_ref[...] = _dot(xb, w_row_ref[...]) + b_row_ref[...]
    colp_ref[...] = _dot(w_col_ref[...], x_tile.T.astype(BF16)) + b_col_ref[...]


def _mix_tile(x, proj_ref, colp_ref, mnw_ref, gnw_ref, gnb_ref, ws_ref, bs_ref, cw_ref, cb_ref, cnw_ref,
              cnb_ref, w_out_ref, l1w_ref, l1b_ref, hcat_ref, cstate_ref, mstate_ref, cbuf_ref, shift_ref, tm):
    n_chunks = tm // CHUNK
    gates = colp_ref[ML_WIDTH:P_COL, :]
    lane_in_chunk = lax.broadcasted_iota(jnp.int32, gates.shape, 1) % CHUNK
    logf_all = _log_sigmoid(gates)
    bcum = logf_all
    d = 1
    while d < CHUNK:
        bcum = bcum + jnp.where(lane_in_chunk >= d, pltpu.roll(bcum, d, 1), 0.0)
        d *= 2

    row_i = lax.broadcasted_iota(jnp.int32, (CHUNK, CHUNK), 0)
    col_i = lax.broadcasted_iota(jnp.int32, (CHUNK, CHUNK), 1)
    causal = col_i <= row_i
    diag = col_i == row_i
    ones_col = jnp.where(col_i == 0, 1.0, 0.0).astype(BF16)

    for sc in range(STREAMS * n_chunks):
        st, c = divmod(sc, n_chunks)
        r0 = st * tm + c * CHUNK
        rows = slice(r0, r0 + CHUNK)
        m_rows = slice(st * SUBLANES, st * SUBLANES + ML_HEADS)
        logi = gates[0:ML_HEADS, r0:r0 + CHUNK]
        logf = logf_all[ML_HEADS:2 * ML_HEADS, r0:r0 + CHUNK]
        b_row = bcum[ML_HEADS:2 * ML_HEADS, r0:r0 + CHUNK]
        g = jnp.sum(logf, axis=-1, keepdims=True)
        m_prev = mstate_ref[m_rows, 0:1]
        a_row = g - b_row + logi
        m_new = jnp.maximum(g + m_prev, jnp.max(a_row, axis=-1, keepdims=True))
        w_row = jnp.exp(a_row - m_new) * K_SCALE
        decay = jnp.exp(g + m_prev - m_new)
        mstate_ref[m_rows, :] = jnp.broadcast_to(m_new, (ML_HEADS, LANES))

        for h in range(ML_HEADS):
            hs = slice(h * ML_HEAD_DIM, (h + 1) * ML_HEAD_DIM)
            q = proj_ref[rows, Q_LO + h * ML_HEAD_DIM:Q_LO + (h + 1) * ML_HEAD_DIM].astype(BF16)
            v = proj_ref[rows, V_LO + h * ML_HEAD_DIM:V_LO + (h + 1) * ML_HEAD_DIM].astype(BF16)
            o_gate = proj_ref[rows, O_LO + h * ML_HEAD_DIM:O_LO + (h + 1) * ML_HEAD_DIM]
            kt = colp_ref[hs, r0:r0 + CHUNK]
            v_ext = jnp.concatenate([v, ones_col], axis=1)

            b_r = b_row[h:h + 1, :]
            b_c = jnp.sum(jnp.where(diag, b_r, 0.0), axis=-1, keepdims=True)
            log_d = jnp.where(causal, b_c - b_r + logi[h:h + 1, :], NEG_INF)
            mp = m_prev[h:h + 1, :]
            log_inter = b_c + mp
            m_row = jnp.maximum(log_inter, jnp.max(log_d, axis=-1, keepdims=True))
            c_ext = cstate_ref[st * ML_HEADS + h]
            from_q = _dot(q, jnp.concatenate([kt.astype(BF16), c_ext.astype(BF16)], axis=1))
            p = from_q[:, 0:CHUNK] * (jnp.exp(log_d - m_row) * K_SCALE)
            w_inter = jnp.exp(log_inter - m_row)
            ktw = (kt * w_row[h:h + 1, :]).astype(BF16)
            onto_v = _dot(jnp.concatenate([p.astype(BF16), ktw], axis=0), v_ext)
            tot = onto_v[0:CHUNK, :] + w_inter * from_q[:, CHUNK:]
            den = jnp.maximum(jnp.abs(tot[:, ML_HEAD_DIM:ML_HEAD_DIM + 1]), jnp.exp(-m_row))
            hh = tot[:, 0:ML_HEAD_DIM] / den
            cstate_ref[st * ML_HEADS + h] = decay[h:h + 1, :] * c_ext + onto_v[CHUNK:, :]

            mu = jnp.mean(hh, axis=-1, keepdims=True)
            hc = hh - mu
            var = jnp.mean(hc * hc, axis=-1, keepdims=True)
            hn = hc * lax.rsqrt(var + 1e-6) * mnw_ref[:, hs]
            hcat_ref[rows, hs] = (jax.nn.sigmoid(o_gate) * hn).astype(BF16)

        u = _gelu_tanh(proj_ref[rows, GU_LO:GU_LO + G_WIDTH])
        z = _layer_norm(_gelu_tanh(proj_ref[rows, GV_LO:GV_LO + G_WIDTH]), gnw_ref[...], gnb_ref[...], LN_EPS)
        lane_head = lax.broadcasted_iota(jnp.int32, (CHUNK, G_WIDTH), 1) // G_HEAD_DIM
        z_bd = jnp.concatenate(
            [jnp.where(lane_head == h, z, 0.0).astype(BF16) for h in range(G_HEADS)], axis=0)
        zs = _dot(ws_ref[...], z_bd) + bs_ref[...]
        hcat_ref[rows, ML_WIDTH:ML_WIDTH + G_WIDTH] = (u * zs).astype(BF16)

        ca = proj_ref[rows, CA_LO:CA_LO + C_WIDTH]
        cb = proj_ref[rows, CB_LO:CB_LO + C_WIDTH]
        cbuf_ref[st, CONV_HALO + c * CHUNK:CONV_HALO + (c + 1) * CHUNK, :] = ca * jax.nn.sigmoid(cb)

    first_tap = CONV_HALO - (CONV_WIDTH - 1)
    for st in range(STREAMS):
        for sh in range(1, SUBLANES):
            shift_ref[st, sh - 1] = cbuf_ref[st, sh:sh + tm + CONV_SPAN, :]
        for c in range(n_chunks):
            acc = jnp.zeros((CHUNK, C_WIDTH), F32) + cb_ref[...]
            for k in range(CONV_WIDTH):
                whole, sh = divmod(first_tap + k, SUBLANES)
                lo = c * CHUNK + whole * SUBLANES
                tap = cbuf_ref[st, lo:lo + CHUNK, :] if sh == 0 else shift_ref[st, sh - 1, lo:lo + CHUNK, :]
                acc = acc + cw_ref[k:k + 1, :] * tap
            cn = _layer_norm(acc, cnw_ref[...], cnb_ref[...], LN_EPS)
            r0 = st * tm + c * CHUNK
            hcat_ref[r0:r0 + CHUNK, ML_WIDTH + G_WIDTH:D_MODEL] = (cn * jax.nn.sigmoid(cn)).astype(BF16)
        cbuf_ref[st, 0:CONV_HALO, :] = cbuf_ref[st, tm:tm + CONV_HALO, :]

    y = _dot(hcat_ref[...], w_out_ref[...])
    x_new = _layer_norm(DEEPNORM_ALPHA * x + y, l1w_ref[...], l1b_ref[...], LN_EPS)
    return x_new


def _prepare_in_proj(w_t_ref, w_row_ref, w_col_ref):
    for (src_lo, src_hi), dst_lo in ((IN_Q, Q_LO), (IN_VO, V_LO), (IN_REST, GU_LO)):
        for off in range(0, src_hi - src_lo, W_PREP_ROWS):
            block = w_t_ref[src_lo + off:src_lo + off + W_PREP_ROWS, :]
            w_row_ref[:, dst_lo + off:dst_lo + off + W_PREP_ROWS] = block.T.astype(BF16)
    w_col_ref[0:ML_WIDTH, :] = w_t_ref[IN_K[0]:IN_K[1], :].astype(BF16)
    gates = w_t_ref[IN_GATES[0]:IN_GATES[1], :]
    w_col_ref[ML_WIDTH:P_COL_PAD, :] = jnp.concatenate(
        [gates, jnp.zeros((P_COL_PAD - P_COL, D_MODEL), F32)], axis=0).astype(BF16)


def _mixer_kernel(x_ref, w_in_t_ref, w_out_f32_ref, b_row_ref, b_col_ref, mnw_ref, gnw_ref, gnb_ref,
                  ws_ref, bs_ref, cw_ref, cb_ref, cnw_ref, cnb_ref, l1w_ref, l1b_ref, rw_ref, rb_ref,
                  o_ref, metat_ref, tile_n_ref, tile_cnt_ref,
                  w_row_ref, w_col_ref, w_out_ref, proj_ref, colp_ref, hcat_ref, cstate_ref, mstate_ref, cbuf_ref, shift_ref,
                  cnt_ref, *, tm, steps_per_seq):
    i = pl.program_id(0)

    @pl.when(i == 0)
    def _():
        cnt_ref[...] = jnp.zeros_like(cnt_ref)
        _prepare_in_proj(w_in_t_ref, w_row_ref, w_col_ref)
        for r0 in range(0, D_MODEL, W_PREP_ROWS):
            w_out_ref[r0:r0 + W_PREP_ROWS, :] = w_out_f32_ref[r0:r0 + W_PREP_ROWS, :].astype(BF16)

    @pl.when(i % steps_per_seq == 0)
    def _():
        cstate_ref[...] = jnp.zeros_like(cstate_ref)
        mstate_ref[...] = jnp.zeros_like(mstate_ref)
        cbuf_ref[:, 0:CONV_HALO, :] = jnp.zeros((STREAMS, CONV_HALO, C_WIDTH), F32)

    x = x_ref[...].reshape(STREAMS * tm, D_MODEL)
    _project(x, w_row_ref, b_row_ref, w_col_ref, b_col_ref, proj_ref, colp_ref)
    x_new = _mix_tile(x, proj_ref, colp_ref, mnw_ref, gnw_ref, gnb_ref, ws_ref, bs_ref, cw_ref, cb_ref, cnw_ref,
                      cnb_ref, w_out_ref, l1w_ref, l1b_ref, hcat_ref, cstate_ref, mstate_ref, cbuf_ref, shift_ref,
                      tm)
    o_ref[...] = x_new.reshape(STREAMS, tm, D_MODEL)
    _route_tile(_router_logits(x_new, rw_ref).T[0:N_EXPERTS, :], rb_ref, metat_ref, tile_n_ref, tile_cnt_ref, cnt_ref)


def _full(shape):
    nd = len(shape)
    return pl.BlockSpec(shape, lambda i, _nd=nd: (0,) * _nd, pipeline_mode=pl.Buffered(1))


def _layer_of(stacked, layer):
    nd = stacked.ndim - 1
    return pl.BlockSpec((None,) + stacked.shape[1:], lambda i, _nd=nd: (layer,) + (0,) * _nd,
                        pipeline_mode=pl.Buffered(1))


def _mixer_layer(layer, x, w_in_t, w_out, stacked_small, rw_split, rb_col):
    batch, seq, _ = x.shape
    tm = MIX_TM
    steps_per_seq = seq // tm
    n_steps = batch // STREAMS * steps_per_seq
    rows = STREAMS * tm
    kernel = functools.partial(_mixer_kernel, tm=tm, steps_per_seq=steps_per_seq)
    shared = (rw_split, rb_col)
    tile_spec = pl.BlockSpec((None, N_EXPERTS, LANES), lambda i: (i, 0, 0))
    step_spec = pl.BlockSpec((STREAMS, tm, D_MODEL), lambda i: (i // steps_per_seq, i % steps_per_seq, 0))
    return pl.pallas_call(
        kernel,
        out_shape=(
            jax.ShapeDtypeStruct(x.shape, F32),
            jax.ShapeDtypeStruct((n_steps, SUBLANES, rows), F32),
            jax.ShapeDtypeStruct((n_steps, N_EXPERTS, LANES), jnp.int32),
            jax.ShapeDtypeStruct((n_steps, N_EXPERTS, LANES), jnp.int32),
        ),
        grid=(n_steps,),
        in_specs=[step_spec] + [_layer_of(w, layer) for w in (w_in_t, w_out) + tuple(stacked_small)]
        + [_full(w.shape) for w in shared],
        out_specs=(step_spec, pl.BlockSpec((None, SUBLANES, rows), lambda i: (i, 0, 0)), tile_spec, tile_spec),
        scratch_shapes=[
            pltpu.VMEM((D_MODEL, P_ROW), BF16),
            pltpu.VMEM((P_COL_PAD, D_MODEL), BF16),
            pltpu.VMEM((D_MODEL, D_MODEL), BF16),
            pltpu.VMEM((rows, P_ROW), F32),
            pltpu.VMEM((P_COL_PAD, rows), F32),
            pltpu.VMEM((rows, D_MODEL), BF16),
            pltpu.VMEM((STREAMS * ML_HEADS, ML_HEAD_DIM, 2 * ML_HEAD_DIM), F32),
            pltpu.VMEM((STREAMS * SUBLANES, LANES), F32),
            pltpu.VMEM((STREAMS, tm + CONV_HALO, C_WIDTH), F32),
            pltpu.VMEM((STREAMS, SUBLANES - 1, tm + CONV_SPAN, C_WIDTH), F32),
            pltpu.VMEM((N_EXPERTS, LANES), F32),
        ],
        compiler_params=pltpu.CompilerParams(
            dimension_semantics=("arbitrary",), vmem_limit_bytes=VMEM_LIMIT),
        name="mixer",
    )(x, w_in_t, w_out, *stacked_small, *shared)


def _pow2_pieces(n, largest, act):
    piece = largest
    if largest >= LONG_PIECE:
        def long_piece(j, carry):
            act(j * LONG_PIECE, LONG_PIECE)
            return carry

        lax.fori_loop(0, lax.shift_right_logical(n, LONG_PIECE.bit_length() - 1), long_piece, jnp.int32(0))
        piece = LONG_PIECE // 2
    while piece >= ROW_ALIGN:
        start = jnp.bitwise_and(n, -2 * piece)

        @pl.when(jnp.bitwise_and(n, piece) != 0)
        def _(piece=piece, start=start):
            act(start, piece)
        piece //= 2


def _for_each_run(n_tab, cnt_tab, base_tab, tile, make_copy, act):
    def body(e, off):
        n = n_tab[tile * N_EXPERTS + e]
        base = base_tab[e] + cnt_tab[tile * N_EXPERTS + e]
        _pow2_pieces(n, MOE_TM, lambda start, size: act(make_copy(
            pl.multiple_of(off + start, ROW_ALIGN), pl.multiple_of(base + start, ROW_ALIGN), size)))
        return off + n

    lax.fori_loop(0, N_EXPERTS, body, jnp.int32(0))


def _dispatch_kernel(n_tab, cnt_tab, base_tab, fill_tab, x_ref, metat_ref, xs_hbm, comp_ref, zero_ref, sem,
                     *, n_rows):
    i = pl.program_id(0)
    last = pl.num_programs(0) - 1
    slot = lax.rem(i, 2)

    def copies(tile, slot_, act):
        def make_copy(src_row, dst_row, rows):
            return pltpu.make_async_copy(comp_ref.at[slot_, pl.ds(src_row, rows)],
                                         xs_hbm.at[pl.ds(dst_row, rows)], sem.at[slot_])
        _for_each_run(n_tab, cnt_tab, base_tab, tile, make_copy, act)

    def drain(tile, slot_):
        total = n_tab[pl.num_programs(0) * N_EXPERTS + tile]
        _pow2_pieces(total, MOE_TM, lambda start, size: pltpu.make_async_copy(
            comp_ref.at[slot_, pl.ds(0, size)], xs_hbm.at[pl.ds(0, size)], sem.at[slot_]).wait())

    def zero_fill(act):
        def zero_copy(dst_row, rows):
            return pltpu.make_async_copy(zero_ref.at[pl.ds(0, rows)],
                                         xs_hbm.at[pl.ds(pl.multiple_of(dst_row, ROW_ALIGN), rows)], sem.at[2])

        def per_expert(e, carry):
            first = fill_tab[2 * e]
            _pow2_pieces(fill_tab[2 * e + 1], ROW_TILE // 2, lambda start, size: act(zero_copy(first + start, size)))
            return carry

        lax.fori_loop(0, N_EXPERTS, per_expert, jnp.int32(0))

        def per_half_tile(k, carry):
            act(zero_copy(fill_tab[2 * N_EXPERTS] + k * (ROW_TILE // 2), ROW_TILE // 2))
            return carry

        lax.fori_loop(0, (n_rows - fill_tab[2 * N_EXPERTS]) // (ROW_TILE // 2), per_half_tile, jnp.int32(0))

    @pl.when(i == 0)
    def _():
        zero_ref[...] = jnp.zeros_like(zero_ref)
        zero_fill(lambda cp: cp.start())

    @pl.when(i >= 2)
    def _():
        drain(i - 2, slot)

    mt = metat_ref[...]
    row_f = lax.broadcasted_iota(jnp.int32, (COMPACT_ROWS, MOE_TM), 0).astype(F32)
    hit1 = row_f == mt[0:1, :]
    hit2 = row_f == mt[1:2, :]
    onehot = jnp.where(hit1, 1.0, jnp.where(hit2, 1.0, 0.0)).astype(BF16)
    x_tile = x_ref[...].reshape(MOE_TM, D_MODEL).astype(BF16)
    comp_ref[slot, :, 0:D_MODEL] = _dot(onehot, x_tile).astype(BF16)
    gate = jnp.sum(jnp.where(hit1, mt[2:3, :], jnp.where(hit2, mt[3:4, :], 0.0)), axis=-1, keepdims=True)
    lane = lax.broadcasted_iota(jnp.int32, (COMPACT_ROWS, LANES), 1)
    packed, rest = jnp.zeros((COMPACT_ROWS, LANES), F32), gate
    for piece in range(GATE_PIECES):
        head = rest.astype(BF16).astype(F32)
        packed = jnp.where(lane == piece, head, packed)
        rest = rest - head
    comp_ref[slot, :, D_MODEL:PAYLOAD_W] = packed.astype(BF16)
    copies(i, slot, lambda cp: cp.start())

    @pl.when(i == last)
    def _():
        @pl.when(i >= 1)
        def _():
            drain(i - 1, 1 - slot)
        drain(i, slot)
        zero_fill(lambda cp: cp.wait())


def _token_step_spec(seq):
    tiles_per_seq = seq // MIX_TM
    return pl.BlockSpec((STREAMS, MIX_TM, D_MODEL), lambda i, *_: (i // tiles_per_seq, i % tiles_per_seq, 0))


_ROUTE_STEP_SPEC = pl.BlockSpec((None, SUBLANES, MOE_TM), lambda i, *_: (i, 0, 0))


def _dispatch_layer(n_tab, cnt_tab, base_tab, fill_tab, x, metat, n_rows):
    batch, seq, _ = x.shape
    return pl.pallas_call(
        functools.partial(_dispatch_kernel, n_rows=n_rows),
        out_shape=jax.ShapeDtypeStruct((n_rows, PAYLOAD_W), BF16),
        grid_spec=pltpu.PrefetchScalarGridSpec(
            num_scalar_prefetch=4,
            grid=(batch * seq // MOE_TM,),
            in_specs=[_token_step_spec(seq), _ROUTE_STEP_SPEC],
            out_specs=pl.BlockSpec(memory_space=pl.ANY),
            scratch_shapes=[
                pltpu.VMEM((2, COMPACT_ROWS, PAYLOAD_W), BF16),
                pltpu.VMEM((ROW_TILE // 2, PAYLOAD_W), BF16),
                pltpu.SemaphoreType.DMA((3,)),
            ],
        ),
        compiler_params=pltpu.CompilerParams(dimension_semantics=("arbitrary",), vmem_limit_bytes=VMEM_LIMIT),
        name="dispatch",
    )(n_tab, cnt_tab, base_tab, fill_tab, x, metat)


def _expert_kernel(tile_e, tile_valid, tile_src, xs_ref, wg_ref, wu_ref, wd_ref, ys_ref, wgu_ref, wdb_ref):
    i = pl.program_id(0)
    valid = tile_valid[i]
    new_expert = jnp.logical_or(i == 0, tile_e[i] != tile_e[jnp.maximum(i - 1, 0)])

    @pl.when(jnp.logical_and(valid > 0, new_expert))
    def _():
        wgu_ref[:, 0:D_EXPERT] = wg_ref[...].astype(BF16)
        wgu_ref[:, D_EXPERT:2 * D_EXPERT] = wu_ref[...].astype(BF16)
        wdb_ref[...] = wd_ref[...].astype(BF16)

    @pl.when(valid > 0)
    def _():
        blk = ROW_TILE // EXPERT_ROW_BLOCKS
        rows = [slice(b * blk, (b + 1) * blk) for b in range(EXPERT_ROW_BLOCKS)]
        gu = [_dot(xs_ref[r, 0:D_MODEL], wgu_ref[...]) for r in rows]
        for r, gu_b in zip(rows, gu):
            g, u = gu_b[:, 0:D_EXPERT], gu_b[:, D_EXPERT:2 * D_EXPERT]
            hid = (g * jax.nn.sigmoid(g) * u).astype(BF16)
            gate = jnp.sum(xs_ref[r, D_MODEL:PAYLOAD_W].astype(F32), axis=-1, keepdims=True)
            ys_ref[r, :] = (gate * _dot(hid, wdb_ref[...])).astype(BF16)

    @pl.when(valid == 0)
    def _():
        ys_ref[...] = jnp.zeros_like(ys_ref)


def _expert_layer(layer, tile_e, tile_valid, tile_src, xs, wg, wu, wd):
    n_steps = tile_e.shape[0]
    return pl.pallas_call(
        _expert_kernel,
        out_shape=jax.ShapeDtypeStruct((xs.shape[0], D_MODEL), BF16),
        grid_spec=pltpu.PrefetchScalarGridSpec(
            num_scalar_prefetch=3,
            grid=(n_steps,),
            in_specs=[
                pl.BlockSpec((ROW_TILE, PAYLOAD_W), lambda i, e, v, src: (src[i], 0)),
                pl.BlockSpec((None, None, D_MODEL, D_EXPERT), lambda i, e, v, src: (layer, e[i], 0, 0)),
                pl.BlockSpec((None, None, D_MODEL, D_EXPERT), lambda i, e, v, src: (layer, e[i], 0, 0)),
                pl.BlockSpec((None, None, D_EXPERT, D_MODEL), lambda i, e, v, src: (layer, e[i], 0, 0)),
            ],
            out_specs=pl.BlockSpec((ROW_TILE, D_MODEL), lambda i, e, v, src: (i, 0)),
            scratch_shapes=[
                pltpu.VMEM((D_MODEL, 2 * D_EXPERT), BF16),
                pltpu.VMEM((D_EXPERT, D_MODEL), BF16),
            ],
        ),
        compiler_params=pltpu.CompilerParams(dimension_semantics=("arbitrary",), vmem_limit_bytes=VMEM_LIMIT),
        name="experts",
    )(tile_e, tile_valid, tile_src, xs, wg, wu, wd)


def _combine_kernel(n_tab, cnt_tab, base_tab, x_ref, metat_ref, ys_hbm, l2w_ref, l2b_ref, o_ref, yc_ref, sem):
    i = pl.program_id(0)
    n_steps = pl.num_programs(0)
    slot = lax.rem(i, 2)

    def copies(tile, slot_, act):
        def make_copy(buf_row, ys_row, rows):
            return pltpu.make_async_copy(ys_hbm.at[pl.ds(ys_row, rows)],
                                         yc_ref.at[slot_, pl.ds(buf_row, rows)], sem.at[slot_])
        _for_each_run(n_tab, cnt_tab, base_tab, tile, make_copy, act)

    def drain(tile, slot_):
        total = n_tab[n_steps * N_EXPERTS + tile]
        _pow2_pieces(total, MOE_TM, lambda start, size: pltpu.make_async_copy(
            ys_hbm.at[pl.ds(0, size)], yc_ref.at[slot_, pl.ds(0, size)], sem.at[slot_]).wait())

    @pl.when(i == 0)
    def _():
        yc_ref[...] = jnp.zeros_like(yc_ref)
        copies(0, 0, lambda cp: cp.start())

    @pl.when(i + 1 < n_steps)
    def _():
        copies(i + 1, 1 - slot, lambda cp: cp.start())

    drain(i, slot)

    t_r = lax.broadcasted_iota(jnp.int32, (MOE_TM, MOE_TM), 0)
    t_c = lax.broadcasted_iota(jnp.int32, (MOE_TM, MOE_TM), 1)
    as_col = lambda row: jnp.sum(jnp.where(t_r == t_c, row, 0.0), axis=-1, keepdims=True)
    col_f = lax.broadcasted_iota(jnp.int32, (MOE_TM, COMPACT_ROWS), 1).astype(F32)
    mt = metat_ref[...]
    onehot = jnp.where(col_f == as_col(mt[0:1, :]), 1.0,
                       jnp.where(col_f == as_col(mt[1:2, :]), 1.0, 0.0)).astype(BF16)
    y = _dot(onehot, yc_ref[slot])
    x = x_ref[...].reshape(MOE_TM, D_MODEL)
    o_ref[...] = _layer_norm(DEEPNORM_ALPHA * x + y, l2w_ref[...], l2b_ref[...], LN_EPS).reshape(o_ref.shape)


def _combine_layer(layer, n_tab, cnt_tab, base_tab, x, metat, ys, l2w, l2b):
    batch, seq, _ = x.shape
    ln_spec = pl.BlockSpec((None, 1, D_MODEL), lambda i, *_: (layer, 0, 0))
    return pl.pallas_call(
        _combine_kernel,
        out_shape=jax.ShapeDtypeStruct(x.shape, F32),
        grid_spec=pltpu.PrefetchScalarGridSpec(
            num_scalar_prefetch=3,
            grid=(batch * seq // MOE_TM,),
            in_specs=[
                _token_step_spec(seq),
                _ROUTE_STEP_SPEC,
                pl.BlockSpec(memory_space=pl.ANY),
                ln_spec,
                ln_spec,
            ],
            out_specs=_token_step_spec(seq),
            scratch_shapes=[pltpu.VMEM((2, COMPACT_ROWS, D_MODEL), BF16), pltpu.SemaphoreType.DMA((2,))],
        ),
        compiler_params=pltpu.CompilerParams(dimension_semantics=("arbitrary",), vmem_limit_bytes=VMEM_LIMIT),
        name="combine",
    )(n_tab, cnt_tab, base_tab, x, metat, ys, l2w, l2b)


def _expert_plan(counts, n_steps):
    tiles_e = (counts + (ROW_TILE - 1)) // ROW_TILE
    cum = jnp.cumsum(tiles_e)
    first_tile = cum - tiles_e
    total = cum[-1]
    step = jnp.arange(n_steps, dtype=jnp.int32)
    owner = (step[:, None] >= cum[None, :]).sum(axis=1).astype(jnp.int32)
    is_owner = owner[:, None] == jnp.arange(N_EXPERTS, dtype=jnp.int32)[None, :]
    local = step - jnp.where(is_owner, first_tile[None, :], 0).sum(axis=1)
    rows_left = jnp.where(is_owner, counts[None, :], 0).sum(axis=1) - local * ROW_TILE
    valid = jnp.where(step < total, jnp.clip(rows_left, 0, ROW_TILE), 0).astype(jnp.int32)
    tile_e = jnp.minimum(owner, N_EXPERTS - 1)
    base = (first_tile * ROW_TILE).astype(jnp.int32)
    fill = jnp.stack([base + counts, tiles_e * ROW_TILE - counts], axis=1).reshape(-1)
    fill_tab = jnp.concatenate([fill, (total * ROW_TILE)[None]]).astype(jnp.int32)
    src = jnp.minimum(step, jnp.maximum(total - 1, 0)).astype(jnp.int32)
    return base, tile_e, valid, src, fill_tab


def _moe_layer(layer, x, metat, tile_n, tile_cnt, wg, wu, wd, l2w, l2b):
    n_tok = x.shape[0] * x.shape[1]
    n_tiles = n_tok // MOE_TM
    n_steps = (2 * n_tok + N_EXPERTS * (ROW_ALIGN - 1) * n_tiles) // ROW_TILE + N_EXPERTS
    per_tile = tile_n[:, :, 0]
    n_tab = jnp.concatenate([per_tile.reshape(-1), per_tile.sum(axis=1)])
    cnt_tab = tile_cnt[:, :, 0].reshape(-1)
    counts = tile_cnt[-1, :, 0] + tile_n[-1, :, 0]
    base_tab, tile_e, tile_valid, tile_src, fill_tab = _expert_plan(counts, n_steps)
    xs = _dispatch_layer(n_tab, cnt_tab, base_tab, fill_tab, x, metat, n_steps * ROW_TILE)
    ys = _expert_layer(layer, tile_e, tile_valid, tile_src, xs, wg, wu, wd)
    return _combine_layer(layer, n_tab, cnt_tab, base_tab, x, metat, ys, l2w, l2b)


def kernel(x, w_in, b_in, mlstm_norm_w, gmlp_norm_w, gmlp_norm_b, gmlp_ws, gmlp_bs, conv_w, conv_b,
           conv_norm_w, conv_norm_b, w_out, ln1_w, ln1_b, router_w, router_b, w_gate, w_up, w_down,
           ln2_w, ln2_b):
    batch, seq, d = x.shape

    q_lo, k_lo, v_lo = 0, ML_WIDTH, 2 * ML_WIDTH
    gate_lo = 4 * ML_WIDTH
    rest_lo = gate_lo + 2 * ML_HEADS

    def row_part(t):
        return jnp.concatenate([t[..., q_lo:k_lo], t[..., v_lo:gate_lo], t[..., rest_lo:]], axis=-1)

    def col_part(t):
        return jnp.concatenate([t[..., k_lo:v_lo], t[..., gate_lo:rest_lo]], axis=-1)

    tril = jnp.tril(jnp.ones((CHUNK, CHUNK), gmlp_ws.dtype))
    rw_pad = jnp.pad(router_w.astype(F32), ((0, 0), (0, LANES - N_EXPERTS)))
    rw_head = rw_pad.astype(BF16)
    rw_split = jnp.concatenate([rw_head, (rw_pad - rw_head.astype(F32)).astype(BF16)], axis=1)
    rb_col = router_b.astype(F32).reshape(N_EXPERTS, 1)

    rows_of = lambda t: t.reshape(DEPTH, 1, -1)
    b_row = rows_of(row_part(b_in))
    b_col = jnp.pad(col_part(b_in), ((0, 0), (0, P_COL_PAD - P_COL))).reshape(DEPTH, P_COL_PAD, 1)
    ws_cat = jnp.transpose(gmlp_ws * tril, (0, 2, 1, 3)).reshape(DEPTH, CHUNK, G_HEADS * CHUNK).astype(BF16)
    bs_full = jnp.repeat(jnp.transpose(gmlp_bs, (0, 2, 1)), G_HEAD_DIM, axis=2)
    cw = jnp.pad(conv_w, ((0, 0), (0, 1), (0, 0)))
    mixer_small = (b_row, b_col, rows_of(mlstm_norm_w), rows_of(gmlp_norm_w), rows_of(gmlp_norm_b), ws_cat, bs_full,
                   cw, rows_of(conv_b), rows_of(conv_norm_w), rows_of(conv_norm_b), rows_of(ln1_w), rows_of(ln1_b))

    w_in_t = jnp.swapaxes(w_in, 1, 2)

    for l in range(DEPTH):
        x, metat, tile_n, tile_cnt = _mixer_layer(l, x, w_in_t, w_out, mixer_small, rw_split, rb_col)
        x = _moe_layer(l, x, metat, tile_n, tile_cnt, w_gate, w_up, w_down, rows_of(ln2_w), rows_of(ln2_b))
    return x
```

```python
import functools

import jax
import jax.numpy as jnp
from jax import lax
from jax.experimental import pallas as pl
from jax.experimental.pallas import tpu as pltpu

D_MODEL = 1024
DEPTH = 4
ML_WIDTH = 512
ML_HEADS = 4
ML_HEAD_DIM = 128
CHUNK = 128
G_WIDTH = 256
G_HEADS = 4
G_HEAD_DIM = 64
C_WIDTH = 256
CONV_WIDTH = 31
N_EXPERTS = 16
N_GROUPS = 4
EXPERTS_PER_GROUP = 4
D_EXPERT = 512
DEEPNORM_ALPHA = (2.0 * DEPTH) ** 0.25
LN_EPS = 1e-5
K_SCALE = ML_HEAD_DIM ** -0.5

Q_LO, V_LO, O_LO, GU_LO, GV_LO, CA_LO, CB_LO = 0, 512, 1024, 1536, 1792, 2048, 2304
P_ROW = 2560
P_COL = ML_WIDTH + 2 * ML_HEADS
P_COL_PAD = ML_WIDTH + 16
IN_Q, IN_K, IN_VO, IN_GATES, IN_REST = (0, 512), (512, 1024), (1024, 2048), (2048, 2056), (2056, 3080)
P_IN = 3080
W_PREP_ROWS = 256

LANES = 128
SUBLANES = 8
MIX_TM = 256
ROW_ALIGN = 16
ROW_TILE = 512
EXPERT_ROW_BLOCKS = 2
XS_RING = 3
PAYLOAD_W = D_MODEL + LANES
CONV_HALO = 32
CONV_SPAN = CONV_HALO - SUBLANES
STREAMS = 2
MOE_TM = STREAMS * MIX_TM
COMPACT_ROWS = 2 * MOE_TM + 2 * LANES
assert COMPACT_ROWS >= 2 * MOE_TM + N_EXPERTS * (ROW_ALIGN - 1)
GATE_PIECES = 3
LONG_PIECE = 2 * (2 * MOE_TM // N_EXPERTS)
VMEM_LIMIT = 58 * 1024 * 1024

F32 = jnp.float32
BF16 = jnp.bfloat16
NEG_INF = float("-inf")


def _layer_norm(x, w, b, eps):
    mu = jnp.mean(x, axis=-1, keepdims=True)
    xc = x - mu
    var = jnp.mean(xc * xc, axis=-1, keepdims=True)
    return xc * lax.rsqrt(var + eps) * w + b


def _gelu_tanh(x):
    return 0.5 * x * (1.0 + jnp.tanh(0.7978845608028654 * (x + 0.044715 * (x * x * x))))


def _log_sigmoid(x):
    return jnp.minimum(x, 0.0) - jnp.log1p(jnp.exp(-jnp.abs(x)))


def _dot(a, b):
    return jnp.dot(a, b, preferred_element_type=F32)


def _router_logits(x_new, rw_ref):
    xh = x_new.astype(BF16)
    xl = (x_new - xh.astype(F32)).astype(BF16)
    head = _dot(xh, rw_ref[...])
    return head[:, 0:LANES] + head[:, LANES:2 * LANES] + _dot(xl, rw_ref[:, 0:LANES])


def _route_tile(logits, rb_ref, metat_ref, tile_n_ref, tile_cnt_ref, cnt_ref):
    tm = logits.shape[1]
    s_all = jax.nn.sigmoid(logits)
    sel_all = s_all + rb_ref[...]
    rows_of = lambda v: [v[k:k + 1, :] for k in range(N_EXPERTS)]
    s, sel = rows_of(s_all), rows_of(sel_all)

    best = None
    for g in range(N_GROUPS):
        r = sel[EXPERTS_PER_GROUP * g:EXPERTS_PER_GROUP * (g + 1)]
        hi01, lo01 = jnp.maximum(r[0], r[1]), jnp.minimum(r[0], r[1])
        hi23, lo23 = jnp.maximum(r[2], r[3]), jnp.minimum(r[2], r[3])
        score = jnp.maximum(hi01, hi23) + jnp.maximum(jnp.minimum(hi01, hi23), jnp.maximum(lo01, lo23))
        if best is None:
            best, gidx = score, jnp.zeros(score.shape, jnp.int32)
        else:
            better = score > best
            gidx = jnp.where(better, g, gidx)
            best = jnp.where(better, score, best)

    def of_group(rows):
        out = []
        for j in range(EXPERTS_PER_GROUP):
            v = rows[j]
            for g in range(1, N_GROUPS):
                v = jnp.where(gidx == g, rows[EXPERTS_PER_GROUP * g + j], v)
            out.append(v)
        return out

    def first_max(vals):
        best_v, best_j = vals[0], jnp.zeros(vals[0].shape, F32)
        for j in range(1, len(vals)):
            better = vals[j] > best_v
            best_j = jnp.where(better, float(j), best_j)
            best_v = jnp.where(better, vals[j], best_v)
        return best_j

    def take(vals, idx):
        v = vals[0]
        for j in range(1, len(vals)):
            v = jnp.where(idx == float(j), vals[j], v)
        return v

    cand, cand_s = of_group(sel), of_group(s)
    j1 = first_max(cand)
    j2 = first_max([jnp.where(j1 == float(j), NEG_INF, cand[j]) for j in range(EXPERTS_PER_GROUP)])
    g1, g2 = take(cand_s, j1), take(cand_s, j2)
    tot = g1 + g2
    first_of_group = gidx.astype(F32) * float(EXPERTS_PER_GROUP)
    e1, e2 = first_of_group + j1, first_of_group + j2

    e_f = lax.broadcasted_iota(jnp.int32, (N_EXPERTS, tm), 0).astype(F32)
    assign = jnp.where(e_f == e1, 1.0, jnp.where(e_f == e2, 1.0, 0.0)).astype(BF16)
    t_r = lax.broadcasted_iota(jnp.int32, (tm, tm), 0)
    t_c = lax.broadcasted_iota(jnp.int32, (tm, tm), 1)
    earlier = jnp.where(t_r < t_c, 1.0, 0.0).astype(BF16)
    rank = _dot(assign, earlier)
    n_b = _dot(assign, jnp.ones((tm, LANES), BF16))
    n_up_b = jnp.floor((n_b + (ROW_ALIGN - 1.0)) * (1.0 / ROW_ALIGN)) * ROW_ALIGN
    x_r = lax.broadcasted_iota(jnp.int32, (N_EXPERTS, N_EXPERTS), 0)
    x_c = lax.broadcasted_iota(jnp.int32, (N_EXPERTS, N_EXPERTS), 1)
    lower = jnp.where(x_c < x_r, 1.0, 0.0).astype(BF16)
    off_b = _dot(lower, n_up_b.astype(BF16))
    pos = rows_of(jnp.concatenate([off_b] * (tm // LANES), axis=1) + rank)
    metat_ref[...] = jnp.concatenate(
        [take(pos, e1), take(pos, e2), g1 / tot, g2 / tot, jnp.zeros((SUBLANES - 4, tm), F32)], axis=0)

    cnt = cnt_ref[...]
    tile_n_ref[...] = n_up_b.astype(jnp.int32)
    tile_cnt_ref[...] = cnt.astype(jnp.int32)
    cnt_ref[...] = cnt + n_up_b


def _project(x_tile, w_row_ref, b_row_ref, w_col_ref, b_col_ref, proj_ref, colp_ref):
    xb = x_tile.astype(BF16)
    proj_ref[...] = _dot(xb, w_row_ref[...]) + b_row_ref[...]
    colp_ref[...] = _dot(w_col_ref[...], x_tile.T.astype(BF16)) + b_col_ref[...]


def _mix_tile(x, proj_ref, colp_ref, mnw_ref, gnw_ref, gnb_ref, ws_ref, bs_ref, cw_ref, cb_ref, cnw_ref,
              cnb_ref, w_out_ref, l1w_ref, l1b_ref, hcat_ref, cstate_ref, mstate_ref, cbuf_ref, shift_ref, tm):
    n_chunks = tm // CHUNK
    gates = colp_ref[ML_WIDTH:P_COL, :]
    lane_in_chunk = lax.broadcasted_iota(jnp.int32, gates.shape, 1) % CHUNK
    logf_all = _log_sigmoid(gates)
    bcum = logf_all
    d = 1
    while d < CHUNK:
        bcum = bcum + jnp.where(lane_in_chunk >= d, pltpu.roll(bcum, d, 1), 0.0)
        d *= 2

    row_i = lax.broadcasted_iota(jnp.int32, (CHUNK, CHUNK), 0)
    col_i = lax.broadcasted_iota(jnp.int32, (CHUNK, CHUNK), 1)
    causal = col_i <= row_i
    diag = col_i == row_i
    ones_col = jnp.where(col_i == 0, 1.0, 0.0).astype(BF16)

    for sc in range(STREAMS * n_chunks):
        st, c = divmod(sc, n_chunks)
        r0 = st * tm + c * CHUNK
        rows = slice(r0, r0 + CHUNK)
        m_rows = slice(st * SUBLANES, st * SUBLANES + ML_HEADS)
        logi = gates[0:ML_HEADS, r0:r0 + CHUNK]
        logf = logf_all[ML_HEADS:2 * ML_HEADS, r0:r0 + CHUNK]
        b_row = bcum[ML_HEADS:2 * ML_HEADS, r0:r0 + CHUNK]
        g = jnp.sum(logf, axis=-1, keepdims=True)
        m_prev = mstate_ref[m_rows, 0:1]
        a_row = g - b_row + logi
        m_new = jnp.maximum(g + m_prev, jnp.max(a_row, axis=-1, keepdims=True))
        w_row = jnp.exp(a_row - m_new) * K_SCALE
        decay = jnp.exp(g + m_prev - m_new)
        mstate_ref[m_rows, :] = jnp.broadcast_to(m_new, (ML_HEADS, LANES))

        for h in range(ML_HEADS):
            hs = slice(h * ML_HEAD_DIM, (h + 1) * ML_HEAD_DIM)
            q = proj_ref[rows, Q_LO + h * ML_HEAD_DIM:Q_LO + (h + 1) * ML_HEAD_DIM].astype(BF16)
            v = proj_ref[rows, V_LO + h * ML_HEAD_DIM:V_LO + (h + 1) * ML_HEAD_DIM].astype(BF16)
            o_gate = proj_ref[rows, O_LO + h * ML_HEAD_DIM:O_LO + (h + 1) * ML_HEAD_DIM]
            kt = colp_ref[hs, r0:r0 + CHUNK]
            v_ext = jnp.concatenate([v, ones_col], axis=1)

            b_r = b_row[h:h + 1, :]
            b_c = jnp.sum(jnp.where(diag, b_r, 0.0), axis=-1, keepdims=True)
            log_d = jnp.where(causal, b_c - b_r + logi[h:h + 1, :], NEG_INF)
            mp = m_prev[h:h + 1, :]
            log_inter = b_c + mp
            m_row = jnp.maximum(log_inter, jnp.max(log_d, axis=-1, keepdims=True))
            c_ext = cstate_ref[st * ML_HEADS + h]
            from_q = _dot(q, jnp.concatenate([kt.astype(BF16), c_ext.astype(BF16)], axis=1))
            p = from_q[:, 0:CHUNK] * (jnp.exp(log_d - m_row) * K_SCALE)
            w_inter = jnp.exp(log_inter - m_row)
            ktw = (kt * w_row[h:h + 1, :]).astype(BF16)
            onto_v = _dot(jnp.concatenate([p.astype(BF16), ktw], axis=0), v_ext)
            tot = onto_v[0:CHUNK, :] + w_inter * from_q[:, CHUNK:]
            den = jnp.maximum(jnp.abs(tot[:, ML_HEAD_DIM:ML_HEAD_DIM + 1]), jnp.exp(-m_row))
            hh = tot[:, 0:ML_HEAD_DIM] / den
            cstate_ref[st * ML_HEADS + h] = decay[h:h + 1, :] * c_ext + onto_v[CHUNK:, :]

            mu = jnp.mean(hh, axis=-1, keepdims=True)
            hc = hh - mu
            var = jnp.mean(hc * hc, axis=-1, keepdims=True)
            hn = hc * lax.rsqrt(var + 1e-6) * mnw_ref[:, hs]
            hcat_ref[rows, hs] = (jax.nn.sigmoid(o_gate) * hn).astype(BF16)

        u = _gelu_tanh(proj_ref[rows, GU_LO:GU_LO + G_WIDTH])
        z = _layer_norm(_gelu_tanh(proj_ref[rows, GV_LO:GV_LO + G_WIDTH]), gnw_ref[...], gnb_ref[...], LN_EPS)
        lane_head = lax.broadcasted_iota(jnp.int32, (CHUNK, G_WIDTH), 1) // G_HEAD_DIM
        z_bd = jnp.concatenate(
            [jnp.where(lane_head == h, z, 0.0).astype(BF16) for h in range(G_HEADS)], axis=0)
        zs = _dot(ws_ref[...], z_bd) + bs_ref[...]
        hcat_ref[rows, ML_WIDTH:ML_WIDTH + G_WIDTH] = (u * zs).astype(BF16)

        ca = proj_ref[rows, CA_LO:CA_LO + C_WIDTH]
        cb = proj_ref[rows, CB_LO:CB_LO + C_WIDTH]
        cbuf_ref[st, CONV_HALO + c * CHUNK:CONV_HALO + (c + 1) * CHUNK, :] = ca * jax.nn.sigmoid(cb)

    first_tap = CONV_HALO - (CONV_WIDTH - 1)
    for st in range(STREAMS):
        for sh in range(1, SUBLANES):
            shift_ref[st, sh - 1] = cbuf_ref[st, sh:sh + tm + CONV_SPAN, :]
        for c in range(n_chunks):
            acc = jnp.zeros((CHUNK, C_WIDTH), F32) + cb_ref[...]
            for k in range(CONV_WIDTH):
                whole, sh = divmod(first_tap + k, SUBLANES)
                lo = c * CHUNK + whole * SUBLANES
                tap = cbuf_ref[st, lo:lo + CHUNK, :] if sh == 0 else shift_ref[st, sh - 1, lo:lo + CHUNK, :]
                acc = acc + cw_ref[k:k + 1, :] * tap
            cn = _layer_norm(acc, cnw_ref[...], cnb_ref[...], LN_EPS)
            r0 = st * tm + c * CHUNK
            hcat_ref[r0:r0 + CHUNK, ML_WIDTH + G_WIDTH:D_MODEL] = (cn * jax.nn.sigmoid(cn)).astype(BF16)
        cbuf_ref[st, 0:CONV_HALO, :] = cbuf_ref[st, tm:tm + CONV_HALO, :]

    y = _dot(hcat_ref[...], w_out_ref[...])
    x_new = _layer_norm(DEEPNORM_ALPHA * x + y, l1w_ref[...], l1b_ref[...], LN_EPS)
    return x_new


def _prepare_in_proj(w_t_ref, w_row_ref, w_col_ref):
    for (src_lo, src_hi), dst_lo in ((IN_Q, Q_LO), (IN_VO, V_LO), (IN_REST, GU_LO)):
        for off in range(0, src_hi - src_lo, W_PREP_ROWS):
            block = w_t_ref[src_lo + off:src_lo + off + W_PREP_ROWS, :]
            w_row_ref[:, dst_lo + off:dst_lo + off + W_PREP_ROWS] = block.T.astype(BF16)
    w_col_ref[0:ML_WIDTH, :] = w_t_ref[IN_K[0]:IN_K[1], :].astype(BF16)
    gates = w_t_ref[IN_GATES[0]:IN_GATES[1], :]
    w_col_ref[ML_WIDTH:P_COL_PAD, :] = jnp.concatenate(
        [gates, jnp.zeros((P_COL_PAD - P_COL, D_MODEL), F32)], axis=0).astype(BF16)


def _mixer_kernel(x_ref, w_in_t_ref, w_out_f32_ref, b_row_ref, b_col_ref, mnw_ref, gnw_ref, gnb_ref,
                  ws_ref, bs_ref, cw_ref, cb_ref, cnw_ref, cnb_ref, l1w_ref, l1b_ref, rw_ref, rb_ref,
                  o_ref, metat_ref, tile_n_ref, tile_cnt_ref,
                  w_row_ref, w_col_ref, w_out_ref, proj_ref, colp_ref, hcat_ref, cstate_ref, mstate_ref, cbuf_ref, shift_ref,
                  cnt_ref, *, tm, steps_per_seq):
    i = pl.program_id(0)

    @pl.when(i == 0)
    def _():
        cnt_ref[...] = jnp.zeros_like(cnt_ref)
        _prepare_in_proj(w_in_t_ref, w_row_ref, w_col_ref)
        for r0 in range(0, D_MODEL, W_PREP_ROWS):
            w_out_ref[r0:r0 + W_PREP_ROWS, :] = w_out_f32_ref[r0:r0 + W_PREP_ROWS, :].astype(BF16)

    @pl.when(i % steps_per_seq == 0)
    def _():
        cstate_ref[...] = jnp.zeros_like(cstate_ref)
        mstate_ref[...] = jnp.zeros_like(mstate_ref)
        cbuf_ref[:, 0:CONV_HALO, :] = jnp.zeros((STREAMS, CONV_HALO, C_WIDTH), F32)

    x = x_ref[...].reshape(STREAMS * tm, D_MODEL)
    _project(x, w_row_ref, b_row_ref, w_col_ref, b_col_ref, proj_ref, colp_ref)
    x_new = _mix_tile(x, proj_ref, colp_ref, mnw_ref, gnw_ref, gnb_ref, ws_ref, bs_ref, cw_ref, cb_ref, cnw_ref,
                      cnb_ref, w_out_ref, l1w_ref, l1b_ref, hcat_ref, cstate_ref, mstate_ref, cbuf_ref, shift_ref,
                      tm)
    o_ref[...] = x_new.reshape(STREAMS, tm, D_MODEL)
    _route_tile(_router_logits(x_new, rw_ref).T[0:N_EXPERTS, :], rb_ref, metat_ref, tile_n_ref, tile_cnt_ref, cnt_ref)


def _full(shape):
    nd = len(shape)
    return pl.BlockSpec(shape, lambda i, _nd=nd: (0,) * _nd, pipeline_mode=pl.Buffered(1))


def _layer_of(stacked, layer):
    nd = stacked.ndim - 1
    return pl.BlockSpec((None,) + stacked.shape[1:], lambda i, _nd=nd: (layer,) + (0,) * _nd,
                        pipeline_mode=pl.Buffered(1))


def _mixer_layer(layer, x, w_in_t, w_out, stacked_small, rw_split, rb_col):
    batch, seq, _ = x.shape
    tm = MIX_TM
    steps_per_seq = seq // tm
    n_steps = batch // STREAMS * steps_per_seq
    rows = STREAMS * tm
    kernel = functools.partial(_mixer_kernel, tm=tm, steps_per_seq=steps_per_seq)
    shared = (rw_split, rb_col)
    tile_spec = pl.BlockSpec((None, N_EXPERTS, LANES), lambda i: (i, 0, 0))
    step_spec = pl.BlockSpec((STREAMS, tm, D_MODEL), lambda i: (i // steps_per_seq, i % steps_per_seq, 0))
    return pl.pallas_call(
        kernel,
        out_shape=(
            jax.ShapeDtypeStruct(x.shape, F32),
            jax.ShapeDtypeStruct((n_steps, SUBLANES, rows), F32),
            jax.ShapeDtypeStruct((n_steps, N_EXPERTS, LANES), jnp.int32),
            jax.ShapeDtypeStruct((n_steps, N_EXPERTS, LANES), jnp.int32),
        ),
        grid=(n_steps,),
        in_specs=[step_spec] + [_layer_of(w, layer) for w in (w_in_t, w_out) + tuple(stacked_small)]
        + [_full(w.shape) for w in shared],
        out_specs=(step_spec, pl.BlockSpec((None, SUBLANES, rows), lambda i: (i, 0, 0)), tile_spec, tile_spec),
        scratch_shapes=[
            pltpu.VMEM((D_MODEL, P_ROW), BF16),
            pltpu.VMEM((P_COL_PAD, D_MODEL), BF16),
            pltpu.VMEM((D_MODEL, D_MODEL), BF16),
            pltpu.VMEM((rows, P_ROW), F32),
            pltpu.VMEM((P_COL_PAD, rows), F32),
            pltpu.VMEM((rows, D_MODEL), BF16),
            pltpu.VMEM((STREAMS * ML_HEADS, ML_HEAD_DIM, 2 * ML_HEAD_DIM), F32),
            pltpu.VMEM((STREAMS * SUBLANES, LANES), F32),
            pltpu.VMEM((STREAMS, tm + CONV_HALO, C_WIDTH), F32),
            pltpu.VMEM((STREAMS, SUBLANES - 1, tm + CONV_SPAN, C_WIDTH), F32),
            pltpu.VMEM((N_EXPERTS, LANES), F32),
        ],
        compiler_params=pltpu.CompilerParams(
            dimension_semantics=("arbitrary",), vmem_limit_bytes=VMEM_LIMIT),
        name="mixer",
    )(x, w_in_t, w_out, *stacked_small, *shared)


def _pow2_pieces(n, largest, act):
    piece = largest
    if largest >= LONG_PIECE:
        def long_piece(j, carry):
            act(j * LONG_PIECE, LONG_PIECE)
            return carry

        lax.fori_loop(0, lax.shift_right_logical(n, LONG_PIECE.bit_length() - 1), long_piece, jnp.int32(0))
        piece = LONG_PIECE // 2
    while piece >= ROW_ALIGN:
        start = jnp.bitwise_and(n, -2 * piece)

        @pl.when(jnp.bitwise_and(n, piece) != 0)
        def _(piece=piece, start=start):
            act(start, piece)
        piece //= 2


def _for_each_run(n_tab, cnt_tab, base_tab, tile, make_copy, act):
    def body(e, off):
        n = n_tab[tile * N_EXPERTS + e]
        base = base_tab[e] + cnt_tab[tile * N_EXPERTS + e]
        _pow2_pieces(n, MOE_TM, lambda start, size: act(make_copy(
            pl.multiple_of(off + start, ROW_ALIGN), pl.multiple_of(base + start, ROW_ALIGN), size)))
        return off + n

    lax.fori_loop(0, N_EXPERTS, body, jnp.int32(0))


def _dispatch_kernel(n_tab, cnt_tab, base_tab, fill_tab, x_ref, metat_ref, xs_hbm, comp_ref, zero_ref, sem,
                     *, n_rows):
    i = pl.program_id(0)
    last = pl.num_programs(0) - 1
    slot = lax.rem(i, 2)

    def copies(tile, slot_, act):
        def make_copy(src_row, dst_row, rows):
            return pltpu.make_async_copy(comp_ref.at[slot_, pl.ds(src_row, rows)],
                                         xs_hbm.at[pl.ds(dst_row, rows)], sem.at[slot_])
        _for_each_run(n_tab, cnt_tab, base_tab, tile, make_copy, act)

    def drain(tile, slot_):
        total = n_tab[pl.num_programs(0) * N_EXPERTS + tile]
        _pow2_pieces(total, MOE_TM, lambda start, size: pltpu.make_async_copy(
            comp_ref.at[slot_, pl.ds(0, size)], xs_hbm.at[pl.ds(0, size)], sem.at[slot_]).wait())

    def zero_fill(act):
        def zero_copy(dst_row, rows):
            return pltpu.make_async_copy(zero_ref.at[pl.ds(0, rows)],
                                         xs_hbm.at[pl.ds(pl.multiple_of(dst_row, ROW_ALIGN), rows)], sem.at[2])

        def per_expert(e, carry):
            first = fill_tab[2 * e]
            _pow2_pieces(fill_tab[2 * e + 1], ROW_TILE // 2, lambda start, size: act(zero_copy(first + start, size)))
            return carry

        lax.fori_loop(0, N_EXPERTS, per_expert, jnp.int32(0))

        def per_half_tile(k, carry):
            act(zero_copy(fill_tab[2 * N_EXPERTS] + k * (ROW_TILE // 2), ROW_TILE // 2))
            return carry

        lax.fori_loop(0, (n_rows - fill_tab[2 * N_EXPERTS]) // (ROW_TILE // 2), per_half_tile, jnp.int32(0))

    @pl.when(i == 0)
    def _():
        zero_ref[...] = jnp.zeros_like(zero_ref)
        zero_fill(lambda cp: cp.start())

    @pl.when(i >= 2)
    def _():
        drain(i - 2, slot)

    mt = metat_ref[...]
    row_f = lax.broadcasted_iota(jnp.int32, (COMPACT_ROWS, MOE_TM), 0).astype(F32)
    hit1 = row_f == mt[0:1, :]
    hit2 = row_f == mt[1:2, :]
    onehot = jnp.where(hit1, 1.0, jnp.where(hit2, 1.0, 0.0)).astype(BF16)
    x_tile = x_ref[...].reshape(MOE_TM, D_MODEL).astype(BF16)
    comp_ref[slot, :, 0:D_MODEL] = _dot(onehot, x_tile).astype(BF16)
    gate = jnp.sum(jnp.where(hit1, mt[2:3, :], jnp.where(hit2, mt[3:4, :], 0.0)), axis=-1, keepdims=True)
    lane = lax.broadcasted_iota(jnp.int32, (COMPACT_ROWS, LANES), 1)
    packed, rest = jnp.zeros((COMPACT_ROWS, LANES), F32), gate
    for piece in range(GATE_PIECES):
        head = rest.astype(BF16).astype(F32)
        packed = jnp.where(lane == piece, head, packed)
        rest = rest - head
    comp_ref[slot, :, D_MODEL:PAYLOAD_W] = packed.astype(BF16)
    copies(i, slot, lambda cp: cp.start())

    @pl.when(i == last)
    def _():
        @pl.when(i >= 1)
        def _():
            drain(i - 1, 1 - slot)
        drain(i, slot)
        zero_fill(lambda cp: cp.wait())


def _token_step_spec(seq):
    tiles_per_seq = seq // MIX_TM
    return pl.BlockSpec((STREAMS, MIX_TM, D_MODEL), lambda i, *_: (i // tiles_per_seq, i % tiles_per_seq, 0))


_ROUTE_STEP_SPEC = pl.BlockSpec((None, SUBLANES, MOE_TM), lambda i, *_: (i, 0, 0))


def _dispatch_layer(n_tab, cnt_tab, base_tab, fill_tab, x, metat, n_rows):
    batch, seq, _ = x.shape
    return pl.pallas_call(
        functools.partial(_dispatch_kernel, n_rows=n_rows),
        out_shape=jax.ShapeDtypeStruct((n_rows, PAYLOAD_W), BF16),
        grid_spec=pltpu.PrefetchScalarGridSpec(
            num_scalar_prefetch=4,
            grid=(batch * seq // MOE_TM,),
            in_specs=[_token_step_spec(seq), _ROUTE_STEP_SPEC],
            out_specs=pl.BlockSpec(memory_space=pl.ANY),
            scratch_shapes=[
                pltpu.VMEM((2, COMPACT_ROWS, PAYLOAD_W), BF16),
                pltpu.VMEM((ROW_TILE // 2, PAYLOAD_W), BF16),
                pltpu.SemaphoreType.DMA((3,)),
            ],
        ),
        compiler_params=pltpu.CompilerParams(dimension_semantics=("arbitrary",), vmem_limit_bytes=VMEM_LIMIT),
        name="dispatch",
    )(n_tab, cnt_tab, base_tab, fill_tab, x, metat)


def _expert_kernel(tile_e, tile_valid, tile_src, xs_hbm, wg_ref, wu_ref, wd_ref, ys_ref, wgu_ref, wdb_ref,
                   ring_ref, ring_sem):
    i = pl.program_id(0)
    n_steps = pl.num_programs(0)
    valid = tile_valid[i]
    new_expert = jnp.logical_or(i == 0, tile_e[i] != tile_e[jnp.maximum(i - 1, 0)])

    def tile_copy(step):
        slot = lax.rem(step, XS_RING)
        first = pl.multiple_of(tile_src[step] * ROW_TILE, ROW_TILE)
        return pltpu.make_async_copy(xs_hbm.at[pl.ds(first, ROW_TILE)], ring_ref.at[slot], ring_sem.at[slot])

    def start_tile(step):
        @pl.when(jnp.logical_and(step < n_steps, tile_valid[jnp.minimum(step, n_steps - 1)] > 0))
        def _():
            tile_copy(jnp.minimum(step, n_steps - 1)).start()

    @pl.when(i == 0)
    def _():
        for ahead in range(XS_RING - 1):
            start_tile(i + ahead)

    start_tile(i + XS_RING - 1)

    @pl.when(valid > 0)
    def _():
        tile_copy(i).wait()

    xs_ref = ring_ref.at[lax.rem(i, XS_RING)]

    @pl.when(jnp.logical_and(valid > 0, new_expert))
    def _():
        wgu_ref[:, 0:D_EXPERT] = wg_ref[...].astype(BF16)
        wgu_ref[:, D_EXPERT:2 * D_EXPERT] = wu_ref[...].astype(BF16)
        wdb_ref[...] = wd_ref[...].astype(BF16)

    @pl.when(valid > 0)
    def _():
        blk = ROW_TILE // EXPERT_ROW_BLOCKS
        rows = [slice(b * blk, (b + 1) * blk) for b in range(EXPERT_ROW_BLOCKS)]
        gu = [_dot(xs_ref[r, 0:D_MODEL], wgu_ref[...]) for r in rows]
        for r, gu_b in zip(rows, gu):
            g, u = gu_b[:, 0:D_EXPERT], gu_b[:, D_EXPERT:2 * D_EXPERT]
            hid = (g * jax.nn.sigmoid(g) * u).astype(BF16)
            gate = jnp.sum(xs_ref[r, D_MODEL:PAYLOAD_W].astype(F32), axis=-1, keepdims=True)
            ys_ref[r, :] = (gate * _dot(hid, wdb_ref[...])).astype(BF16)

    @pl.when(valid == 0)
    def _():
        ys_ref[...] = jnp.zeros_like(ys_ref)


def _expert_layer(layer, tile_e, tile_valid, tile_src, xs, wg, wu, wd):
    n_steps = tile_e.shape[0]
    return pl.pallas_call(
        _expert_kernel,
        out_shape=jax.ShapeDtypeStruct((xs.shape[0], D_MODEL), BF16),
        grid_spec=pltpu.PrefetchScalarGridSpec(
            num_scalar_prefetch=3,
            grid=(n_steps,),
            in_specs=[
                pl.BlockSpec(memory_space=pl.ANY),
                pl.BlockSpec((None, None, D_MODEL, D_EXPERT), lambda i, e, v, src: (layer, e[i], 0, 0)),
                pl.BlockSpec((None, None, D_MODEL, D_EXPERT), lambda i, e, v, src: (layer, e[i], 0, 0)),
                pl.BlockSpec((None, None, D_EXPERT, D_MODEL), lambda i, e, v, src: (layer, e[i], 0, 0)),
            ],
            out_specs=pl.BlockSpec((ROW_TILE, D_MODEL), lambda i, e, v, src: (i, 0)),
            scratch_shapes=[
                pltpu.VMEM((D_MODEL, 2 * D_EXPERT), BF16),
                pltpu.VMEM((D_EXPERT, D_MODEL), BF16),
                pltpu.VMEM((XS_RING, ROW_TILE, PAYLOAD_W), BF16),
                pltpu.SemaphoreType.DMA((XS_RING,)),
            ],
        ),
        compiler_params=pltpu.CompilerParams(dimension_semantics=("arbitrary",), vmem_limit_bytes=VMEM_LIMIT),
        name="experts",
    )(tile_e, tile_valid, tile_src, xs, wg, wu, wd)


def _combine_kernel(n_tab, cnt_tab, base_tab, x_ref, metat_ref, ys_hbm, l2w_ref, l2b_ref, o_ref, yc_ref, sem):
    i = pl.program_id(0)
    n_steps = pl.num_programs(0)
    slot = lax.rem(i, 2)

    def copies(tile, slot_, act):
        def make_copy(buf_row, ys_row, rows):
            return pltpu.make_async_copy(ys_hbm.at[pl.ds(ys_row, rows)],
                                         yc_ref.at[slot_, pl.ds(buf_row, rows)], sem.at[slot_])
        _for_each_run(n_tab, cnt_tab, base_tab, tile, make_copy, act)

    def drain(tile, slot_):
        total = n_tab[n_steps * N_EXPERTS + tile]
        _pow2_pieces(total, MOE_TM, lambda start, size: pltpu.make_async_copy(
            ys_hbm.at[pl.ds(0, size)], yc_ref.at[slot_, pl.ds(0, size)], sem.at[slot_]).wait())

    @pl.when(i == 0)
    def _():
        yc_ref[...] = jnp.zeros_like(yc_ref)
        copies(0, 0, lambda cp: cp.start())

    @pl.when(i + 1 < n_steps)
    def _():
        copies(i + 1, 1 - slot, lambda cp: cp.start())

    drain(i, slot)

    t_r = lax.broadcasted_iota(jnp.int32, (MOE_TM, MOE_TM), 0)
    t_c = lax.broadcasted_iota(jnp.int32, (MOE_TM, MOE_TM), 1)
    as_col = lambda row: jnp.sum(jnp.where(t_r == t_c, row, 0.0), axis=-1, keepdims=True)
    col_f = lax.broadcasted_iota(jnp.int32, (MOE_TM, COMPACT_ROWS), 1).astype(F32)
    mt = metat_ref[...]
    onehot = jnp.where(col_f == as_col(mt[0:1, :]), 1.0,
                       jnp.where(col_f == as_col(mt[1:2, :]), 1.0, 0.0)).astype(BF16)
    y = _dot(onehot, yc_ref[slot])
    x = x_ref[...].reshape(MOE_TM, D_MODEL)
    o_ref[...] = _layer_norm(DEEPNORM_ALPHA * x + y, l2w_ref[...], l2b_ref[...], LN_EPS).reshape(o_ref.shape)


def _combine_layer(layer, n_tab, cnt_tab, base_tab, x, metat, ys, l2w, l2b):
    batch, seq, _ = x.shape
    ln_spec = pl.BlockSpec((None, 1, D_MODEL), lambda i, *_: (layer, 0, 0))
    return pl.pallas_call(
        _combine_kernel,
        out_shape=jax.ShapeDtypeStruct(x.shape, F32),
        grid_spec=pltpu.PrefetchScalarGridSpec(
            num_scalar_prefetch=3,
            grid=(batch * seq // MOE_TM,),
            in_specs=[
                _token_step_spec(seq),
                _ROUTE_STEP_SPEC,
                pl.BlockSpec(memory_space=pl.ANY),
                ln_spec,
                ln_spec,
            ],
            out_specs=_token_step_spec(seq),
            scratch_shapes=[pltpu.VMEM((2, COMPACT_ROWS, D_MODEL), BF16), pltpu.SemaphoreType.DMA((2,))],
        ),
        compiler_params=pltpu.CompilerParams(dimension_semantics=("arbitrary",), vmem_limit_bytes=VMEM_LIMIT),
        name="combine",
    )(n_tab, cnt_tab, base_tab, x, metat, ys, l2w, l2b)


def _expert_plan(counts, n_steps):
    tiles_e = (counts + (ROW_TILE - 1)) // ROW_TILE
    cum = jnp.cumsum(tiles_e)
    first_tile = cum - tiles_e
    total = cum[-1]
    step = jnp.arange(n_steps, dtype=jnp.int32)
    owner = (step[:, None] >= cum[None, :]).sum(axis=1).astype(jnp.int32)
    is_owner = owner[:, None] == jnp.arange(N_EXPERTS, dtype=jnp.int32)[None, :]
    local = step - jnp.where(is_owner, first_tile[None, :], 0).sum(axis=1)
    rows_left = jnp.where(is_owner, counts[None, :], 0).sum(axis=1) - local * ROW_TILE
    valid = jnp.where(step < total, jnp.clip(rows_left, 0, ROW_TILE), 0).astype(jnp.int32)
    tile_e = jnp.minimum(owner, N_EXPERTS - 1)
    base = (first_tile * ROW_TILE).astype(jnp.int32)
    fill = jnp.stack([base + counts, tiles_e * ROW_TILE - counts], axis=1).reshape(-1)
    fill_tab = jnp.concatenate([fill, (total * ROW_TILE)[None]]).astype(jnp.int32)
    src = jnp.minimum(step, jnp.maximum(total - 1, 0)).astype(jnp.int32)
    return base, tile_e, valid, src, fill_tab


def _moe_layer(layer, x, metat, tile_n, tile_cnt, wg, wu, wd, l2w, l2b):
    n_tok = x.shape[0] * x.shape[1]
    n_tiles = n_tok // MOE_TM
    n_steps = (2 * n_tok + N_EXPERTS * (ROW_ALIGN - 1) * n_tiles) // ROW_TILE + N_EXPERTS
    per_tile = tile_n[:, :, 0]
    n_tab = jnp.concatenate([per_tile.reshape(-1), per_tile.sum(axis=1)])
    cnt_tab = tile_cnt[:, :, 0].reshape(-1)
    counts = tile_cnt[-1, :, 0] + tile_n[-1, :, 0]
    base_tab, tile_e, tile_valid, tile_src, fill_tab = _expert_plan(counts, n_steps)
    xs = _dispatch_layer(n_tab, cnt_tab, base_tab, fill_tab, x, metat, n_steps * ROW_TILE)
    ys = _expert_layer(layer, tile_e, tile_valid, tile_src, xs, wg, wu, wd)
    return _combine_layer(layer, n_tab, cnt_tab, base_tab, x, metat, ys, l2w, l2b)


def kernel(x, w_in, b_in, mlstm_norm_w, gmlp_norm_w, gmlp_norm_b, gmlp_ws, gmlp_bs, conv_w, conv_b,
           conv_norm_w, conv_norm_b, w_out, ln1_w, ln1_b, router_w, router_b, w_gate, w_up, w_down,
           ln2_w, ln2_b):
    batch, seq, d = x.shape

    q_lo, k_lo, v_lo = 0, ML_WIDTH, 2 * ML_WIDTH
    gate_lo = 4 * ML_WIDTH
    rest_lo = gate_lo + 2 * ML_HEADS

    def row_part(t):
        return jnp.concatenate([t[..., q_lo:k_lo], t[..., v_lo:gate_lo], t[..., rest_lo:]], axis=-1)

    def col_part(t):
        return jnp.concatenate([t[..., k_lo:v_lo], t[..., gate_lo:rest_lo]], axis=-1)

    tril = jnp.tril(jnp.ones((CHUNK, CHUNK), gmlp_ws.dtype))
    rw_pad = jnp.pad(router_w.astype(F32), ((0, 0), (0, LANES - N_EXPERTS)))
    rw_head = rw_pad.astype(BF16)
    rw_split = jnp.concatenate([rw_head, (rw_pad - rw_head.astype(F32)).astype(BF16)], axis=1)
    rb_col = router_b.astype(F32).reshape(N_EXPERTS, 1)

    rows_of = lambda t: t.reshape(DEPTH, 1, -1)
    b_row = rows_of(row_part(b_in))
    b_col = jnp.pad(col_part(b_in), ((0, 0), (0, P_COL_PAD - P_COL))).reshape(DEPTH, P_COL_PAD, 1)
    ws_cat = jnp.transpose(gmlp_ws * tril, (0, 2, 1, 3)).reshape(DEPTH, CHUNK, G_HEADS * CHUNK).astype(BF16)
    bs_full = jnp.repeat(jnp.transpose(gmlp_bs, (0, 2, 1)), G_HEAD_DIM, axis=2)
    cw = jnp.pad(conv_w, ((0, 0), (0, 1), (0, 0)))
    mixer_small = (b_row, b_col, rows_of(mlstm_norm_w), rows_of(gmlp_norm_w), rows_of(gmlp_norm_b), ws_cat, bs_full,
                   cw, rows_of(conv_b), rows_of(conv_norm_w), rows_of(conv_norm_b), rows_of(ln1_w), rows_of(ln1_b))

    w_in_t = jnp.swapaxes(w_in, 1, 2)

    for l in range(DEPTH):
        x, metat, tile_n, tile_cnt = _mixer_layer(l, x, w_in_t, w_out, mixer_small, rw_split, rb_col)
        x = _moe_layer(l, x, metat, tile_n, tile_cnt, w_gate, w_up, w_down, rows_of(ln2_w), rows_of(ln2_b))
    return x
```
